```python
import jax
import jax.numpy as jnp
from jax import lax
import numpy as np

D_MODEL = 2048
BATCH = 1
SEQ = 8192
DEPTH = 2

MIX_WIDTH = D_MODEL
ATTN_WIDTH = MIX_WIDTH // 2
RWKV_WIDTH = MIX_WIDTH - ATTN_WIDTH
HEAD_DIM = 64
ATTN_HEADS = ATTN_WIDTH // HEAD_DIM
ATTN_KV_HEADS = 4
ATTN_GROUP = ATTN_HEADS // ATTN_KV_HEADS
KV_WIDTH = ATTN_KV_HEADS * HEAD_DIM
WINDOW = 128
ATTN_BLOCK = WINDOW
ROPE_THETA = 10000.0
RWKV_HEAD = 64
RWKV_HEADS = RWKV_WIDTH // RWKV_HEAD
DECAY_LORA = max(32, int(round(1.8 * RWKV_WIDTH ** 0.5 / 32)) * 32)
AAA_LORA = max(32, int(round(1.8 * RWKV_WIDTH ** 0.5 / 32)) * 32)
MV_LORA = max(32, int(round(1.3 * RWKV_WIDTH ** 0.5 / 32)) * 32)
GATE_LORA = max(32, int(round(0.6 * RWKV_WIDTH ** 0.8 / 32)) * 32)
RWKV_GN_EPS = 64e-5
FFN_HIDDEN = -(-8 * D_MODEL // (3 * 256)) * 256
LN_EPS = 1e-5
N_MOD = 6
DEEPNORM_ALPHA = (2.0 * DEPTH) ** 0.25
DEEPNORM_BETA = (8.0 * DEPTH) ** -0.25

ATTN_COLS = ATTN_WIDTH + 2 * KV_WIDTH
RWKV_SPLITS = (RWKV_WIDTH, 2 * RWKV_WIDTH, 3 * RWKV_WIDTH,
               3 * RWKV_WIDTH + DECAY_LORA, 3 * RWKV_WIDTH + DECAY_LORA + AAA_LORA)
RWKV_COLS = 3 * RWKV_WIDTH + DECAY_LORA + AAA_LORA + GATE_LORA
N_IN = ATTN_COLS + RWKV_COLS

kernel_name = "hymba_swa_sink_rwkv7_deepnorm_adaln"


def layer_norm(z, w, b):
    z32 = z.astype(jnp.float32)
    mu = jnp.mean(z32, axis=-1, keepdims=True)
    var = jnp.mean(jnp.square(z32 - mu), axis=-1, keepdims=True)
    return ((z32 - mu) * lax.rsqrt(var + LN_EPS) * w + b).astype(z.dtype)


def rope_tables(positions):
    inv = ROPE_THETA ** (-jnp.arange(0, HEAD_DIM, 2, dtype=jnp.float32) / HEAD_DIM)
    ang = positions.astype(jnp.float32)[..., None] * inv
    return jnp.cos(ang), jnp.sin(ang)


def apply_rope(t, cos, sin):
    t32 = t.astype(jnp.float32)
    t1, t2 = jnp.split(t32, 2, axis=-1)
    cos = cos[:, :, None, :]
    sin = sin[:, :, None, :]
    out = jnp.concatenate([t1 * cos - t2 * sin, t2 * cos + t1 * sin], axis=-1)
    return out.astype(t.dtype)


def window_mask(n_blocks):
    qa = jnp.arange(ATTN_BLOCK)[:, None]
    ks = jnp.arange(2 * ATTN_BLOCK)[None, :]
    rel = ATTN_BLOCK + qa - ks
    band = (rel >= 0) & (rel < WINDOW)
    blk = jnp.arange(n_blocks)[:, None, None]
    return band[None] & ((blk > 0) | (ks[None] >= ATTN_BLOCK))


def sliding_window_attention(q, k, v, sinks, cos, sin):
    B, T = q.shape[:2]
    nb = T // ATTN_BLOCK
    q = apply_rope(q.reshape(B, T, ATTN_HEADS, HEAD_DIM), cos, sin)
    k = apply_rope(k.reshape(B, T, ATTN_KV_HEADS, HEAD_DIM), cos, sin)
    v = v.reshape(B, T, ATTN_KV_HEADS, HEAD_DIM)
    qb = q.reshape(B, nb, ATTN_BLOCK, ATTN_KV_HEADS, ATTN_GROUP, HEAD_DIM)

    def with_prev(t):
        tb = t.reshape(B, nb, ATTN_BLOCK, ATTN_KV_HEADS, HEAD_DIM)
        prev = jnp.pad(tb[:, :-1], ((0, 0), (1, 0), (0, 0), (0, 0), (0, 0)))
        return jnp.concatenate([prev, tb], axis=2)

    kb, vb = with_prev(k), with_prev(v)
    s = jnp.einsum('bnqkgd,bnskd->bnkgqs', qb, kb,
                   preferred_element_type=jnp.float32) * (HEAD_DIM ** -0.5)
    mask = window_mask(nb)[None, :, None, None]
    s = jnp.where(mask, s, jnp.finfo(jnp.float32).min)
    sink = jnp.broadcast_to(
        sinks.astype(jnp.float32).reshape(1, 1, ATTN_KV_HEADS, ATTN_GROUP, 1, 1),
        s.shape[:-1] + (1,))
    p = jax.nn.softmax(jnp.concatenate([s, sink], axis=-1), axis=-1)[..., :-1]
    o = jnp.einsum('bnkgqs,bnskd->bnqkgd', p.astype(vb.dtype), vb)
    return o.reshape(B, T, ATTN_WIDTH)


def token_shift(z):
    return jnp.pad(z[:, :-1], ((0, 0), (1, 0), (0, 0)))


def wkv7_scan(r, decay, k, v, a, b):
    def step(S, inp):
        r_t, w_t, k_t, v_t, a_t, b_t = inp
        sa = jnp.einsum('bhij,bhj->bhi', S, a_t)
        S = (S * w_t[:, :, None, :] + sa[..., None] * b_t[:, :, None, :]
             + v_t[..., None] * k_t[:, :, None, :])
        return S, jnp.einsum('bhij,bhj->bhi', S, r_t)

    seq = tuple(jnp.moveaxis(t, 1, 0) for t in (r, decay, k, v, a, b))
    B = r.shape[0]
    S0 = jnp.zeros((B, RWKV_HEADS, RWKV_HEAD, RWKV_HEAD), jnp.float32)
    _, out = lax.scan(step, S0, seq)
    return jnp.moveaxis(out, 0, 1)


def rwkv7_group(cols, mu, w0, w2, a0, a2, g2, k_k, k_a, r_k, gn_w, gn_b, v_first, vres):
    B, T = cols.shape[:2]
    f32 = jnp.float32
    xs = cols + (token_shift(cols) - cols) * mu
    r, k, v, wd, ad, gd = jnp.split(xs, RWKV_SPLITS, axis=-1)
    w = -jax.nn.softplus(-(w0 + jnp.tanh(wd) @ w2)) - 0.5
    a = jax.nn.sigmoid(a0 + ad @ a2)
    g = jax.nn.sigmoid(gd) @ g2
    if vres is None:
        v_first = v
    else:
        v0, v1, v2 = vres
        v = v + (v_first - v) * jax.nn.sigmoid(v0 + (v @ v1) @ v2)

    def heads(t):
        return t.reshape(B, T, RWKV_HEADS, RWKV_HEAD).astype(f32)

    kk = heads(k * k_k)
    kk = kk / jnp.maximum(jnp.sqrt(jnp.sum(kk * kk, axis=-1, keepdims=True)), 1e-12)
    k = k * (1 + (a - 1) * k_a)
    rh, kh, vh, ah = heads(r), heads(k), heads(v), heads(a)
    decay = jnp.exp(-jnp.exp(heads(w)))
    o = wkv7_scan(rh, decay, kh, vh, -kk, kk * ah)
    mu_o = jnp.mean(o, axis=-1, keepdims=True)
    var_o = jnp.mean(jnp.square(o - mu_o), axis=-1, keepdims=True)
    o = ((o - mu_o) * lax.rsqrt(var_o + RWKV_GN_EPS) * gn_w.reshape(RWKV_HEADS, RWKV_HEAD)
         + gn_b.reshape(RWKV_HEADS, RWKV_HEAD))
    bonus = jnp.sum(rh * kh * r_k.astype(f32), axis=-1, keepdims=True) * vh
    o = (o + bonus).reshape(B, T, RWKV_WIDTH) * g
    return o.astype(cols.dtype), v_first


def setup_inputs(seed: int = 0) -> dict:
    key = jax.random.key(seed)
    ks = iter(jax.random.split(key, 40))
    nrm = lambda shape, s: jax.random.normal(next(ks), shape, jnp.float32) * s
    L, D, F = DEPTH, D_MODEL, FFN_HIDDEN
    start = jax.random.randint(next(ks), (BATCH, 1), 0, 4096, dtype=jnp.int32)
    positions = start + jnp.arange(SEQ, dtype=jnp.int32)[None, :]
    return {
        "x": nrm((BATCH, SEQ, D), 1.0),
        "c": nrm((BATCH, D), 1.0),
        "positions": positions,
        "w_ada": nrm((L, D, N_MOD * D), D ** -0.5),
        "b_ada": nrm((L, N_MOD * D), 0.02),
        "w_in": nrm((L, D, N_IN), D ** -0.5),
        "attn_sinks": nrm((L, ATTN_HEADS), 0.5),
        "rwkv_mu": jax.random.uniform(next(ks), (L, RWKV_COLS), jnp.float32),
        "rwkv_w0": jax.random.uniform(next(ks), (L, RWKV_WIDTH), jnp.float32, -6.5, -1.5),
        "rwkv_w2": nrm((L, DECAY_LORA, RWKV_WIDTH), 0.5 * DECAY_LORA ** -0.5),
        "rwkv_a0": nrm((L, RWKV_WIDTH), 0.1),
        "rwkv_a2": nrm((L, AAA_LORA, RWKV_WIDTH), 0.5 * AAA_LORA ** -0.5),
        "rwkv_g2": nrm((L, GATE_LORA, RWKV_WIDTH), GATE_LORA ** -0.5),
        "rwkv_k_k": 0.85 + nrm((L, RWKV_WIDTH), 0.02),
        "rwkv_k_a": 1.0 + nrm((L, RWKV_WIDTH), 0.02),
        "rwkv_r_k": nrm((L, RWKV_HEADS, RWKV_HEAD), 0.1),
        "rwkv_gn_w": 1.0 + nrm((L, RWKV_WIDTH), 0.02),
        "rwkv_gn_b": nrm((L, RWKV_WIDTH), 0.02),
        "rwkv_v0": 1.0 + nrm((L - 1, RWKV_WIDTH), 0.1),
        "rwkv_v1": nrm((L - 1, RWKV_WIDTH, MV_LORA), RWKV_WIDTH ** -0.5),
        "rwkv_v2": nrm((L - 1, MV_LORA, RWKV_WIDTH), 0.5 * MV_LORA ** -0.5),
        "w_out": nrm((L, MIX_WIDTH, D), DEEPNORM_BETA * MIX_WIDTH ** -0.5),
        "ln1_w": 1.0 + nrm((L, D), 0.02),
        "ln1_b": nrm((L, D), 0.02),
        "w_gate_up": nrm((L, D, 2 * F), D ** -0.5),
        "w_down": nrm((L, F, D), DEEPNORM_BETA * F ** -0.5),
        "ln2_w": 1.0 + nrm((L, D), 0.02),
        "ln2_b": nrm((L, D), 0.02),
    }


def reference(x, c, positions, w_ada, b_ada, w_in, attn_sinks, rwkv_mu, rwkv_w0, rwkv_w2,
              rwkv_a0, rwkv_a2, rwkv_g2, rwkv_k_k, rwkv_k_a, rwkv_r_k, rwkv_gn_w, rwkv_gn_b,
              rwkv_v0, rwkv_v1, rwkv_v2, w_out, ln1_w, ln1_b, w_gate_up, w_down, ln2_w, ln2_b):
    B = x.shape[0]
    cos, sin = rope_tables(positions)
    cond = jax.nn.silu(c)
    v_first = None
    for l in range(DEPTH):
        mod = (jnp.einsum('bd,de->be', cond, w_ada[l]) + b_ada[l]).reshape(B, N_MOD, 1, D_MODEL)
        sh_m, sc_m, gt_m, sh_f, sc_f, gt_f = [mod[:, i] for i in range(N_MOD)]

        u = x * (1 + sc_m) + sh_m
        proj = u @ w_in[l]
        q = proj[..., :ATTN_WIDTH]
        k = proj[..., ATTN_WIDTH:ATTN_WIDTH + KV_WIDTH]
        v = proj[..., ATTN_WIDTH + KV_WIDTH:ATTN_COLS]
        y_attn = sliding_window_attention(q, k, v, attn_sinks[l], cos, sin)
        vres = None if l == 0 else (rwkv_v0[l - 1], rwkv_v1[l - 1], rwkv_v2[l - 1])
        y_rwkv, v_first = rwkv7_group(
            proj[..., ATTN_COLS:], rwkv_mu[l], rwkv_w0[l], rwkv_w2[l], rwkv_a0[l],
            rwkv_a2[l], rwkv_g2[l], rwkv_k_k[l], rwkv_k_a[l], rwkv_r_k[l],
            rwkv_gn_w[l], rwkv_gn_b[l], v_first, vres)
        mix = jnp.concatenate([y_attn, y_rwkv], axis=-1) @ w_out[l]
        x = layer_norm(DEEPNORM_ALPHA * x + gt_m * mix, ln1_w[l], ln1_b[l])

        u = x * (1 + sc_f) + sh_f
        gate, up = jnp.split(u @ w_gate_up[l], 2, axis=-1)
        ffn = (jax.nn.silu(gate) * up) @ w_down[l]
        x = layer_norm(DEEPNORM_ALPHA * x + gt_f * ffn, ln2_w[l], ln2_b[l])
    return x
```

```python
import functools

import numpy as np
import jax
import jax.numpy as jnp
from jax import lax
from jax.experimental import pallas as pl
from jax.experimental.pallas import tpu as pltpu

F32 = jnp.float32
BF16 = jnp.bfloat16

D_MODEL = 2048
DEPTH = 2
HEAD_DIM = 64
HALF = HEAD_DIM // 2
ATTN_WIDTH = 1024
ATTN_HEADS = 16
KV_HEADS = 4
KV_WIDTH = KV_HEADS * HEAD_DIM
WINDOW = 128
ROPE_THETA = 10000.0
RWKV_WIDTH = 1024
DECAY_LORA = 64
AAA_LORA = 64
GATE_LORA = 160
MV_LORA = 32
RWKV_GN_EPS = 64e-5
FFN_HIDDEN = 5632
LN_EPS = 1e-5
N_MOD = 6
DEEPNORM_ALPHA = (2.0 * DEPTH) ** 0.25

LANES = 128
N_TILES = RWKV_WIDTH // LANES
CHUNK = 64
VMEM_LIMIT = 56 * 1024 * 1024

COL_Q, COL_R, COL_K, COL_V = 0, 1024, 2048, 3072
COL_AK, COL_AV, COL_LORA = 4096, 4352, 4608
N_PROJ = 5120
LORA_PAD = 512

NN = ((1,), (0,))
NT = ((1,), (1,))
TN = ((0,), (0,))


def _dot(a, b, dims=NN):
    return lax.dot_general(a, b, (dims, ((), ())), preferred_element_type=F32)


def _split2(a):
    hi = a.astype(BF16)
    lo = (a - hi.astype(F32)).astype(BF16)
    return hi, lo


def _mm(a, b, dims=NN, passes=1):
    if passes == 1:
        return _dot(a.astype(BF16), b.astype(BF16), dims)
    ah, al = _split2(a)
    bh, bl = _split2(b)
    return _dot(ah, bh, dims) + (_dot(ah, bl, dims) + _dot(al, bh, dims))


def _mm_exact_lhs(a_bf16, b, dims=NN):
    b0 = b.astype(BF16)
    r1 = b - b0.astype(F32)
    b1 = r1.astype(BF16)
    b2 = (r1 - b1.astype(F32)).astype(BF16)
    return _dot(a_bf16, b0, dims) + (_dot(a_bf16, b1, dims) + _dot(a_bf16, b2, dims))


def _mm_exact_rhs(a, b_bf16):
    a0 = a.astype(BF16)
    r1 = a - a0.astype(F32)
    a1 = r1.astype(BF16)
    a2 = (r1 - a1.astype(F32)).astype(BF16)
    return _dot(a0, b_bf16) + (_dot(a1, b_bf16) + _dot(a2, b_bf16))


def _iota(shape, dim):
    return lax.broadcasted_iota(jnp.int32, shape, dim)


def _head_ones():
    r = _iota((LANES, LANES), 0) // HEAD_DIM
    c = _iota((LANES, LANES), 1) // HEAD_DIM
    return jnp.where(r == c, 1.0, 0.0).astype(BF16)


def _softplus(z):
    return jnp.maximum(z, 0.0) + jnp.log(1.0 + jnp.exp(-jnp.abs(z)))


def _layer_norm(z, w, b):
    mu = jnp.mean(z, axis=-1, keepdims=True)
    zc = z - mu
    var = jnp.mean(zc * zc, axis=-1, keepdims=True)
    return zc * lax.rsqrt(var + LN_EPS) * w + b


def _adaln_kernel(c_ref, w_ref, b_ref, o_ref, *, tn):
    cb = c_ref[...]
    cond = cb * jax.nn.sigmoid(cb)
    for j in range(tn // LANES):
        sl = slice(j * LANES, (j + 1) * LANES)
        o_ref[0, :, sl] = jnp.sum(w_ref[0, :, sl] * cond, axis=0, keepdims=True) + b_ref[0, :, sl]


def _adaln(c, w_ada, b_ada):
    depth, d, n = w_ada.shape
    tn = 1024
    c_b = jnp.broadcast_to(c.reshape(d, 1), (d, LANES))
    out = pl.pallas_call(
        functools.partial(_adaln_kernel, tn=tn),
        grid=(depth, n // tn),
        in_specs=[pl.BlockSpec((d, LANES), lambda l, j: (0, 0)),
                  pl.BlockSpec((1, d, tn), lambda l, j: (l, 0, j)),
                  pl.BlockSpec((1, 1, tn), lambda l, j: (l, 0, j))],
        out_specs=pl.BlockSpec((1, 1, tn), lambda l, j: (l, 0, j)),
        out_shape=jax.ShapeDtypeStruct((depth, 1, n), F32),
        compiler_params=pltpu.CompilerParams(
            dimension_semantics=("parallel", "parallel"), vmem_limit_bytes=VMEM_LIMIT),
        name="adaln",
    )(c_b, w_ada, b_ada.reshape(depth, 1, n))
    return out.reshape(depth, N_MOD, 1, d)


def _in_proj_kernel(x_ref, sc_ref, sh_ref, w_ref, o_ref, u_ref):
    @pl.when(pl.program_id(1) == 0)
    def _():
        u_ref[...] = (x_ref[...] * (1.0 + sc_ref[...]) + sh_ref[...]).astype(BF16)

    o_ref[...] = _dot(u_ref[...], w_ref[...])


def _in_proj(x, sc, sh, w_bf16):
    t, d = x.shape
    n = w_bf16.shape[1]
    tm, tn = 1024, 512
    return pl.pallas_call(
        _in_proj_kernel,
        grid=(t // tm, n // tn),
        in_specs=[pl.BlockSpec((tm, d), lambda i, j: (i, 0)),
                  pl.BlockSpec((1, d), lambda i, j: (0, 0)),
                  pl.BlockSpec((1, d), lambda i, j: (0, 0)),
                  pl.BlockSpec((d, tn), lambda i, j: (0, j))],
        out_specs=pl.BlockSpec((tm, tn), lambda i, j: (i, j)),
        out_shape=jax.ShapeDtypeStruct((t, n), F32),
        scratch_shapes=[pltpu.VMEM((tm, d), BF16)],
        compiler_params=pltpu.CompilerParams(
            dimension_semantics=("parallel", "arbitrary"), vmem_limit_bytes=VMEM_LIMIT),
        name="in_proj",
    )(x, sc, sh, w_bf16)


def _rope(t, cos, sin):
    return t * cos + pltpu.roll(t, 64, axis=1) * sin


def _attn_kernel(sink_ref, q_ref, kc_ref, kp_ref, vc_ref, vp_ref, cosc_ref, sinc_ref, cosp_ref, sinp_ref,
                 o_ref):
    blk = pl.program_id(0)
    w = WINDOW
    cos_c, sin_c = cosc_ref[...], sinc_ref[...]
    cos_p, sin_p = cosp_ref[...], sinp_ref[...]

    lane = _iota((w, LANES), 1)
    in_a_qk = (lane % HEAD_DIM) < HALF
    in_a_v = lane < HEAD_DIM

    qa = _iota((2 * w, 2 * w), 0) % w
    ks = _iota((2 * w, 2 * w), 1)
    valid = (ks > qa) & (ks <= qa + w) & ((ks >= w) | (blk > 0))
    row_is_a = _iota((2 * w, 1), 0) < w

    for m in range(KV_HEADS // 2):
        sl_kv = slice(m * LANES, (m + 1) * LANES)
        k_ctx = jnp.concatenate([_rope(kp_ref[:, sl_kv], cos_p, sin_p),
                                 _rope(kc_ref[:, sl_kv], cos_c, sin_c)], axis=0).astype(BF16)
        v_ctx = jnp.concatenate([vp_ref[:, sl_kv], vc_ref[:, sl_kv]], axis=0).astype(BF16)
        for i in range(4):
            tile = 4 * m + i
            sl = slice(tile * LANES, (tile + 1) * LANES)
            q = _rope(q_ref[:, sl], cos_c, sin_c) * (HEAD_DIM ** -0.5)
            q_st = jnp.concatenate([jnp.where(in_a_qk, q, 0.0), jnp.where(in_a_qk, 0.0, q)],
                                   axis=0).astype(BF16)
            s = _dot(q_st, k_ctx, NT)
            s = jnp.where(valid, s, -1e30)
            sink = jnp.where(row_is_a, sink_ref[2 * tile], sink_ref[2 * tile + 1])
            mx = jnp.maximum(jnp.max(s, axis=-1, keepdims=True), sink)
            p = jnp.exp(s - mx)
            den = jnp.sum(p, axis=-1, keepdims=True) + jnp.exp(sink - mx)
            o_st = _dot(p.astype(BF16), v_ctx) * (1.0 / den)
            o_ref[:, sl] = jnp.where(in_a_v, o_st[:w], o_st[w:])


def _attention(proj, sinks_perm, cos_t, sin_t):
    t = proj.shape[0]
    w = WINDOW
    prev = lambda i, s: (jnp.maximum(i - 1, 0), 0)
    cur = lambda i, s: (i, 0)
    return pl.pallas_call(
        _attn_kernel,
        grid_spec=pltpu.PrefetchScalarGridSpec(
            num_scalar_prefetch=1,
            grid=(t // w,),
            in_specs=[pl.BlockSpec((w, ATTN_WIDTH), lambda i, s: (i, COL_Q // ATTN_WIDTH)),
                      pl.BlockSpec((w, KV_WIDTH), lambda i, s: (i, COL_AK // KV_WIDTH)),
                      pl.BlockSpec((w, KV_WIDTH), lambda i, s: (jnp.maximum(i - 1, 0), COL_AK // KV_WIDTH)),
                      pl.BlockSpec((w, KV_WIDTH), lambda i, s: (i, COL_AV // KV_WIDTH)),
                      pl.BlockSpec((w, KV_WIDTH), lambda i, s: (jnp.maximum(i - 1, 0), COL_AV // KV_WIDTH)),
                      pl.BlockSpec((w, LANES), cur), pl.BlockSpec((w, LANES), cur),
                      pl.BlockSpec((w, LANES), prev), pl.BlockSpec((w, LANES), prev)],
            out_specs=pl.BlockSpec((w, ATTN_WIDTH), lambda i, s: (i, 0)),
        ),
        out_shape=jax.ShapeDtypeStruct((t, ATTN_WIDTH), F32),
        compiler_params=pltpu.CompilerParams(
            dimension_semantics=("parallel",), vmem_limit_bytes=VMEM_LIMIT),
        name="attention",
    )(sinks_perm, proj, proj, proj, proj, proj, cos_t, sin_t, cos_t, sin_t)


def _prep_kernel(*refs, tm, has_vres):
    if has_vres:
        (r_ref, k_ref, v_ref, l_ref, rp_ref, kp_ref, vp_ref, lp_ref, mu_ref, mul_ref, par_ref,
         w2_ref, a2_ref, g2_ref, vf_ref, v1_ref, v2_ref,
         ro_ref, lw_ref, ko_ref, vo_ref, kk_ref, b_ref, g_ref, bon_ref) = refs
    else:
        (r_ref, k_ref, v_ref, l_ref, rp_ref, kp_ref, vp_ref, lp_ref, mu_ref, mul_ref, par_ref,
         w2_ref, a2_ref, g2_ref,
         ro_ref, lw_ref, ko_ref, vo_ref, kk_ref, b_ref, g_ref, bon_ref) = refs
    first = pl.program_id(0) == 0

    def shifted_lerp(x_ref, p_ref, mu):
        x = x_ref[...]
        prev_row = jnp.where(first, 0.0, p_ref[7:8, :])
        row = _iota(x.shape, 0)
        xp = jnp.where(row == 0, prev_row, pltpu.roll(x, 1, axis=0))
        return x + (xp - x) * mu

    r = shifted_lerp(r_ref, rp_ref, mu_ref[0:1, :])
    k = shifted_lerp(k_ref, kp_ref, mu_ref[1:2, :])
    v = shifted_lerp(v_ref, vp_ref, mu_ref[2:3, :])
    lo = shifted_lerp(l_ref, lp_ref, mul_ref[...])

    w0, a0 = par_ref[0:1, :], par_ref[1:2, :]
    k_k, k_a, r_k = par_ref[2:3, :], par_ref[3:4, :], par_ref[4:5, :]

    wa = lo[:, 0:LANES]
    gd = lo[:, LANES:3 * LANES]
    w_pre = w0 + _mm(jnp.tanh(wa), w2_ref[...])
    w = -_softplus(-w_pre) - 0.5
    a = jax.nn.sigmoid(a0 + _mm(wa, a2_ref[...]))
    g = _mm(jax.nn.sigmoid(gd), g2_ref[...])
    if has_vres:
        v0 = par_ref[5:6, :]
        mix = jax.nn.sigmoid(v0 + _mm(_mm(v, v1_ref[...]), v2_ref[...]))
        v = v + (vf_ref[...] - v) * mix

    kk = k * k_k
    k_mod = k * (1.0 + (a - 1.0) * k_a)
    rk = r * k_mod * r_k
    ones = _head_ones()
    for j in range(N_TILES):
        sl = slice(j * LANES, (j + 1) * LANES)
        kk_j = kk[:, sl]
        ss = _mm_exact_rhs(kk_j * kk_j, ones)
        kk_n = kk_j / jnp.maximum(jnp.sqrt(ss), 1e-12)
        kk_ref[:, sl] = kk_n
        b_ref[:, sl] = kk_n * a[:, sl]
        bon_ref[:, sl] = _mm_exact_rhs(rk[:, sl], ones) * v[:, sl]
    ro_ref[...] = r
    lw_ref[...] = -jnp.exp(w)
    ko_ref[...] = k_mod
    vo_ref[...] = v
    g_ref[...] = g


def _rwkv_prep(proj, mu3, mu_lora, params, w2p, a2p, g2p, vres):
    t = proj.shape[0]
    tm = 256
    wdt = RWKV_WIDTH
    has_vres = vres is not None

    def col(cb, width):
        return pl.BlockSpec((tm, width), lambda i: (i, cb))

    def prev(cb, width):
        return pl.BlockSpec((8, width), lambda i: (jnp.maximum(i * (tm // 8) - 1, 0), cb))

    def full(a):
        return pl.BlockSpec(a.shape, lambda i: (0, 0))

    in_specs = [col(COL_R // wdt, wdt), col(COL_K // wdt, wdt), col(COL_V // wdt, wdt),
                col(COL_LORA // LORA_PAD, LORA_PAD),
                prev(COL_R // wdt, wdt), prev(COL_K // wdt, wdt), prev(COL_V // wdt, wdt),
                prev(COL_LORA // LORA_PAD, LORA_PAD),
                full(mu3), full(mu_lora), full(params), full(w2p), full(a2p), full(g2p)]
    args = [proj] * 8 + [mu3, mu_lora, params, w2p, a2p, g2p]
    if has_vres:
        v_first, v1p, v2p = vres
        in_specs += [pl.BlockSpec((tm, wdt), lambda i: (i, 0)), full(v1p), full(v2p)]
        args += [v_first, v1p, v2p]
    out_spec = pl.BlockSpec((tm, wdt), lambda i: (i, 0))
    out_sds = jax.ShapeDtypeStruct((t, wdt), F32)
    return pl.pallas_call(
        functools.partial(_prep_kernel, tm=tm, has_vres=has_vres),
        grid=(t // tm,),
        in_specs=in_specs,
        out_specs=[out_spec] * 8,
        out_shape=[out_sds] * 8,
        compiler_params=pltpu.CompilerParams(
            dimension_semantics=("parallel",), vmem_limit_bytes=VMEM_LIMIT),
        name="rwkv_prep",
    )(*args)


WKV_PASSES = 3


def _wkv_kernel(r_ref, lw_ref, k_ref, v_ref, kk_ref, b_ref, o_ref, st_ref):
    c = CHUNK

    @pl.when(pl.program_id(0) == 0)
    def _():
        st_ref[...] = jnp.zeros_like(st_ref)

    row_t = _iota((c, c), 0)
    col_t = _iota((c, c), 1)
    l_incl = jnp.where(row_t >= col_t, 1.0, 0.0).astype(BF16)
    cum_all = _mm_exact_lhs(l_incl, lw_ref[...])

    lane = _iota((c, LANES), 1)
    m_a = lane < HEAD_DIM
    r2 = _iota((2 * c, 2 * c), 0)
    c2 = _iota((2 * c, 2 * c), 1)
    strict = r2 > c2
    incl = r2 >= c2
    eye = r2 == c2
    anti = (r2 < c) != (c2 < c)

    def stack(x):
        return jnp.concatenate([jnp.where(m_a, x, 0.0), jnp.where(m_a, 0.0, x)], axis=0)

    def stack_sw(x):
        return jnp.concatenate([jnp.where(m_a, 0.0, x), jnp.where(m_a, x, 0.0)], axis=0)

    for j in range(N_TILES):
        sl = slice(j * LANES, (j + 1) * LANES)
        lw = lw_ref[:, sl]
        cum = cum_all[:, sl]
        cum_last = cum[c - 1:c, :]
        e_in = jnp.exp(cum)
        e_ex = jnp.exp(cum - lw)
        e_neg = jnp.exp(-cum)
        e_tail = jnp.exp(cum_last - cum)
        w_c = jnp.exp(cum_last)
        r = r_ref[:, sl]
        k = k_ref[:, sl]
        kk = kk_ref[:, sl]
        b = b_ref[:, sl]
        v_sw = pltpu.roll(v_ref[:, sl], 64, axis=1)

        a_st = stack(-kk * e_ex)
        r_st = stack(r * e_in)
        lhs = jnp.concatenate([a_st, r_st], axis=0)
        rhs = jnp.concatenate([stack(b * e_neg), stack(k * e_neg)], axis=0)
        gm = _mm(lhs, rhs, NT, WKV_PASSES)
        d_ab = jnp.where(strict, gm[:2 * c, :2 * c], 0.0)
        d_ak = jnp.where(strict, gm[:2 * c, 2 * c:], 0.0)
        d_rb = jnp.where(incl, gm[2 * c:, :2 * c], 0.0)
        d_rk = jnp.where(incl, gm[2 * c:, 2 * c:], 0.0)

        v_st = stack_sw(v_sw)
        y = a_st + _mm(d_ak, v_st, NN, WKV_PASSES)
        d = d_ab
        n_lvl = 6
        for lvl in range(n_lvl):
            if lvl < n_lvl - 1:
                res = _mm(d, jnp.concatenate([y, d], axis=1), NN, WKV_PASSES)
                y = y + res[:, :2 * c]
                d = res[:, 2 * c:]
            else:
                y = y + _mm(d, y, NN, WKV_PASSES)

        ah_st = jnp.concatenate([jnp.where(m_a, y[:c], 0.0), jnp.where(m_a, 0.0, y[c:])], axis=0)
        u_loc = jnp.concatenate([jnp.where(m_a, 0.0, y[:c]), jnp.where(m_a, y[c:], 0.0)], axis=0)
        st = st_ref[j]
        rs_as = _mm(jnp.concatenate([r_st, ah_st], axis=0), st)
        u_st = rs_as[2 * c:] + u_loc
        o_st = rs_as[:2 * c] + _mm(jnp.concatenate([d_rb, d_rk], axis=1),
                                   jnp.concatenate([u_st, v_st], axis=0))
        o_ref[:, sl] = pltpu.roll(o_st[:c] + o_st[c:], 64, axis=1)

        u_sw = u_st[:c] + u_st[c:]
        delta = _mm(jnp.concatenate([b * e_tail, k * e_tail], axis=0),
                    jnp.concatenate([u_sw, v_sw], axis=0), TN)
        w_col = jnp.sum(jnp.where(eye, jnp.broadcast_to(w_c, (2 * c, 2 * c)), 0.0), axis=1, keepdims=True)
        st_ref[j] = w_col * st + jnp.where(anti, delta, 0.0)


def _wkv(r, lw, k, v, kk, b):
    t, wdt = r.shape
    spec = pl.BlockSpec((CHUNK, wdt), lambda i: (i, 0))
    return pl.pallas_call(
        _wkv_kernel,
        grid=(t // CHUNK,),
        in_specs=[spec] * 6,
        out_specs=spec,
        out_shape=jax.ShapeDtypeStruct((t, wdt), F32),
        scratch_shapes=[pltpu.VMEM((N_TILES, LANES, LANES), F32)],
        compiler_params=pltpu.CompilerParams(
            dimension_semantics=("arbitrary",), vmem_limit_bytes=VMEM_LIMIT),
        name="wkv",
    )(r, lw, k, v, kk, b)


def _out_proj_kernel(ya_ref, o_ref, bon_ref, g_ref, gn_ref, wa_ref, wr_ref, x_ref, mod_ref, out_ref):
    ones = _head_ones()
    gn_w, gn_b = gn_ref[0:1, :], gn_ref[1:2, :]
    acc = _dot(ya_ref[...].astype(BF16), wa_ref[...])
    for j in range(N_TILES):
        sl = slice(j * LANES, (j + 1) * LANES)
        o = o_ref[:, sl]
        mu = _mm_exact_rhs(o, ones) * (1.0 / HEAD_DIM)
        oc = o - mu
        var = _mm_exact_rhs(oc * oc, ones) * (1.0 / HEAD_DIM)
        y = (oc * lax.rsqrt(var + RWKV_GN_EPS) * gn_w[:, sl] + gn_b[:, sl] + bon_ref[:, sl]) * g_ref[:, sl]
        acc = acc + _dot(y.astype(BF16), wr_ref[sl, :])
    z = DEEPNORM_ALPHA * x_ref[...] + mod_ref[0:1, :] * acc
    out_ref[...] = _layer_norm(z, mod_ref[1:2, :], mod_ref[2:3, :])


def _out_proj(y_attn, o, bonus, g, gn, wa, wr, x, mod3):
    t, d = x.shape
    tm = 256
    wdt = RWKV_WIDTH
    row = lambda width: pl.BlockSpec((tm, width), lambda i: (i, 0))
    full = lambda a: pl.BlockSpec(a.shape, lambda i: (0, 0))
    return pl.pallas_call(
        _out_proj_kernel,
        grid=(t // tm,),
        in_specs=[row(ATTN_WIDTH), row(wdt), row(wdt), row(wdt), full(gn), full(wa), full(wr), row(d),
                  full(mod3)],
        out_specs=row(d),
        out_shape=jax.ShapeDtypeStruct((t, d), F32),
        compiler_params=pltpu.CompilerParams(
            dimension_semantics=("parallel",), vmem_limit_bytes=VMEM_LIMIT),
        name="out_proj",
    )(y_attn, o, bonus, g, gn, wa, wr, x, mod3)


def _ffn_kernel(x_ref, mod_ref, wg_ref, wu_ref, wd_ref, out_ref, u_ref, acc_ref):
    j = pl.program_id(1)

    @pl.when(j == 0)
    def _():
        u_ref[...] = (x_ref[...] * (1.0 + mod_ref[1:2, :]) + mod_ref[0:1, :]).astype(BF16)
        acc_ref[...] = jnp.zeros_like(acc_ref)

    u = u_ref[...]
    gate = _dot(u, wg_ref[...])
    up = _dot(u, wu_ref[...])
    h = (gate * jax.nn.sigmoid(gate) * up).astype(BF16)
    acc_ref[...] += _dot(h, wd_ref[...])

    @pl.when(j == pl.num_programs(1) - 1)
    def _():
        z = DEEPNORM_ALPHA * x_ref[...] + mod_ref[2:3, :] * acc_ref[...]
        out_ref[...] = _layer_norm(z, mod_ref[3:4, :], mod_ref[4:5, :])


def _ffn(x, mod5, wg, wu, wd):
    t, d = x.shape
    f = wg.shape[1]
    tm, tf = 512, 512
    return pl.pallas_call(
        _ffn_kernel,
        grid=(t // tm, f // tf),
        in_specs=[pl.BlockSpec((tm, d), lambda i, j: (i, 0)),
                  pl.BlockSpec(mod5.shape, lambda i, j: (0, 0)),
                  pl.BlockSpec((d, tf), lambda i, j: (0, j)),
                  pl.BlockSpec((d, tf), lambda i, j: (0, j)),
                  pl.BlockSpec((tf, d), lambda i, j: (j, 0))],
        out_specs=pl.BlockSpec((tm, d), lambda i, j: (i, 0)),
        out_shape=jax.ShapeDtypeStruct((t, d), F32),
        scratch_shapes=[pltpu.VMEM((tm, d), BF16), pltpu.VMEM((tm, d), F32)],
        compiler_params=pltpu.CompilerParams(
            dimension_semantics=("parallel", "arbitrary"), vmem_limit_bytes=VMEM_LIMIT),
        name="ffn",
    )(x, mod5, wg, wu, wd)


def _attn_permutations():
    q_cols, heads = [], []
    for tile in range(ATTN_HEADS // 2):
        m, i = divmod(tile, 4)
        h_a, h_b = 8 * m + i, 8 * m + 4 + i
        heads += [h_a, h_b]
        for half in range(2):
            for h in (h_a, h_b):
                q_cols += list(range(h * HEAD_DIM + half * HALF, h * HEAD_DIM + (half + 1) * HALF))
    k_cols = []
    for m in range(KV_HEADS // 2):
        for half in range(2):
            for h in (2 * m, 2 * m + 1):
                k_cols += list(range(h * HEAD_DIM + half * HALF, h * HEAD_DIM + (half + 1) * HALF))
    out_rows = []
    for h in heads:
        out_rows += list(range(h * HEAD_DIM, (h + 1) * HEAD_DIM))
    return (np.asarray(q_cols, np.int32), np.asarray(k_cols, np.int32),
            np.asarray(heads, np.int32), np.asarray(out_rows, np.int32))


def _pad_rows(w, rows, offset=0):
    out = jnp.zeros((rows, w.shape[1]), w.dtype)
    return lax.dynamic_update_slice(out, w, (offset, 0))


def kernel(x, c, positions, w_ada, b_ada, w_in, attn_sinks, rwkv_mu, rwkv_w0, rwkv_w2, rwkv_a0, rwkv_a2, rwkv_g2, rwkv_k_k, rwkv_k_a, rwkv_r_k, rwkv_gn_w, rwkv_gn_b, rwkv_v0, rwkv_v1, rwkv_v2, w_out, ln1_w, ln1_b, w_gate_up, w_down, ln2_w, ln2_b):
    bsz, t, d = x.shape
    assert bsz == 1 and d == D_MODEL
    depth = w_in.shape[0]
    q_cols, k_cols, heads, out_rows = _attn_permutations()

    inv = ROPE_THETA ** (-jnp.arange(0, HEAD_DIM, 2, dtype=F32) / HEAD_DIM)
    ang = positions[0].astype(F32)[:, None] * inv
    cos, sin = jnp.cos(ang), jnp.sin(ang)
    cos_t = jnp.concatenate([cos, cos, cos, cos], axis=-1)
    sin_t = jnp.concatenate([-sin, -sin, sin, sin], axis=-1)

    mod = _adaln(c, w_ada, b_ada)
    xs = x[0]
    v_first = None
    rw = RWKV_WIDTH
    r0 = ATTN_WIDTH + 2 * KV_WIDTH
    for l in range(depth):
        sh_m, sc_m, gt_m, sh_f, sc_f, gt_f = [mod[l, i] for i in range(N_MOD)]
        wl = w_in[l]
        w_cat = jnp.concatenate([
            wl[:, q_cols],
            wl[:, r0:r0 + 3 * rw],
            wl[:, ATTN_WIDTH + k_cols],
            wl[:, ATTN_WIDTH + KV_WIDTH:r0],
            wl[:, r0 + 3 * rw:],
            jnp.zeros((d, N_PROJ - COL_LORA - (DECAY_LORA + AAA_LORA + GATE_LORA)), F32)],
            axis=1).astype(BF16)
        proj = _in_proj(xs, sc_m, sh_m, w_cat)

        y_attn = _attention(proj, attn_sinks[l][heads], cos_t, sin_t)

        mu = rwkv_mu[l]
        mu3 = mu[:3 * rw].reshape(3, rw)
        mu_lora = jnp.pad(mu[3 * rw:], (0, LORA_PAD - (mu.shape[0] - 3 * rw))).reshape(1, LORA_PAD)
        zero = jnp.zeros((rw,), F32)
        params = jnp.stack([rwkv_w0[l], rwkv_a0[l], rwkv_k_k[l], rwkv_k_a[l], rwkv_r_k[l].reshape(rw),
                            rwkv_v0[l - 1] if l > 0 else zero, zero, zero])
        w2p = _pad_rows(rwkv_w2[l], LANES, 0)
        a2p = _pad_rows(rwkv_a2[l], LANES, DECAY_LORA)
        g2p = _pad_rows(rwkv_g2[l], 2 * LANES, 0)
        if l == 0:
            vres = None
        else:
            v1p = jnp.pad(rwkv_v1[l - 1], ((0, 0), (0, LANES - MV_LORA)))
            v2p = _pad_rows(rwkv_v2[l - 1], LANES, 0)
            vres = (v_first, v1p, v2p)
        r, lw, k_mod, v, kk, b, g, bonus = _rwkv_prep(proj, mu3, mu_lora, params, w2p, a2p, g2p, vres)
        if l == 0:
            v_first = v
        o = _wkv(r, lw, k_mod, v, kk, b)

        gn = jnp.stack([rwkv_gn_w[l], rwkv_gn_b[l]])
        wa = w_out[l][:ATTN_WIDTH][out_rows].astype(BF16)
        wr = w_out[l][ATTN_WIDTH:].astype(BF16)
        mod3 = jnp.concatenate([gt_m, ln1_w[l][None], ln1_b[l][None]], axis=0)
        xs = _out_proj(y_attn, o, bonus, g, gn, wa, wr, xs, mod3)

        mod5 = jnp.concatenate([sh_f, sc_f, gt_f, ln2_w[l][None], ln2_b[l][None]], axis=0)
        wg = w_gate_up[l][:, :FFN_HIDDEN].astype(BF16)
        wu = w_gate_up[l][:, FFN_HIDDEN:].astype(BF16)
        xs = _ffn(xs, mod5, wg, wu, w_down[l].astype(BF16))
    return xs[None]
```

```python
import functools

import numpy as np
import jax
import jax.numpy as jnp
from jax import lax
from jax.experimental import pallas as pl
from jax.experimental.pallas import tpu as pltpu

F32 = jnp.float32
BF16 = jnp.bfloat16

D_MODEL = 2048
DEPTH = 2
HEAD_DIM = 64
HALF = HEAD_DIM // 2
ATTN_WIDTH = 1024
ATTN_HEADS = 16
KV_HEADS = 4
KV_WIDTH = KV_HEADS * HEAD_DIM
WINDOW = 128
ROPE_THETA = 10000.0
RWKV_WIDTH = 1024
DECAY_LORA = 64
AAA_LORA = 64
GATE_LORA = 160
MV_LORA = 32
RWKV_GN_EPS = 64e-5
FFN_HIDDEN = 5632
LN_EPS = 1e-5
N_MOD = 6
DEEPNORM_ALPHA = (2.0 * DEPTH) ** 0.25

LANES = 128
N_TILES = RWKV_WIDTH // LANES
CHUNK = 64
VMEM_LIMIT = 56 * 1024 * 1024

COL_Q, COL_R, COL_K, COL_V = 0, 1024, 2048, 3072
COL_AK, COL_AV, COL_LORA = 4096, 4352, 4608
N_PROJ = 5120
LORA_PAD = 512

NN = ((1,), (0,))
NT = ((1,), (1,))
TN = ((0,), (0,))


def _dot(a, b, dims=NN):
    return lax.dot_general(a, b, (dims, ((), ())), preferred_element_type=F32)


def _split2(a):
    hi = a.astype(BF16)
    lo = (a - hi.astype(F32)).astype(BF16)
    return hi, lo


def _mm(a, b, dims=NN, passes=1):
    if passes == 1:
        return _dot(a.astype(BF16), b.astype(BF16), dims)
    ah, al = _split2(a)
    bh, bl = _split2(b)
    return _dot(ah, bh, dims) + (_dot(ah, bl, dims) + _dot(al, bh, dims))


def _mm_exact_lhs(a_bf16, b, dims=NN):
    b0 = b.astype(BF16)
    r1 = b - b0.astype(F32)
    b1 = r1.astype(BF16)
    b2 = (r1 - b1.astype(F32)).astype(BF16)
    return _dot(a_bf16, b0, dims) + (_dot(a_bf16, b1, dims) + _dot(a_bf16, b2, dims))


def _mm_exact_rhs(a, b_bf16):
    a0 = a.astype(BF16)
    r1 = a - a0.astype(F32)
    a1 = r1.astype(BF16)
    a2 = (r1 - a1.astype(F32)).astype(BF16)
    return _dot(a0, b_bf16) + (_dot(a1, b_bf16) + _dot(a2, b_bf16))


def _iota(shape, dim):
    return lax.broadcasted_iota(jnp.int32, shape, dim)


def _head_ones():
    r = _iota((LANES, LANES), 0) // HEAD_DIM
    c = _iota((LANES, LANES), 1) // HEAD_DIM
    return jnp.where(r == c, 1.0, 0.0).astype(BF16)


def _softplus(z):
    return jnp.maximum(z, 0.0) + jnp.log(1.0 + jnp.exp(-jnp.abs(z)))


def _layer_norm(z, w, b):
    mu = jnp.mean(z, axis=-1, keepdims=True)
    zc = z - mu
    var = jnp.mean(zc * zc, axis=-1, keepdims=True)
    return zc * lax.rsqrt(var + LN_EPS) * w + b


def _adaln_kernel(c_ref, w_ref, b_ref, o_ref, *, tn):
    cb = c_ref[...]
    cond = cb * jax.nn.sigmoid(cb)
    for j in range(tn // LANES):
        sl = slice(j * LANES, (j + 1) * LANES)
        o_ref[0, :, sl] = jnp.sum(w_ref[0, :, sl] * cond, axis=0, keepdims=True) + b_ref[0, :, sl]


def _adaln(c, w_ada, b_ada):
    depth, d, n = w_ada.shape
    tn = 1024
    c_b = jnp.broadcast_to(c.reshape(d, 1), (d, LANES))
    out = pl.pallas_call(
        functools.partial(_adaln_kernel, tn=tn),
        grid=(depth, n // tn),
        in_specs=[pl.BlockSpec((d, LANES), lambda l, j: (0, 0)),
                  pl.BlockSpec((1, d, tn), lambda l, j: (l, 0, j)),
                  pl.BlockSpec((1, 1, tn), lambda l, j: (l, 0, j))],
        out_specs=pl.BlockSpec((1, 1, tn), lambda l, j: (l, 0, j)),
        out_shape=jax.ShapeDtypeStruct((depth, 1, n), F32),
        compiler_params=pltpu.CompilerParams(
            dimension_semantics=("parallel", "parallel"), vmem_limit_bytes=VMEM_LIMIT),
        name="adaln",
    )(c_b, w_ada, b_ada.reshape(depth, 1, n))
    return out.reshape(depth, N_MOD, 1, d)


def _in_proj_kernel(x_ref, sc_ref, sh_ref, w_ref, o_ref, u_ref):
    @pl.when(pl.program_id(1) == 0)
    def _():
        u_ref[...] = (x_ref[...] * (1.0 + sc_ref[...]) + sh_ref[...]).astype(BF16)

    o_ref[...] = _dot(u_ref[...], w_ref[...])


def _in_proj(x, sc, sh, w_bf16):
    t, d = x.shape
    n = w_bf16.shape[1]
    tm, tn = 1024, 512
    return pl.pallas_call(
        _in_proj_kernel,
        grid=(t // tm, n // tn),
        in_specs=[pl.BlockSpec((tm, d), lambda i, j: (i, 0)),
                  pl.BlockSpec((1, d), lambda i, j: (0, 0)),
                  pl.BlockSpec((1, d), lambda i, j: (0, 0)),
                  pl.BlockSpec((d, tn), lambda i, j: (0, j))],
        out_specs=pl.BlockSpec((tm, tn), lambda i, j: (i, j)),
        out_shape=jax.ShapeDtypeStruct((t, n), F32),
        scratch_shapes=[pltpu.VMEM((tm, d), BF16)],
        compiler_params=pltpu.CompilerParams(
            dimension_semantics=("parallel", "arbitrary"), vmem_limit_bytes=VMEM_LIMIT),
        name="in_proj",
    )(x, sc, sh, w_bf16)


def _rope(t, cos, sin):
    return t * cos + pltpu.roll(t, 64, axis=1) * sin


def _attn_kernel(sink_ref, q_ref, kc_ref, kp_ref, vc_ref, vp_ref, cosc_ref, sinc_ref, cosp_ref, sinp_ref,
                 o_ref):
    blk = pl.program_id(0)
    w = WINDOW
    cos_c, sin_c = cosc_ref[...], sinc_ref[...]
    cos_p, sin_p = cosp_ref[...], sinp_ref[...]

    lane = _iota((w, LANES), 1)
    in_a_qk = (lane % HEAD_DIM) < HALF
    in_a_v = lane < HEAD_DIM

    qa = _iota((2 * w, 2 * w), 0) % w
    ks = _iota((2 * w, 2 * w), 1)
    valid = (ks > qa) & (ks <= qa + w) & ((ks >= w) | (blk > 0))
    row_is_a = _iota((2 * w, 1), 0) < w

    for m in range(KV_HEADS // 2):
        sl_kv = slice(m * LANES, (m + 1) * LANES)
        k_ctx = jnp.concatenate([_rope(kp_ref[:, sl_kv], cos_p, sin_p),
                                 _rope(kc_ref[:, sl_kv], cos_c, sin_c)], axis=0).astype(BF16)
        v_ctx = jnp.concatenate([vp_ref[:, sl_kv], vc_ref[:, sl_kv]], axis=0).astype(BF16)
        for i in range(4):
            tile = 4 * m + i
            sl = slice(tile * LANES, (tile + 1) * LANES)
            q = _rope(q_ref[:, sl], cos_c, sin_c) * (HEAD_DIM ** -0.5)
            q_st = jnp.concatenate([jnp.where(in_a_qk, q, 0.0), jnp.where(in_a_qk, 0.0, q)],
                                   axis=0).astype(BF16)
            s = _dot(q_st, k_ctx, NT)
            s = jnp.where(valid, s, -1e30)
            sink = jnp.where(row_is_a, sink_ref[2 * tile], sink_ref[2 * tile + 1])
            mx = jnp.maximum(jnp.max(s, axis=-1, keepdims=True), sink)
            p = jnp.exp(s - mx)
            den = jnp.sum(p, axis=-1, keepdims=True) + jnp.exp(sink - mx)
            o_st = _dot(p.astype(BF16), v_ctx) * (1.0 / den)
            o_ref[:, sl] = jnp.where(in_a_v, o_st[:w], o_st[w:])


def _attention(proj, sinks_perm, cos_t, sin_t):
    t = proj.shape[0]
    w = WINDOW
    prev = lambda i, s: (jnp.maximum(i - 1, 0), 0)
    cur = lambda i, s: (i, 0)
    return pl.pallas_call(
        _attn_kernel,
        grid_spec=pltpu.PrefetchScalarGridSpec(
            num_scalar_prefetch=1,
            grid=(t // w,),
            in_specs=[pl.BlockSpec((w, ATTN_WIDTH), lambda i, s: (i, COL_Q // ATTN_WIDTH)),
                      pl.BlockSpec((w, KV_WIDTH), lambda i, s: (i, COL_AK // KV_WIDTH)),
                      pl.BlockSpec((w, KV_WIDTH), lambda i, s: (jnp.maximum(i - 1, 0), COL_AK // KV_WIDTH)),
                      pl.BlockSpec((w, KV_WIDTH), lambda i, s: (i, COL_AV // KV_WIDTH)),
                      pl.BlockSpec((w, KV_WIDTH), lambda i, s: (jnp.maximum(i - 1, 0), COL_AV // KV_WIDTH)),
                      pl.BlockSpec((w, LANES), cur), pl.BlockSpec((w, LANES), cur),
                      pl.BlockSpec((w, LANES), prev), pl.BlockSpec((w, LANES), prev)],
            out_specs=pl.BlockSpec((w, ATTN_WIDTH), lambda i, s: (i, 0)),
        ),
        out_shape=jax.ShapeDtypeStruct((t, ATTN_WIDTH), F32),
        compiler_params=pltpu.CompilerParams(
            dimension_semantics=("parallel",), vmem_limit_bytes=VMEM_LIMIT),
        name="attention",
    )(sinks_perm, proj, proj, proj, proj, proj, cos_t, sin_t, cos_t, sin_t)


def _prep_kernel(*refs, tm, has_vres):
    if has_vres:
        (r_ref, k_ref, v_ref, l_ref, rp_ref, kp_ref, vp_ref, lp_ref, mu_ref, mul_ref, par_ref,
         w2_ref, a2_ref, g2_ref, vf_ref, v1_ref, v2_ref,
         ro_ref, lw_ref, ko_ref, vo_ref, kk_ref, b_ref, g_ref, bon_ref) = refs
    else:
        (r_ref, k_ref, v_ref, l_ref, rp_ref, kp_ref, vp_ref, lp_ref, mu_ref, mul_ref, par_ref,
         w2_ref, a2_ref, g2_ref,
         ro_ref, lw_ref, ko_ref, vo_ref, kk_ref, b_ref, g_ref, bon_ref) = refs
    first = pl.program_id(0) == 0

    def shifted_lerp(x_ref, p_ref, mu):
        x = x_ref[...]
        prev_row = jnp.where(first, 0.0, p_ref[7:8, :])
        row = _iota(x.shape, 0)
        xp = jnp.where(row == 0, prev_row, pltpu.roll(x, 1, axis=0))
        return x + (xp - x) * mu

    r = shifted_lerp(r_ref, rp_ref, mu_ref[0:1, :])
    k = shifted_lerp(k_ref, kp_ref, mu_ref[1:2, :])
    v = shifted_lerp(v_ref, vp_ref, mu_ref[2:3, :])
    lo = shifted_lerp(l_ref, lp_ref, mul_ref[...])

    w0, a0 = par_ref[0:1, :], par_ref[1:2, :]
    k_k, k_a, r_k = par_ref[2:3, :], par_ref[3:4, :], par_ref[4:5, :]

    wa = lo[:, 0:LANES]
    gd = lo[:, LANES:3 * LANES]
    w_pre = w0 + _mm(jnp.tanh(wa), w2_ref[...])
    w = -_softplus(-w_pre) - 0.5
    a = jax.nn.sigmoid(a0 + _mm(wa, a2_ref[...]))
    g = _mm(jax.nn.sigmoid(gd), g2_ref[...])
    if has_vres:
        v0 = par_ref[5:6, :]
        mix = jax.nn.sigmoid(v0 + _mm(_mm(v, v1_ref[...]), v2_ref[...]))
        v = v + (vf_ref[...] - v) * mix

    kk = k * k_k
    k_mod = k * (1.0 + (a - 1.0) * k_a)
    rk = r * k_mod * r_k
    ones = _head_ones()
    for j in range(N_TILES):
        sl = slice(j * LANES, (j + 1) * LANES)
        kk_j = kk[:, sl]
        ss = _mm_exact_rhs(kk_j * kk_j, ones)
        kk_n = kk_j / jnp.maximum(jnp.sqrt(ss), 1e-12)
        kk_ref[:, sl] = kk_n
        b_ref[:, sl] = kk_n * a[:, sl]
        bon_ref[:, sl] = _mm_exact_rhs(rk[:, sl], ones) * v[:, sl]
    ro_ref[...] = r
    lw_ref[...] = -jnp.exp(w)
    ko_ref[...] = k_mod
    vo_ref[...] = v
    g_ref[...] = g


def _rwkv_prep(proj, mu3, mu_lora, params, w2p, a2p, g2p, vres):
    t = proj.shape[0]
    tm = 256
    wdt = RWKV_WIDTH
    has_vres = vres is not None

    def col(cb, width):
        return pl.BlockSpec((tm, width), lambda i: (i, cb))

    def prev(cb, width):
        return pl.BlockSpec((8, width), lambda i: (jnp.maximum(i * (tm // 8) - 1, 0), cb))

    def full(a):
        return pl.BlockSpec(a.shape, lambda i: (0, 0))

    in_specs = [col(COL_R // wdt, wdt), col(COL_K // wdt, wdt), col(COL_V // wdt, wdt),
                col(COL_LORA // LORA_PAD, LORA_PAD),
                prev(COL_R // wdt, wdt), prev(COL_K // wdt, wdt), prev(COL_V // wdt, wdt),
                prev(COL_LORA // LORA_PAD, LORA_PAD),
                full(mu3), full(mu_lora), full(params), full(w2p), full(a2p), full(g2p)]
    args = [proj] * 8 + [mu3, mu_lora, params, w2p, a2p, g2p]
    if has_vres:
        v_first, v1p, v2p = vres
        in_specs += [pl.BlockSpec((tm, wdt), lambda i: (i, 0)), full(v1p), full(v2p)]
        args += [v_first, v1p, v2p]
    out_spec = pl.BlockSpec((tm, wdt), lambda i: (i, 0))
    out_sds = jax.ShapeDtypeStruct((t, wdt), F32)
    return pl.pallas_call(
        functools.partial(_prep_kernel, tm=tm, has_vres=has_vres),
        grid=(t // tm,),
        in_specs=in_specs,
        out_specs=[out_spec] * 8,
        out_shape=[out_sds] * 8,
        compiler_params=pltpu.CompilerParams(
            dimension_semantics=("parallel",), vmem_limit_bytes=VMEM_LIMIT),
        name="rwkv_prep",
    )(*args)


G_PASSES = 1
SOLVE_PASSES = 1
SOLVE_REFINE = False


def _wkv_kernel(r_ref, lw_ref, k_ref, v_ref, kk_ref, b_ref, o_ref, st_ref):
    c = CHUNK

    @pl.when(pl.program_id(0) == 0)
    def _():
        st_ref[...] = jnp.zeros_like(st_ref)

    row_t = _iota((c, c), 0)
    col_t = _iota((c, c), 1)
    l_incl = jnp.where(row_t >= col_t, 1.0, 0.0).astype(BF16)
    cum_all = _mm_exact_lhs(l_incl, lw_ref[...])

    lane = _iota((c, LANES), 1)
    m_a = lane < HEAD_DIM
    r2 = _iota((2 * c, 2 * c), 0)
    c2 = _iota((2 * c, 2 * c), 1)
    strict = r2 > c2
    incl = r2 >= c2
    eye = r2 == c2
    anti = (r2 < c) != (c2 < c)

    def stack(x):
        return jnp.concatenate([jnp.where(m_a, x, 0.0), jnp.where(m_a, 0.0, x)], axis=0)

    def stack_sw(x):
        return jnp.concatenate([jnp.where(m_a, 0.0, x), jnp.where(m_a, x, 0.0)], axis=0)

    tiles = range(N_TILES)
    sls = [slice(j * LANES, (j + 1) * LANES) for j in tiles]
    bf = lambda z: z.astype(BF16)

    a_st, r_st, rhs, v_sw, v_st, bk_tail, w_c = [], [], [], [], [], [], []
    for j in tiles:
        sl = sls[j]
        lw = lw_ref[:, sl]
        cum = cum_all[:, sl]
        cum_last = cum[c - 1:c, :]
        e_neg = jnp.exp(-cum)
        e_tail = jnp.exp(cum_last - cum)
        k = k_ref[:, sl]
        b = b_ref[:, sl]
        a_st.append(stack(-kk_ref[:, sl] * jnp.exp(cum - lw)))
        r_st.append(bf(stack(r_ref[:, sl] * jnp.exp(cum))))
        rhs.append(jnp.concatenate([stack(b * e_neg), stack(k * e_neg)], axis=0))
        vs = pltpu.roll(v_ref[:, sl], 64, axis=1)
        v_sw.append(bf(vs))
        v_st.append(bf(stack_sw(vs)))
        bk_tail.append(bf(jnp.concatenate([b * e_tail, k * e_tail], axis=0)))
        w_c.append(jnp.exp(cum_last))

    incl2 = jnp.concatenate([incl, incl], axis=1)
    d_ab, d_ak, d_r = [], [], []
    for j in tiles:
        ga = _mm(a_st[j], rhs[j], NT, G_PASSES)
        gr = _dot(r_st[j], bf(rhs[j]), NT)
        d_ab.append(ga[:, :2 * c])
        d_ak.append(jnp.where(strict, ga[:, 2 * c:], 0.0))
        d_r.append(bf(jnp.where(incl2, gr, 0.0)))

    p = SOLVE_PASSES
    y0 = [a_st[j] + _mm(d_ak[j], v_st[j].astype(F32), NN, G_PASSES) for j in tiles]
    blk = lambda s: (r2 // s) == (c2 // s)
    diag8 = strict & blk(8)
    d1 = [jnp.where(diag8, d_ab[j], 0.0) for j in tiles]
    p1 = [jnp.where(eye, 1.0, d1[j]) for j in tiles]
    d2 = [_mm(d1[j], d1[j], NN, p) for j in tiles]
    res = [_mm(d2[j], jnp.concatenate([p1[j], d2[j]], axis=1), NN, p) for j in tiles]
    p2 = [p1[j] + res[j][:, :2 * c] for j in tiles]
    tm = [p2[j] + _mm(res[j][:, 2 * c:], p2[j], NN, p) for j in tiles]
    for s in (8, 16, 32):
        off = strict & blk(2 * s) & jnp.logical_not(blk(s))
        xm = [_mm(tm[j], jnp.where(off, d_ab[j], 0.0), NN, p) for j in tiles]
        tm = [tm[j] + _mm(xm[j], tm[j], NN, p) for j in tiles]
    y = [_mm(tm[j], y0[j], NN, p) for j in tiles]
    if SOLVE_REFINE:
        a_low = [jnp.where(strict, d_ab[j], 0.0) for j in tiles]
        rsd = [y0[j] - y[j] + _mm(a_low[j], y[j], NN, 3) for j in tiles]
        y = [y[j] + _mm(tm[j], rsd[j], NN, p) for j in tiles]

    rs, u_st = [], []
    for j in tiles:
        ah_st = jnp.concatenate([jnp.where(m_a, y[j][:c], 0.0), jnp.where(m_a, 0.0, y[j][c:])], axis=0)
        u_loc = jnp.concatenate([jnp.where(m_a, 0.0, y[j][:c]), jnp.where(m_a, y[j][c:], 0.0)], axis=0)
        rs_as = _dot(jnp.concatenate([r_st[j], bf(ah_st)], axis=0), bf(st_ref[j]))
        rs.append(rs_as[:2 * c])
        u_st.append(rs_as[2 * c:] + u_loc)
    for j in tiles:
        o_st = rs[j] + _dot(d_r[j], jnp.concatenate([bf(u_st[j]), v_st[j]], axis=0))
        o_ref[:, sls[j]] = pltpu.roll(o_st[:c] + o_st[c:], 64, axis=1)
    for j in tiles:
        u_sw = u_st[j][:c] + u_st[j][c:]
        delta = _dot(bk_tail[j], jnp.concatenate([bf(u_sw), v_sw[j]], axis=0), TN)
        w_col = jnp.sum(jnp.where(eye, jnp.broadcast_to(w_c[j], (2 * c, 2 * c)), 0.0), axis=1, keepdims=True)
        st_ref[j] = w_col * st_ref[j] + jnp.where(anti, delta, 0.0)


def _wkv(r, lw, k, v, kk, b):
    t, wdt = r.shape
    spec = pl.BlockSpec((CHUNK, wdt), lambda i: (i, 0))
    return pl.pallas_call(
        _wkv_kernel,
        grid=(t // CHUNK,),
        in_specs=[spec] * 6,
        out_specs=spec,
        out_shape=jax.ShapeDtypeStruct((t, wdt), F32),
        scratch_shapes=[pltpu.VMEM((N_TILES, LANES, LANES), F32)],
        compiler_params=pltpu.CompilerParams(
            dimension_semantics=("arbitrary",), vmem_limit_bytes=VMEM_LIMIT),
        name="wkv",
    )(r, lw, k, v, kk, b)


def _out_proj_kernel(ya_ref, o_ref, bon_ref, g_ref, gn_ref, wa_ref, wr_ref, x_ref, mod_ref, out_ref):
    ones = _head_ones()
    gn_w, gn_b = gn_ref[0:1, :], gn_ref[1:2, :]
    acc = _dot(ya_ref[...].astype(BF16), wa_ref[...])
    for j in range(N_TILES):
        sl = slice(j * LANES, (j + 1) * LANES)
        o = o_ref[:, sl]
        mu = _mm_exact_rhs(o, ones) * (1.0 / HEAD_DIM)
        oc = o - mu
        var = _mm_exact_rhs(oc * oc, ones) * (1.0 / HEAD_DIM)
        y = (oc * lax.rsqrt(var + RWKV_GN_EPS) * gn_w[:, sl] + gn_b[:, sl] + bon_ref[:, sl]) * g_ref[:, sl]
        acc = acc + _dot(y.astype(BF16), wr_ref[sl, :])
    z = DEEPNORM_ALPHA * x_ref[...] + mod_ref[0:1, :] * acc
    out_ref[...] = _layer_norm(z, mod_ref[1:2, :], mod_ref[2:3, :])


def _out_proj(y_attn, o, bonus, g, gn, wa, wr, x, mod3):
    t, d = x.shape
    tm = 256
    wdt = RWKV_WIDTH
    row = lambda width: pl.BlockSpec((tm, width), lambda i: (i, 0))
    full = lambda a: pl.BlockSpec(a.shape, lambda i: (0, 0))
    return pl.pallas_call(
        _out_proj_kernel,
        grid=(t // tm,),
        in_specs=[row(ATTN_WIDTH), row(wdt), row(wdt), row(wdt), full(gn), full(wa), full(wr), row(d),
                  full(mod3)],
        out_specs=row(d),
        out_shape=jax.ShapeDtypeStruct((t, d), F32),
        compiler_params=pltpu.CompilerParams(
            dimension_semantics=("parallel",), vmem_limit_bytes=VMEM_LIMIT),
        name="out_proj",
    )(y_attn, o, bonus, g, gn, wa, wr, x, mod3)


def _ffn_kernel(x_ref, mod_ref, wg_ref, wu_ref, wd_ref, out_ref, u_ref, acc_ref):
    j = pl.program_id(1)

    @pl.when(j == 0)
    def _():
        u_ref[...] = (x_ref[...] * (1.0 + mod_ref[1:2, :]) + mod_ref[0:1, :]).astype(BF16)
        acc_ref[...] = jnp.zeros_like(acc_ref)

    u = u_ref[...]
    gate = _dot(u, wg_ref[...])
    up = _dot(u, wu_ref[...])
    h = (gate * jax.nn.sigmoid(gate) * up).astype(BF16)
    acc_ref[...] += _dot(h, wd_ref[...])

    @pl.when(j == pl.num_programs(1) - 1)
    def _():
        z = DEEPNORM_ALPHA * x_ref[...] + mod_ref[2:3, :] * acc_ref[...]
        out_ref[...] = _layer_norm(z, mod_ref[3:4, :], mod_ref[4:5, :])


def _ffn(x, mod5, wg, wu, wd):
    t, d = x.shape
    f = wg.shape[1]
    tm, tf = 512, 512
    return pl.pallas_call(
        _ffn_kernel,
        grid=(t // tm, f // tf),
        in_specs=[pl.BlockSpec((tm, d), lambda i, j: (i, 0)),
                  pl.BlockSpec(mod5.shape, lambda i, j: (0, 0)),
                  pl.BlockSpec((d, tf), lambda i, j: (0, j)),
                  pl.BlockSpec((d, tf), lambda i, j: (0, j)),
                  pl.BlockSpec((tf, d), lambda i, j: (j, 0))],
        out_specs=pl.BlockSpec((tm, d), lambda i, j: (i, 0)),
        out_shape=jax.ShapeDtypeStruct((t, d), F32),
        scratch_shapes=[pltpu.VMEM((tm, d), BF16), pltpu.VMEM((tm, d), F32)],
        compiler_params=pltpu.CompilerParams(
            dimension_semantics=("parallel", "arbitrary"), vmem_limit_bytes=VMEM_LIMIT),
        name="ffn",
    )(x, mod5, wg, wu, wd)


def _attn_permutations():
    q_cols, heads = [], []
    for tile in range(ATTN_HEADS // 2):
        m, i = divmod(tile, 4)
        h_a, h_b = 8 * m + i, 8 * m + 4 + i
        heads += [h_a, h_b]
        for half in range(2):
            for h in (h_a, h_b):
                q_cols += list(range(h * HEAD_DIM + half * HALF, h * HEAD_DIM + (half + 1) * HALF))
    k_cols = []
    for m in range(KV_HEADS // 2):
        for half in range(2):
            for h in (2 * m, 2 * m + 1):
                k_cols += list(range(h * HEAD_DIM + half * HALF, h * HEAD_DIM + (half + 1) * HALF))
    out_rows = []
    for h in heads:
        out_rows += list(range(h * HEAD_DIM, (h + 1) * HEAD_DIM))
    return (np.asarray(q_cols, np.int32), np.asarray(k_cols, np.int32),
            np.asarray(heads, np.int32), np.asarray(out_rows, np.int32))


def _pad_rows(w, rows, offset=0):
    out = jnp.zeros((rows, w.shape[1]), w.dtype)
    return lax.dynamic_update_slice(out, w, (offset, 0))


def kernel(x, c, positions, w_ada, b_ada, w_in, attn_sinks, rwkv_mu, rwkv_w0, rwkv_w2, rwkv_a0, rwkv_a2, rwkv_g2, rwkv_k_k, rwkv_k_a, rwkv_r_k, rwkv_gn_w, rwkv_gn_b, rwkv_v0, rwkv_v1, rwkv_v2, w_out, ln1_w, ln1_b, w_gate_up, w_down, ln2_w, ln2_b):
    bsz, t, d = x.shape
    assert bsz == 1 and d == D_MODEL
    depth = w_in.shape[0]
    q_cols, k_cols, heads, out_rows = _attn_permutations()

    inv = ROPE_THETA ** (-jnp.arange(0, HEAD_DIM, 2, dtype=F32) / HEAD_DIM)
    ang = positions[0].astype(F32)[:, None] * inv
    cos, sin = jnp.cos(ang), jnp.sin(ang)
    cos_t = jnp.concatenate([cos, cos, cos, cos], axis=-1)
    sin_t = jnp.concatenate([-sin, -sin, sin, sin], axis=-1)

    mod = _adaln(c, w_ada, b_ada)
    xs = x[0]
    v_first = None
    rw = RWKV_WIDTH
    r0 = ATTN_WIDTH + 2 * KV_WIDTH
    for l in range(depth):
        sh_m, sc_m, gt_m, sh_f, sc_f, gt_f = [mod[l, i] for i in range(N_MOD)]
        wl = w_in[l]
        w_cat = jnp.concatenate([
            wl[:, q_cols],
            wl[:, r0:r0 + 3 * rw],
            wl[:, ATTN_WIDTH + k_cols],
            wl[:, ATTN_WIDTH + KV_WIDTH:r0],
            wl[:, r0 + 3 * rw:],
            jnp.zeros((d, N_PROJ - COL_LORA - (DECAY_LORA + AAA_LORA + GATE_LORA)), F32)],
            axis=1).astype(BF16)
        proj = _in_proj(xs, sc_m, sh_m, w_cat)

        y_attn = _attention(proj, attn_sinks[l][heads], cos_t, sin_t)

        mu = rwkv_mu[l]
        mu3 = mu[:3 * rw].reshape(3, rw)
        mu_lora = jnp.pad(mu[3 * rw:], (0, LORA_PAD - (mu.shape[0] - 3 * rw))).reshape(1, LORA_PAD)
        zero = jnp.zeros((rw,), F32)
        params = jnp.stack([rwkv_w0[l], rwkv_a0[l], rwkv_k_k[l], rwkv_k_a[l], rwkv_r_k[l].reshape(rw),
                            rwkv_v0[l - 1] if l > 0 else zero, zero, zero])
        w2p = _pad_rows(rwkv_w2[l], LANES, 0)
        a2p = _pad_rows(rwkv_a2[l], LANES, DECAY_LORA)
        g2p = _pad_rows(rwkv_g2[l], 2 * LANES, 0)
        if l == 0:
            vres = None
        else:
            v1p = jnp.pad(rwkv_v1[l - 1], ((0, 0), (0, LANES - MV_LORA)))
            v2p = _pad_rows(rwkv_v2[l - 1], LANES, 0)
            vres = (v_first, v1p, v2p)
        r, lw, k_mod, v, kk, b, g, bonus = _rwkv_prep(proj, mu3, mu_lora, params, w2p, a2p, g2p, vres)
        if l == 0:
            v_first = v
        o = _wkv(r, lw, k_mod, v, kk, b)

        gn = jnp.stack([rwkv_gn_w[l], rwkv_gn_b[l]])
        wa = w_out[l][:ATTN_WIDTH][out_rows].astype(BF16)
        wr = w_out[l][ATTN_WIDTH:].astype(BF16)
        mod3 = jnp.concatenate([gt_m, ln1_w[l][None], ln1_b[l][None]], axis=0)
        xs = _out_proj(y_attn, o, bonus, g, gn, wa, wr, xs, mod3)

        mod5 = jnp.concatenate([sh_f, sc_f, gt_f, ln2_w[l][None], ln2_b[l][None]], axis=0)
        wg = w_gate_up[l][:, :FFN_HIDDEN].astype(BF16)
        wu = w_gate_up[l][:, FFN_HIDDEN:].astype(BF16)
        xs = _ffn(xs, mod5, wg, wu, w_down[l].astype(BF16))
    return xs[None]
```

```python
import functools

import numpy as np
import jax
import jax.numpy as jnp
from jax import lax
from jax.experimental import pallas as pl
from jax.experimental.pallas import tpu as pltpu

F32 = jnp.float32
BF16 = jnp.bfloat16

D_MODEL = 2048
DEPTH = 2
HEAD_DIM = 64
HALF = HEAD_DIM // 2
ATTN_WIDTH = 1024
ATTN_HEADS = 16
KV_HEADS = 4
KV_WIDTH = KV_HEADS * HEAD_DIM
ATTN_COLS = ATTN_WIDTH + 2 * KV_WIDTH
WINDOW = 128
ROPE_THETA = 10000.0
RWKV_WIDTH = 1024
DECAY_LORA = 64
AAA_LORA = 64
GATE_LORA = 160
LORA_COLS = DECAY_LORA + AAA_LORA + GATE_LORA
MV_LORA = 32
RWKV_GN_EPS = 64e-5
FFN_HIDDEN = 5632
LN_EPS = 1e-5
N_MOD = 6
DEEPNORM_ALPHA = (2.0 * DEPTH) ** 0.25

LANES = 128
N_TILES = RWKV_WIDTH // LANES
CHUNK = 64
VMEM_LIMIT = 56 * 1024 * 1024

COL_Q, COL_AK, COL_AV, COL_LORA = 0, 1024, 1280, 1536
COL_R, COL_K, COL_V = 2048, 3072, 4096
N_SMALL = 2048
N_PROJ = 5120
LORA_PAD = 512
PROJ_TN = 512

NN = ((1,), (0,))
NT = ((1,), (1,))
TN = ((0,), (0,))


def _dot(a, b, dims=NN):
    return lax.dot_general(a, b, (dims, ((), ())), preferred_element_type=F32)


def _bf(z):
    return z.astype(BF16)


def _mm(a, b):
    return _dot(_bf(a), _bf(b))


def _cumsum_rows(tri_bf16, b):
    b0 = _bf(b)
    r1 = b - b0.astype(F32)
    b1 = _bf(r1)
    b2 = _bf(r1 - b1.astype(F32))
    return _dot(tri_bf16, b0) + (_dot(tri_bf16, b1) + _dot(tri_bf16, b2))


def _head_sum(a, ones_bf16):
    a0 = _bf(a)
    a1 = _bf(a - a0.astype(F32))
    return _dot(a0, ones_bf16) + _dot(a1, ones_bf16)


def _iota(shape, dim):
    return lax.broadcasted_iota(jnp.int32, shape, dim)


def _softplus(z):
    return jnp.maximum(z, 0.0) + jnp.log(1.0 + jnp.exp(-jnp.abs(z)))


def _layer_norm(z, w, b):
    mu = jnp.mean(z, axis=-1, keepdims=True)
    zc = z - mu
    var = jnp.mean(zc * zc, axis=-1, keepdims=True)
    return zc * lax.rsqrt(var + LN_EPS) * w + b


def _params(**kw):
    return pltpu.CompilerParams(vmem_limit_bytes=VMEM_LIMIT, **kw)


def _adaln_kernel(c_ref, w_ref, b_ref, o_ref, *, tn):
    cb = c_ref[...]
    cond = cb * jax.nn.sigmoid(cb)
    for j in range(tn // LANES):
        sl = slice(j * LANES, (j + 1) * LANES)
        o_ref[:, sl] = jnp.sum(w_ref[:, sl] * cond, axis=0, keepdims=True) + b_ref[:, sl]


def _adaln(c, w_ada, b_ada):
    depth, d, n = w_ada.shape
    tn = 1024
    c_b = jnp.broadcast_to(c.reshape(d, 1), (d, LANES))
    out = pl.pallas_call(
        functools.partial(_adaln_kernel, tn=tn),
        grid=(depth, n // tn),
        in_specs=[pl.BlockSpec((d, LANES), lambda l, j: (0, 0)),
                  pl.BlockSpec((None, d, tn), lambda l, j: (l, 0, j)),
                  pl.BlockSpec((None, 1, tn), lambda l, j: (l, 0, j))],
        out_specs=pl.BlockSpec((None, 1, tn), lambda l, j: (l, 0, j)),
        out_shape=jax.ShapeDtypeStruct((depth, 1, n), F32),
        compiler_params=_params(dimension_semantics=("parallel", "parallel")),
        name="adaln",
    )(c_b, w_ada, b_ada.reshape(depth, 1, n))
    return out.reshape(depth, N_MOD, d)


def _in_proj_kernel(x_ref, mod_ref, ws_ref, wb_ref, o_ref, u_ref, *, n_small):
    j = pl.program_id(1)

    @pl.when(j == 0)
    def _():
        u_ref[...] = _bf(x_ref[...] * (1.0 + mod_ref[1:2, :]) + mod_ref[0:1, :])

    @pl.when(j < n_small)
    def _():
        o_ref[...] = _dot(u_ref[...], ws_ref[...])

    @pl.when(j >= n_small)
    def _():
        o_ref[...] = _dot(u_ref[...], _bf(wb_ref[...]))


def _in_proj(x, mod, w_small, w_in, l):
    t, d = x.shape
    tm, tn = 1024, PROJ_TN
    n_small = N_SMALL // tn
    first_big = ATTN_COLS // tn
    return pl.pallas_call(
        functools.partial(_in_proj_kernel, n_small=n_small),
        grid=(t // tm, N_PROJ // tn),
        in_specs=[pl.BlockSpec((tm, d), lambda i, j: (i, 0)),
                  pl.BlockSpec((None, N_MOD, d), lambda i, j: (l, 0, 0)),
                  pl.BlockSpec((None, d, tn), lambda i, j: (l, 0, jnp.minimum(j, n_small - 1))),
                  pl.BlockSpec((None, d, tn), lambda i, j: (l, 0, jnp.maximum(j - n_small, 0) + first_big))],
        out_specs=pl.BlockSpec((tm, tn), lambda i, j: (i, j)),
        out_shape=jax.ShapeDtypeStruct((t, N_PROJ), F32),
        scratch_shapes=[pltpu.VMEM((tm, d), BF16)],
        compiler_params=_params(dimension_semantics=("parallel", "arbitrary")),
        name="in_proj",
    )(x, mod, w_small, w_in)


def _rope(t, cos, sin):
    return t * cos + pltpu.roll(t, 64, axis=1) * sin


def _attn_kernel(sink_ref, q_ref, kc_ref, kp_ref, vc_ref, vp_ref, cosc_ref, sinc_ref, cosp_ref, sinp_ref,
                 o_ref, *, l):
    blk = pl.program_id(0)
    w = WINDOW
    cos_c, sin_c = cosc_ref[...], sinc_ref[...]
    cos_p, sin_p = cosp_ref[...], sinp_ref[...]

    lane = _iota((w, LANES), 1)
    in_a_qk = (lane % HEAD_DIM) < HALF
    in_a_v = lane < HEAD_DIM

    qa = _iota((2 * w, 2 * w), 0) % w
    ks = _iota((2 * w, 2 * w), 1)
    valid = (ks > qa) & (ks <= qa + w) & ((ks >= w) | (blk > 0))
    row_is_a = _iota((2 * w, 1), 0) < w

    for m in range(KV_HEADS // 2):
        sl_kv = slice(m * LANES, (m + 1) * LANES)
        k_ctx = _bf(jnp.concatenate([_rope(kp_ref[:, sl_kv], cos_p, sin_p),
                                     _rope(kc_ref[:, sl_kv], cos_c, sin_c)], axis=0))
        v_ctx = _bf(jnp.concatenate([vp_ref[:, sl_kv], vc_ref[:, sl_kv]], axis=0))
        for i in range(4):
            tile = 4 * m + i
            sl = slice(tile * LANES, (tile + 1) * LANES)
            q = _rope(q_ref[:, sl], cos_c, sin_c) * (HEAD_DIM ** -0.5)
            q_st = _bf(jnp.concatenate([jnp.where(in_a_qk, q, 0.0), jnp.where(in_a_qk, 0.0, q)], axis=0))
            s = _dot(q_st, k_ctx, NT)
            s = jnp.where(valid, s, -1e30)
            sink = jnp.where(row_is_a, sink_ref[l * ATTN_HEADS + 2 * tile],
                             sink_ref[l * ATTN_HEADS + 2 * tile + 1])
            mx = jnp.maximum(jnp.max(s, axis=-1, keepdims=True), sink)
            p = jnp.exp(s - mx)
            den = jnp.sum(p, axis=-1, keepdims=True) + jnp.exp(sink - mx)
            o_st = _dot(_bf(p), v_ctx) * (1.0 / den)
            o_ref[:, sl] = _bf(jnp.where(in_a_v, o_st[:w], o_st[w:]))


def _attention(proj, sinks_perm, cos_t, sin_t, l):
    t = proj.shape[0]
    w = WINDOW
    prev = lambda i, s: (jnp.maximum(i - 1, 0), 0)
    cur = lambda i, s: (i, 0)
    return pl.pallas_call(
        functools.partial(_attn_kernel, l=l),
        grid_spec=pltpu.PrefetchScalarGridSpec(
            num_scalar_prefetch=1,
            grid=(t // w,),
            in_specs=[pl.BlockSpec((w, ATTN_WIDTH), lambda i, s: (i, COL_Q // ATTN_WIDTH)),
                      pl.BlockSpec((w, KV_WIDTH), lambda i, s: (i, COL_AK // KV_WIDTH)),
                      pl.BlockSpec((w, KV_WIDTH), lambda i, s: (jnp.maximum(i - 1, 0), COL_AK // KV_WIDTH)),
                      pl.BlockSpec((w, KV_WIDTH), lambda i, s: (i, COL_AV // KV_WIDTH)),
                      pl.BlockSpec((w, KV_WIDTH), lambda i, s: (jnp.maximum(i - 1, 0), COL_AV // KV_WIDTH)),
                      pl.BlockSpec((w, LANES), cur), pl.BlockSpec((w, LANES), cur),
                      pl.BlockSpec((w, LANES), prev), pl.BlockSpec((w, LANES), prev)],
            out_specs=pl.BlockSpec((w, ATTN_WIDTH), lambda i, s: (i, 0)),
        ),
        out_shape=jax.ShapeDtypeStruct((t, ATTN_WIDTH), BF16),
        compiler_params=_params(dimension_semantics=("parallel",)),
        name="attention",
    )(sinks_perm, proj, proj, proj, proj, proj, cos_t, sin_t, cos_t, sin_t)


def _prep_kernel(*refs, has_vres):
    if has_vres:
        (r_ref, k_ref, v_ref, l_ref, rp_ref, kp_ref, vp_ref, lp_ref, mu_ref, mul_ref, par_ref,
         w2_ref, a2_ref, g2_ref, ones_ref, vf_ref, v1_ref, v2_ref,
         ro_ref, lw_ref, ko_ref, vo_ref, kk_ref, b_ref, g_ref, bon_ref) = refs
    else:
        (r_ref, k_ref, v_ref, l_ref, rp_ref, kp_ref, vp_ref, lp_ref, mu_ref, mul_ref, par_ref,
         w2_ref, a2_ref, g2_ref, ones_ref,
         ro_ref, lw_ref, ko_ref, vo_ref, kk_ref, b_ref, g_ref, bon_ref) = refs
    first = pl.program_id(0) == 0

    def shifted_lerp(x_ref, p_ref, mu):
        x = x_ref[...]
        prev_row = jnp.where(first, 0.0, p_ref[7:8, :])
        row = _iota(x.shape, 0)
        xp = jnp.where(row == 0, prev_row, pltpu.roll(x, 1, axis=0))
        return x + (xp - x) * mu

    r = shifted_lerp(r_ref, rp_ref, mu_ref[0:1, :])
    k = shifted_lerp(k_ref, kp_ref, mu_ref[1:2, :])
    v = shifted_lerp(v_ref, vp_ref, mu_ref[2:3, :])
    lo = shifted_lerp(l_ref, lp_ref, mul_ref[...])

    w0, a0 = par_ref[0:1, :], par_ref[1:2, :]
    k_k, k_a, r_k = par_ref[2:3, :], par_ref[3:4, :], par_ref[4:5, :]

    wa = lo[:, 0:LANES]
    gd = lo[:, LANES:3 * LANES]
    w_pre = w0 + _mm(jnp.tanh(wa), w2_ref[...])
    w = -_softplus(-w_pre) - 0.5
    a = jax.nn.sigmoid(a0 + _mm(wa, a2_ref[...]))
    g = _mm(jax.nn.sigmoid(gd), g2_ref[...])
    if has_vres:
        v0 = par_ref[5:6, :]
        mix = jax.nn.sigmoid(v0 + _mm(_mm(v, v1_ref[...]), v2_ref[...]))
        v = v + (vf_ref[...] - v) * mix

    kk = k * k_k
    k_mod = k * (1.0 + (a - 1.0) * k_a)
    rk = r * k_mod * r_k
    ones = ones_ref[...]
    for j in range(N_TILES):
        sl = slice(j * LANES, (j + 1) * LANES)
        kk_j = kk[:, sl]
        ss = _head_sum(kk_j * kk_j, ones)
        kk_n = kk_j / jnp.maximum(jnp.sqrt(ss), 1e-12)
        kk_ref[:, sl] = kk_n
        b_ref[:, sl] = kk_n * a[:, sl]
        bon_ref[:, sl] = _head_sum(rk[:, sl], ones) * v[:, sl]
    ro_ref[...] = r
    lw_ref[...] = -jnp.exp(w)
    ko_ref[...] = k_mod
    vo_ref[...] = v
    g_ref[...] = g


def _rwkv_prep(proj, mu3, mu_lora, params, w2p, a2p, g2p, ones, vres, l):
    t = proj.shape[0]
    tm = 256
    wdt = RWKV_WIDTH
    has_vres = vres is not None

    def col(cb, width):
        return pl.BlockSpec((tm, width), lambda i: (i, cb))

    def prev(cb, width):
        return pl.BlockSpec((8, width), lambda i: (jnp.maximum(i * (tm // 8) - 1, 0), cb))

    def layer(a, lyr=l):
        return pl.BlockSpec((None,) + a.shape[1:], lambda i: (lyr, 0, 0))

    in_specs = [col(COL_R // wdt, wdt), col(COL_K // wdt, wdt), col(COL_V // wdt, wdt),
                col(COL_LORA // LORA_PAD, LORA_PAD),
                prev(COL_R // wdt, wdt), prev(COL_K // wdt, wdt), prev(COL_V // wdt, wdt),
                prev(COL_LORA // LORA_PAD, LORA_PAD),
                layer(mu3), layer(mu_lora), layer(params), layer(w2p), layer(a2p), layer(g2p),
                pl.BlockSpec(ones.shape, lambda i: (0, 0))]
    args = [proj] * 8 + [mu3, mu_lora, params, w2p, a2p, g2p, ones]
    if has_vres:
        v_first, v1p, v2p = vres
        in_specs += [pl.BlockSpec((tm, wdt), lambda i: (i, 0)), layer(v1p, l - 1), layer(v2p, l - 1)]
        args += [v_first, v1p, v2p]
    out_spec = pl.BlockSpec((tm, wdt), lambda i: (i, 0))
    out_sds = jax.ShapeDtypeStruct((t, wdt), F32)
    return pl.pallas_call(
        functools.partial(_prep_kernel, has_vres=has_vres),
        grid=(t // tm,),
        in_specs=in_specs,
        out_specs=[out_spec] * 8,
        out_shape=[out_sds] * 8,
        compiler_params=_params(dimension_semantics=("parallel",)),
        name="rwkv_prep",
    )(*args)


def _wkv_kernel(r_ref, lw_ref, k_ref, v_ref, kk_ref, b_ref, o_ref, st_ref):
    c = CHUNK

    @pl.when(pl.program_id(0) == 0)
    def _():
        st_ref[...] = jnp.zeros_like(st_ref)

    row_t = _iota((c, c), 0)
    col_t = _iota((c, c), 1)
    l_incl = _bf(jnp.where(row_t >= col_t, 1.0, 0.0))
    cum_all = _cumsum_rows(l_incl, lw_ref[...])

    lane = _iota((c, LANES), 1)
    m_a = lane < HEAD_DIM
    r2 = _iota((2 * c, 2 * c), 0)
    c2 = _iota((2 * c, 2 * c), 1)
    strict = r2 > c2
    incl = r2 >= c2
    eye = r2 == c2
    anti = (r2 < c) != (c2 < c)

    def stack(x):
        return jnp.concatenate([jnp.where(m_a, x, 0.0), jnp.where(m_a, 0.0, x)], axis=0)

    def stack_sw(x):
        return jnp.concatenate([jnp.where(m_a, 0.0, x), jnp.where(m_a, x, 0.0)], axis=0)

    tiles = range(N_TILES)
    sls = [slice(j * LANES, (j + 1) * LANES) for j in tiles]

    a_st, r_st, rhs, v_sw, v_st, bk_tail, w_c = [], [], [], [], [], [], []
    for j in tiles:
        sl = sls[j]
        lw = lw_ref[:, sl]
        cum = cum_all[:, sl]
        cum_last = cum[c - 1:c, :]
        e_neg = jnp.exp(-cum)
        e_tail = jnp.exp(cum_last - cum)
        k = k_ref[:, sl]
        b = b_ref[:, sl]
        a_st.append(stack(-kk_ref[:, sl] * jnp.exp(cum - lw)))
        r_st.append(_bf(stack(r_ref[:, sl] * jnp.exp(cum))))
        rhs.append(_bf(jnp.concatenate([stack(b * e_neg), stack(k * e_neg)], axis=0)))
        vs = pltpu.roll(v_ref[:, sl], 64, axis=1)
        v_sw.append(_bf(vs))
        v_st.append(_bf(stack_sw(vs)))
        bk_tail.append(_bf(jnp.concatenate([b * e_tail, k * e_tail], axis=0)))
        w_c.append(jnp.exp(cum_last))

    incl2 = jnp.concatenate([incl, incl], axis=1)
    d_ab, d_ak, d_r = [], [], []
    for j in tiles:
        gm = _dot(jnp.concatenate([_bf(a_st[j]), r_st[j]], axis=0), rhs[j], NT)
        d_ab.append(gm[:2 * c, :2 * c])
        d_ak.append(_bf(jnp.where(strict, gm[:2 * c, 2 * c:], 0.0)))
        d_r.append(_bf(jnp.where(incl2, gm[2 * c:, :], 0.0)))

    y0 = [a_st[j] + _dot(d_ak[j], v_st[j]) for j in tiles]
    blk = lambda s: (r2 // s) == (c2 // s)
    diag8 = strict & blk(8)
    d1 = [jnp.where(diag8, d_ab[j], 0.0) for j in tiles]
    p1 = [jnp.where(eye, 1.0, d1[j]) for j in tiles]
    d2 = [_mm(d1[j], d1[j]) for j in tiles]
    res = [_mm(d2[j], jnp.concatenate([p1[j], d2[j]], axis=1)) for j in tiles]
    p2 = [p1[j] + res[j][:, :2 * c] for j in tiles]
    tm = [p2[j] + _mm(res[j][:, 2 * c:], p2[j]) for j in tiles]
    for s in (8, 16, 32):
        off = strict & blk(2 * s) & jnp.logical_not(blk(s))
        xm = [_mm(tm[j], jnp.where(off, d_ab[j], 0.0)) for j in tiles]
        tm = [tm[j] + _mm(xm[j], tm[j]) for j in tiles]
    y = [_mm(tm[j], y0[j]) for j in tiles]

    rs, u_st = [], []
    for j in tiles:
        ah_st = jnp.concatenate([jnp.where(m_a, y[j][:c], 0.0), jnp.where(m_a, 0.0, y[j][c:])], axis=0)
        u_loc = jnp.concatenate([jnp.where(m_a, 0.0, y[j][:c]), jnp.where(m_a, y[j][c:], 0.0)], axis=0)
        rs_as = _dot(jnp.concatenate([r_st[j], _bf(ah_st)], axis=0), _bf(st_ref[j]))
        rs.append(rs_as[:2 * c])
        u_st.append(rs_as[2 * c:] + u_loc)
    for j in tiles:
        o_st = rs[j] + _dot(d_r[j], jnp.concatenate([_bf(u_st[j]), v_st[j]], axis=0))
        o_ref[:, sls[j]] = pltpu.roll(o_st[:c] + o_st[c:], 64, axis=1)
    for j in tiles:
        u_sw = u_st[j][:c] + u_st[j][c:]
        delta = _dot(bk_tail[j], jnp.concatenate([_bf(u_sw), v_sw[j]], axis=0), TN)
        w_col = jnp.sum(jnp.where(eye, jnp.broadcast_to(w_c[j], (2 * c, 2 * c)), 0.0), axis=1, keepdims=True)
        st_ref[j] = w_col * st_ref[j] + jnp.where(anti, delta, 0.0)


def _wkv(r, lw, k, v, kk, b):
    t, wdt = r.shape
    spec = pl.BlockSpec((CHUNK, wdt), lambda i: (i, 0))
    return pl.pallas_call(
        _wkv_kernel,
        grid=(t // CHUNK,),
        in_specs=[spec] * 6,
        out_specs=spec,
        out_shape=jax.ShapeDtypeStruct((t, wdt), F32),
        scratch_shapes=[pltpu.VMEM((N_TILES, LANES, LANES), F32)],
        compiler_params=_params(dimension_semantics=("arbitrary",)),
        name="wkv",
    )(r, lw, k, v, kk, b)


def _out_proj_kernel(ya_ref, o_ref, bon_ref, g_ref, gn_ref, ones_ref, w_ref, x_ref, mod_ref, ln_ref, out_ref,
                     lhs_ref):
    ones = ones_ref[...]
    lhs_ref[:, :ATTN_WIDTH] = ya_ref[...]
    for j in range(N_TILES):
        sl = slice(j * LANES, (j + 1) * LANES)
        o = o_ref[:, sl]
        mu = _head_sum(o, ones) * (1.0 / HEAD_DIM)
        oc = o - mu
        var = _head_sum(oc * oc, ones) * (1.0 / HEAD_DIM)
        y = (oc * lax.rsqrt(var + RWKV_GN_EPS) * gn_ref[0:1, sl] + gn_ref[1:2, sl] + bon_ref[:, sl]) * g_ref[:, sl]
        lhs_ref[:, ATTN_WIDTH + j * LANES:ATTN_WIDTH + (j + 1) * LANES] = _bf(y)
    mix = _dot(lhs_ref[...], w_ref[...])
    z = DEEPNORM_ALPHA * x_ref[...] + mod_ref[2:3, :] * mix
    out_ref[...] = _layer_norm(z, ln_ref[0:1, :], ln_ref[1:2, :])


def _out_proj(y_attn, o, bonus, g, gn, ones, w_cat, x, mod, ln, l):
    t, d = x.shape
    tm = 512
    wdt = RWKV_WIDTH
    row = lambda width: pl.BlockSpec((tm, width), lambda i: (i, 0))
    layer = lambda a: pl.BlockSpec((None,) + a.shape[1:], lambda i: (l, 0, 0))
    return pl.pallas_call(
        _out_proj_kernel,
        grid=(t // tm,),
        in_specs=[row(ATTN_WIDTH), row(wdt), row(wdt), row(wdt), layer(gn),
                  pl.BlockSpec(ones.shape, lambda i: (0, 0)), layer(w_cat), row(d), layer(mod), layer(ln)],
        out_specs=row(d),
        out_shape=jax.ShapeDtypeStruct((t, d), F32),
        scratch_shapes=[pltpu.VMEM((tm, ATTN_WIDTH + wdt), BF16)],
        compiler_params=_params(dimension_semantics=("parallel",)),
        name="out_proj",
    )(y_attn, o, bonus, g, gn, ones, w_cat, x, mod, ln)


def _ffn_kernel(x_ref, mod_ref, ln_ref, wg_ref, wu_ref, wd_ref, out_ref, u_ref, acc_ref):
    j = pl.program_id(1)

    @pl.when(j == 0)
    def _():
        u_ref[...] = _bf(x_ref[...] * (1.0 + mod_ref[4:5, :]) + mod_ref[3:4, :])
        acc_ref[...] = jnp.zeros_like(acc_ref)

    u = u_ref[...]
    gate = _dot(u, wg_ref[...])
    up = _dot(u, wu_ref[...])
    h = _bf(gate * jax.nn.sigmoid(gate) * up)
    acc_ref[...] += _dot(h, wd_ref[...])

    @pl.when(j == pl.num_programs(1) - 1)
    def _():
        z = DEEPNORM_ALPHA * x_ref[...] + mod_ref[5:6, :] * acc_ref[...]
        out_ref[...] = _layer_norm(z, ln_ref[2:3, :], ln_ref[3:4, :])


def _ffn(x, mod, ln, w_gu, w_dn, l):
    t, d = x.shape
    f = w_dn.shape[1]
    tm, tf = 512, 512
    nf = f // tf
    return pl.pallas_call(
        _ffn_kernel,
        grid=(t // tm, nf),
        in_specs=[pl.BlockSpec((tm, d), lambda i, j: (i, 0)),
                  pl.BlockSpec((None,) + mod.shape[1:], lambda i, j: (l, 0, 0)),
                  pl.BlockSpec((None,) + ln.shape[1:], lambda i, j: (l, 0, 0)),
                  pl.BlockSpec((None, d, tf), lambda i, j: (l, 0, j)),
                  pl.BlockSpec((None, d, tf), lambda i, j: (l, 0, j + nf)),
                  pl.BlockSpec((None, tf, d), lambda i, j: (l, j, 0))],
        out_specs=pl.BlockSpec((tm, d), lambda i, j: (i, 0)),
        out_shape=jax.ShapeDtypeStruct((t, d), F32),
        scratch_shapes=[pltpu.VMEM((tm, d), BF16), pltpu.VMEM((tm, d), F32)],
        compiler_params=_params(dimension_semantics=("parallel", "arbitrary")),
        name="ffn",
    )(x, mod, ln, w_gu, w_gu, w_dn)


def _attn_permutations():
    q_cols, heads = [], []
    for tile in range(ATTN_HEADS // 2):
        m, i = divmod(tile, 4)
        h_a, h_b = 8 * m + i, 8 * m + 4 + i
        heads += [h_a, h_b]
        for half in range(2):
            for h in (h_a, h_b):
                q_cols += list(range(h * HEAD_DIM + half * HALF, h * HEAD_DIM + (half + 1) * HALF))
    k_cols = []
    for m in range(KV_HEADS // 2):
        for half in range(2):
            for h in (2 * m, 2 * m + 1):
                k_cols += list(range(h * HEAD_DIM + half * HALF, h * HEAD_DIM + (half + 1) * HALF))
    out_rows = []
    for h in heads:
        out_rows += list(range(h * HEAD_DIM, (h + 1) * HEAD_DIM))
    return (np.asarray(q_cols, np.int32), np.asarray(k_cols, np.int32),
            np.asarray(heads, np.int32), np.asarray(out_rows, np.int32))


def _pad_axis(a, axis, before, total):
    pads = [(0, 0)] * a.ndim
    pads[axis] = (before, total - before - a.shape[axis])
    return jnp.pad(a, pads)


def kernel(x, c, positions, w_ada, b_ada, w_in, attn_sinks, rwkv_mu, rwkv_w0, rwkv_w2, rwkv_a0, rwkv_a2, rwkv_g2, rwkv_k_k, rwkv_k_a, rwkv_r_k, rwkv_gn_w, rwkv_gn_b, rwkv_v0, rwkv_v1, rwkv_v2, w_out, ln1_w, ln1_b, w_gate_up, w_down, ln2_w, ln2_b):
    bsz, t, d = x.shape
    assert bsz == 1 and d == D_MODEL
    depth = w_in.shape[0]
    rw = RWKV_WIDTH
    q_cols, k_cols, heads, out_rows = _attn_permutations()

    inv = ROPE_THETA ** (-jnp.arange(0, HEAD_DIM, 2, dtype=F32) / HEAD_DIM)
    ang = positions[0].astype(F32)[:, None] * inv
    cos, sin = jnp.cos(ang), jnp.sin(ang)
    cos_t = jnp.concatenate([cos, cos, cos, cos], axis=-1)
    sin_t = jnp.concatenate([-sin, -sin, sin, sin], axis=-1)

    small_cols = np.concatenate([q_cols, ATTN_WIDTH + k_cols,
                                 np.arange(ATTN_WIDTH + KV_WIDTH, ATTN_COLS),
                                 np.arange(ATTN_COLS + 3 * rw, ATTN_COLS + 3 * rw + LORA_COLS)])
    w_small = _pad_axis(w_in[:, :, small_cols], 2, 0, N_SMALL).astype(BF16)
    sinks_perm = attn_sinks[:, heads].reshape(-1)
    mu3 = rwkv_mu[:, :3 * rw].reshape(depth, 3, rw)
    mu_lora = _pad_axis(rwkv_mu[:, 3 * rw:], 1, 0, LORA_PAD).reshape(depth, 1, LORA_PAD)
    zero = jnp.zeros((depth, rw), F32)
    v0_all = _pad_axis(rwkv_v0, 0, 1, depth)
    params = jnp.stack([rwkv_w0, rwkv_a0, rwkv_k_k, rwkv_k_a, rwkv_r_k.reshape(depth, rw), v0_all, zero, zero],
                       axis=1)
    w2p = _pad_axis(rwkv_w2, 1, 0, LANES)
    a2p = _pad_axis(rwkv_a2, 1, DECAY_LORA, LANES)
    g2p = _pad_axis(rwkv_g2, 1, 0, 2 * LANES)
    v1p = _pad_axis(rwkv_v1, 2, 0, LANES)
    v2p = _pad_axis(rwkv_v2, 1, 0, LANES)
    seg = np.arange(LANES) // HEAD_DIM
    ones = jnp.asarray(seg[:, None] == seg[None, :], BF16)
    gn = jnp.stack([rwkv_gn_w, rwkv_gn_b], axis=1)
    w_out_cat = jnp.concatenate([w_out[:, :ATTN_WIDTH][:, out_rows], w_out[:, ATTN_WIDTH:]], axis=1).astype(BF16)
    ln = jnp.stack([ln1_w, ln1_b, ln2_w, ln2_b], axis=1)
    w_gu = w_gate_up.astype(BF16)
    w_dn = w_down.astype(BF16)

    mod = _adaln(c, w_ada, b_ada)
    xs = x[0]
    v_first = None
    for l in range(depth):
        proj = _in_proj(xs, mod, w_small, w_in, l)
        y_attn = _attention(proj, sinks_perm, cos_t, sin_t, l)
        vres = None if l == 0 else (v_first, v1p, v2p)
        r, lw, k_mod, v, kk, b, g, bonus = _rwkv_prep(proj, mu3, mu_lora, params, w2p, a2p, g2p, ones, vres, l)
        if l == 0:
            v_first = v
        o = _wkv(r, lw, k_mod, v, kk, b)
        xs = _out_proj(y_attn, o, bonus, g, gn, ones, w_out_cat, xs, mod, ln, l)
        xs = _ffn(xs, mod, ln, w_gu, w_dn, l)
    return xs[None]
```

```python
import functools

import numpy as np
import jax
import jax.numpy as jnp
from jax import lax
from jax.experimental import pallas as pl
from jax.experimental.pallas import tpu as pltpu

F32 = jnp.float32
BF16 = jnp.bfloat16

D_MODEL = 2048
DEPTH = 2
HEAD_DIM = 64
HALF = HEAD_DIM // 2
ATTN_WIDTH = 1024
ATTN_HEADS = 16
KV_HEADS = 4
KV_WIDTH = KV_HEADS * HEAD_DIM
ATTN_COLS = ATTN_WIDTH + 2 * KV_WIDTH
WINDOW = 128
ROPE_THETA = 10000.0
RWKV_WIDTH = 1024
DECAY_LORA = 64
AAA_LORA = 64
GATE_LORA = 160
LORA_COLS = DECAY_LORA + AAA_LORA + GATE_LORA
MV_LORA = 32
RWKV_GN_EPS = 64e-5
FFN_HIDDEN = 5632
LN_EPS = 1e-5
N_MOD = 6
DEEPNORM_ALPHA = (2.0 * DEPTH) ** 0.25

LANES = 128
N_TILES = RWKV_WIDTH // LANES
CHUNK = 64
WKV_SUB = 2
VMEM_LIMIT = 56 * 1024 * 1024

COL_Q, COL_AK, COL_AV, COL_LORA = 0, 1024, 1280, 1536
COL_R, COL_K, COL_V = 2048, 3072, 4096
N_PROJ = 5120
LORA_PAD = 512
SLAB = 256
PROJ_TN = 512

NN = ((1,), (0,))
NT = ((1,), (1,))
TN = ((0,), (0,))


def _dot(a, b, dims=NN):
    return lax.dot_general(a, b, (dims, ((), ())), preferred_element_type=F32)


def _bf(z):
    return z.astype(BF16)


def _mm(a, b):
    return _dot(_bf(a), _bf(b))


def _cumsum_rows(tri_bf16, b):
    b0 = _bf(b)
    r1 = b - b0.astype(F32)
    b1 = _bf(r1)
    b2 = _bf(r1 - b1.astype(F32))
    return _dot(tri_bf16, b0) + (_dot(tri_bf16, b1) + _dot(tri_bf16, b2))


def _head_sum(a, ones_bf16):
    a0 = _bf(a)
    a1 = _bf(a - a0.astype(F32))
    return _dot(a0, ones_bf16) + _dot(a1, ones_bf16)


def _pair_dots(lhs, rhs):
    out = []
    for p in range(0, len(lhs), 2):
        r0, r1 = _bf(rhs[p]), _bf(rhs[p + 1])
        z = jnp.zeros_like(r0)
        bd = jnp.concatenate([jnp.concatenate([r0, z], axis=1), jnp.concatenate([z, r1], axis=1)], axis=0)
        res = _dot(jnp.concatenate([_bf(lhs[p]), _bf(lhs[p + 1])], axis=1), bd)
        n = r0.shape[1]
        out += [res[:, :n], res[:, n:]]
    return out


def _iota(shape, dim):
    return lax.broadcasted_iota(jnp.int32, shape, dim)


def _softplus(z):
    return jnp.maximum(z, 0.0) + jnp.log(1.0 + jnp.exp(-jnp.abs(z)))


def _layer_norm(z, w, b):
    mu = jnp.mean(z, axis=-1, keepdims=True)
    zc = z - mu
    var = jnp.mean(zc * zc, axis=-1, keepdims=True)
    return zc * lax.rsqrt(var + LN_EPS) * w + b


def _params(**kw):
    return pltpu.CompilerParams(vmem_limit_bytes=VMEM_LIMIT, **kw)


def _adaln_kernel(c_ref, w_ref, b_ref, o_ref, *, tn):
    cb = c_ref[...]
    cond = cb * jax.nn.sigmoid(cb)
    for j in range(tn // LANES):
        sl = slice(j * LANES, (j + 1) * LANES)
        o_ref[:, sl] = jnp.sum(w_ref[:, sl] * cond, axis=0, keepdims=True) + b_ref[:, sl]


def _adaln(c, w_ada, b_ada):
    depth, d, n = w_ada.shape
    tn = 1024
    c_b = jnp.broadcast_to(c.reshape(d, 1), (d, LANES))
    out = pl.pallas_call(
        functools.partial(_adaln_kernel, tn=tn),
        grid=(depth, n // tn),
        in_specs=[pl.BlockSpec((d, LANES), lambda l, j: (0, 0)),
                  pl.BlockSpec((None, d, tn), lambda l, j: (l, 0, j)),
                  pl.BlockSpec((None, 1, tn), lambda l, j: (l, 0, j))],
        out_specs=pl.BlockSpec((None, 1, tn), lambda l, j: (l, 0, j)),
        out_shape=jax.ShapeDtypeStruct((depth, 1, n), F32),
        compiler_params=_params(dimension_semantics=("parallel", "parallel")),
        name="adaln",
    )(c_b, w_ada, b_ada.reshape(depth, 1, n))
    return out.reshape(depth, N_MOD, d)


def _in_proj_kernel(x_ref, mod_ref, w_ref, o_ref, u_ref):
    @pl.when(pl.program_id(1) == 0)
    def _():
        u_ref[...] = _bf(x_ref[...] * (1.0 + mod_ref[1:2, :]) + mod_ref[0:1, :])

    o_ref[...] = _dot(u_ref[...], w_ref[...])


def _in_proj(x, mod, w_cat, l):
    t, d = x.shape
    tm, tn = 1024, PROJ_TN
    return pl.pallas_call(
        _in_proj_kernel,
        grid=(t // tm, N_PROJ // tn),
        in_specs=[pl.BlockSpec((tm, d), lambda i, j: (i, 0)),
                  pl.BlockSpec((None, N_MOD, d), lambda i, j: (l, 0, 0)),
                  pl.BlockSpec((None, d, tn), lambda i, j: (l, 0, j))],
        out_specs=pl.BlockSpec((tm, tn), lambda i, j: (i, j)),
        out_shape=jax.ShapeDtypeStruct((t, N_PROJ), F32),
        scratch_shapes=[pltpu.VMEM((tm, d), BF16)],
        compiler_params=_params(dimension_semantics=("parallel", "arbitrary")),
        name="in_proj",
    )(x, mod, w_cat)


def _rope(t, cos, sin):
    return t * cos + pltpu.roll(t, 64, axis=1) * sin


def _attn_kernel(sink_ref, q_ref, kc_ref, kp_ref, vc_ref, vp_ref, cosc_ref, sinc_ref, cosp_ref, sinp_ref,
                 o_ref, *, l):
    blk = pl.program_id(0)
    w = WINDOW
    cos_c, sin_c = cosc_ref[...], sinc_ref[...]
    cos_p, sin_p = cosp_ref[...], sinp_ref[...]

    lane = _iota((w, LANES), 1)
    in_a_qk = (lane % HEAD_DIM) < HALF
    in_a_v = lane < HEAD_DIM

    qa = _iota((2 * w, 2 * w), 0) % w
    ks = _iota((2 * w, 2 * w), 1)
    valid = (ks > qa) & (ks <= qa + w) & ((ks >= w) | (blk > 0))
    row_is_a = _iota((2 * w, 1), 0) < w

    for m in range(KV_HEADS // 2):
        sl_kv = slice(m * LANES, (m + 1) * LANES)
        k_ctx = _bf(jnp.concatenate([_rope(kp_ref[:, sl_kv], cos_p, sin_p),
                                     _rope(kc_ref[:, sl_kv], cos_c, sin_c)], axis=0))
        v_ctx = _bf(jnp.concatenate([vp_ref[:, sl_kv], vc_ref[:, sl_kv]], axis=0))
        for i in range(4):
            tile = 4 * m + i
            sl = slice(tile * LANES, (tile + 1) * LANES)
            q = _rope(q_ref[:, sl], cos_c, sin_c) * (HEAD_DIM ** -0.5)
            q_st = _bf(jnp.concatenate([jnp.where(in_a_qk, q, 0.0), jnp.where(in_a_qk, 0.0, q)], axis=0))
            s = _dot(q_st, k_ctx, NT)
            s = jnp.where(valid, s, -1e30)
            sink = jnp.where(row_is_a, sink_ref[l * ATTN_HEADS + 2 * tile],
                             sink_ref[l * ATTN_HEADS + 2 * tile + 1])
            mx = jnp.maximum(jnp.max(s, axis=-1, keepdims=True), sink)
            p = jnp.exp(s - mx)
            den = jnp.sum(p, axis=-1, keepdims=True) + jnp.exp(sink - mx)
            o_st = _dot(_bf(p), v_ctx) * (1.0 / den)
            o_ref[:, sl] = _bf(jnp.where(in_a_v, o_st[:w], o_st[w:]))


def _attention(proj, sinks_perm, cos_t, sin_t, l):
    t = proj.shape[0]
    w = WINDOW
    prev = lambda i, s: (jnp.maximum(i - 1, 0), 0)
    cur = lambda i, s: (i, 0)
    return pl.pallas_call(
        functools.partial(_attn_kernel, l=l),
        grid_spec=pltpu.PrefetchScalarGridSpec(
            num_scalar_prefetch=1,
            grid=(t // w,),
            in_specs=[pl.BlockSpec((w, ATTN_WIDTH), lambda i, s: (i, COL_Q // ATTN_WIDTH)),
                      pl.BlockSpec((w, KV_WIDTH), lambda i, s: (i, COL_AK // KV_WIDTH)),
                      pl.BlockSpec((w, KV_WIDTH), lambda i, s: (jnp.maximum(i - 1, 0), COL_AK // KV_WIDTH)),
                      pl.BlockSpec((w, KV_WIDTH), lambda i, s: (i, COL_AV // KV_WIDTH)),
                      pl.BlockSpec((w, KV_WIDTH), lambda i, s: (jnp.maximum(i - 1, 0), COL_AV // KV_WIDTH)),
                      pl.BlockSpec((w, LANES), cur), pl.BlockSpec((w, LANES), cur),
                      pl.BlockSpec((w, LANES), prev), pl.BlockSpec((w, LANES), prev)],
            out_specs=pl.BlockSpec((w, ATTN_WIDTH), lambda i, s: (i, 0)),
        ),
        out_shape=jax.ShapeDtypeStruct((t, ATTN_WIDTH), BF16),
        compiler_params=_params(dimension_semantics=("parallel",)),
        name="attention",
    )(sinks_perm, proj, proj, proj, proj, proj, cos_t, sin_t, cos_t, sin_t)


def _prep_kernel(*refs, has_vres):
    if has_vres:
        (r_ref, k_ref, v_ref, l_ref, rp_ref, kp_ref, vp_ref, lp_ref, mu_ref, mul_ref, par_ref,
         w2_ref, a2_ref, g2_ref, ones_ref, vf_ref, v1_ref, v2_ref,
         ro_ref, lw_ref, ko_ref, vo_ref, kk_ref, b_ref, g_ref, bon_ref) = refs
    else:
        (r_ref, k_ref, v_ref, l_ref, rp_ref, kp_ref, vp_ref, lp_ref, mu_ref, mul_ref, par_ref,
         w2_ref, a2_ref, g2_ref, ones_ref,
         ro_ref, lw_ref, ko_ref, vo_ref, kk_ref, b_ref, g_ref, bon_ref) = refs
    first = pl.program_id(0) == 0

    def shifted_lerp(x_ref, p_ref, mu):
        x = x_ref[...]
        prev_row = jnp.where(first, 0.0, p_ref[7:8, :])
        row = _iota(x.shape, 0)
        xp = jnp.where(row == 0, prev_row, pltpu.roll(x, 1, axis=0))
        return x + (xp - x) * mu

    r = shifted_lerp(r_ref, rp_ref, mu_ref[0:1, :])
    k = shifted_lerp(k_ref, kp_ref, mu_ref[1:2, :])
    v = shifted_lerp(v_ref, vp_ref, mu_ref[2:3, :])
    lo = shifted_lerp(l_ref, lp_ref, mul_ref[...])

    w0, a0 = par_ref[0:1, :], par_ref[1:2, :]
    k_k, k_a, r_k = par_ref[2:3, :], par_ref[3:4, :], par_ref[4:5, :]

    wa = lo[:, 0:LANES]
    gd = lo[:, LANES:3 * LANES]
    w_pre = w0 + _mm(jnp.tanh(wa), w2_ref[...])
    w = -_softplus(-w_pre) - 0.5
    a = jax.nn.sigmoid(a0 + _mm(wa, a2_ref[...]))
    g = _mm(jax.nn.sigmoid(gd), g2_ref[...])
    if has_vres:
        v0 = par_ref[5:6, :]
        mix = jax.nn.sigmoid(v0 + _mm(_mm(v, v1_ref[...]), v2_ref[...]))
        v = v + (vf_ref[...].astype(F32) - v) * mix

    kk = k * k_k
    k_mod = k * (1.0 + (a - 1.0) * k_a)
    rk = r * k_mod * r_k
    ones = ones_ref[...]
    for j in range(RWKV_WIDTH // SLAB):
        sl = slice(j * SLAB, (j + 1) * SLAB)
        kk_j = kk[:, sl]
        ss = _mm(kk_j * kk_j, ones)
        kk_n = kk_j * lax.rsqrt(jnp.maximum(ss, 1e-24))
        kk_ref[:, sl] = _bf(kk_n)
        b_ref[:, sl] = _bf(kk_n * a[:, sl])
        bon_ref[:, sl] = _bf(_mm(rk[:, sl], ones) * v[:, sl])
    ro_ref[...] = _bf(r)
    lw_ref[...] = -jnp.exp(w)
    ko_ref[...] = _bf(k_mod)
    vo_ref[...] = _bf(v)
    g_ref[...] = _bf(g)


def _rwkv_prep(proj, mu3, mu_lora, params, w2p, a2p, g2p, ones, vres, l):
    t = proj.shape[0]
    tm = 256
    wdt = RWKV_WIDTH
    has_vres = vres is not None

    def col(cb, width):
        return pl.BlockSpec((tm, width), lambda i: (i, cb))

    def prev(cb, width):
        return pl.BlockSpec((8, width), lambda i: (jnp.maximum(i * (tm // 8) - 1, 0), cb))

    def layer(a, lyr=l):
        return pl.BlockSpec((None,) + a.shape[1:], lambda i: (lyr, 0, 0))

    in_specs = [col(COL_R // wdt, wdt), col(COL_K // wdt, wdt), col(COL_V // wdt, wdt),
                col(COL_LORA // LORA_PAD, LORA_PAD),
                prev(COL_R // wdt, wdt), prev(COL_K // wdt, wdt), prev(COL_V // wdt, wdt),
                prev(COL_LORA // LORA_PAD, LORA_PAD),
                layer(mu3), layer(mu_lora), layer(params), layer(w2p), layer(a2p), layer(g2p),
                pl.BlockSpec(ones.shape, lambda i: (0, 0))]
    args = [proj] * 8 + [mu3, mu_lora, params, w2p, a2p, g2p, ones]
    if has_vres:
        v_first, v1p, v2p = vres
        in_specs += [pl.BlockSpec((tm, wdt), lambda i: (i, 0)), layer(v1p, l - 1), layer(v2p, l - 1)]
        args += [v_first, v1p, v2p]
    out_spec = pl.BlockSpec((tm, wdt), lambda i: (i, 0))
    out_dtypes = [BF16, F32, BF16, BF16, BF16, BF16, BF16, BF16]
    return pl.pallas_call(
        functools.partial(_prep_kernel, has_vres=has_vres),
        grid=(t // tm,),
        in_specs=in_specs,
        out_specs=[out_spec] * 8,
        out_shape=[jax.ShapeDtypeStruct((t, wdt), dt) for dt in out_dtypes],
        compiler_params=_params(dimension_semantics=("parallel",)),
        name="rwkv_prep",
    )(*args)


def _wkv_kernel(r_ref, lw_ref, k_ref, v_ref, kk_ref, b_ref, o_ref, st_ref):
    c = CHUNK

    @pl.when(pl.program_id(0) == 0)
    def _():
        st_ref[...] = jnp.zeros_like(st_ref)

    row_t = _iota((c, c), 0)
    col_t = _iota((c, c), 1)
    l_incl = _bf(jnp.where(row_t >= col_t, 1.0, 0.0))

    lane = _iota((c, LANES), 1)
    m_a = lane < HEAD_DIM
    r2 = _iota((2 * c, 2 * c), 0)
    c2 = _iota((2 * c, 2 * c), 1)
    strict = r2 > c2
    incl = r2 >= c2
    eye = r2 == c2
    anti = (r2 < c) != (c2 < c)

    def stack(x):
        return jnp.concatenate([jnp.where(m_a, x, 0.0), jnp.where(m_a, 0.0, x)], axis=0)

    def stack_sw(x):
        return jnp.concatenate([jnp.where(m_a, 0.0, x), jnp.where(m_a, x, 0.0)], axis=0)

    tiles = range(N_TILES)
    sls = [slice(j * LANES, (j + 1) * LANES) for j in tiles]

    units = [(s, j) for s in range(WKV_SUB) for j in tiles]
    un = range(len(units))

    a_st, r_st, rhs, v_sw, v_st, bk_tail, w_c = [], [], [], [], [], [], []
    for s in range(WKV_SUB):
        rows = slice(s * c, (s + 1) * c)
        cum_all = _cumsum_rows(l_incl, lw_ref[rows, :])
        for j in tiles:
            sl = sls[j]
            lw = lw_ref[rows, sl]
            cum = cum_all[:, sl]
            cum_last = cum[c - 1:c, :]
            e_neg = jnp.exp(-cum)
            e_tail = jnp.exp(cum_last - cum)
            k = k_ref[rows, sl].astype(F32)
            b = b_ref[rows, sl].astype(F32)
            a_st.append(stack(-kk_ref[rows, sl].astype(F32) * jnp.exp(cum - lw)))
            r_st.append(_bf(stack(r_ref[rows, sl].astype(F32) * jnp.exp(cum))))
            rhs.append(_bf(jnp.concatenate([stack(b * e_neg), stack(k * e_neg)], axis=0)))
            vs = pltpu.roll(v_ref[rows, sl].astype(F32), 64, axis=1)
            v_sw.append(_bf(vs))
            v_st.append(_bf(stack_sw(vs)))
            bk_tail.append(_bf(jnp.concatenate([b * e_tail, k * e_tail], axis=0)))
            w_c.append(jnp.exp(cum_last))

    incl2 = jnp.concatenate([incl, incl], axis=1)
    d_ab, d_ak, d_r = [], [], []
    for u in un:
        gm = _dot(jnp.concatenate([_bf(a_st[u]), r_st[u]], axis=0), rhs[u], NT)
        d_ab.append(gm[:2 * c, :2 * c])
        d_ak.append(_bf(jnp.where(strict, gm[:2 * c, 2 * c:], 0.0)))
        d_r.append(_bf(jnp.where(incl2, gm[2 * c:, :], 0.0)))

    av = _pair_dots(d_ak, v_st)
    y0 = [a_st[u] + av[u] for u in un]
    blk = lambda s: (r2 // s) == (c2 // s)
    diag8 = strict & blk(8)
    d1 = [jnp.where(diag8, d_ab[u], 0.0) for u in un]
    p1 = [jnp.where(eye, 1.0, d1[u]) for u in un]
    d2 = _pair_dots(d1, d1)
    res = [_mm(d2[u], jnp.concatenate([p1[u], d2[u]], axis=1)) for u in un]
    p2 = [p1[u] + res[u][:, :2 * c] for u in un]
    d4p2 = _pair_dots([res[u][:, 2 * c:] for u in un], p2)
    tm = [p2[u] + d4p2[u] for u in un]
    for s in (8, 16, 32):
        off = strict & blk(2 * s) & jnp.logical_not(blk(s))
        xm = _pair_dots(tm, [jnp.where(off, d_ab[u], 0.0) for u in un])
        xt = _pair_dots(xm, tm)
        tm = [tm[u] + xt[u] for u in un]
    y = _pair_dots(tm, y0)

    st = [st_ref[j] for j in tiles]
    for s in range(WKV_SUB):
        rows = slice(s * c, (s + 1) * c)
        base = s * N_TILES
        ra, u_loc = [], []
        for j in tiles:
            yj = y[base + j]
            ah_st = jnp.concatenate([jnp.where(m_a, yj[:c], 0.0), jnp.where(m_a, 0.0, yj[c:])], axis=0)
            u_loc.append(jnp.concatenate([jnp.where(m_a, 0.0, yj[:c]), jnp.where(m_a, yj[c:], 0.0)], axis=0))
            ra.append(jnp.concatenate([r_st[base + j], _bf(ah_st)], axis=0))
        rs_as = _pair_dots(ra, st)
        u_st = [rs_as[j][2 * c:] + u_loc[j] for j in tiles]
        for j in tiles:
            o_st = rs_as[j][:2 * c] + _dot(d_r[base + j], jnp.concatenate([_bf(u_st[j]), v_st[base + j]], axis=0))
            o_ref[rows, sls[j]] = pltpu.roll(o_st[:c] + o_st[c:], 64, axis=1)
        new_st = []
        for j in tiles:
            u_sw = u_st[j][:c] + u_st[j][c:]
            delta = _dot(bk_tail[base + j], jnp.concatenate([_bf(u_sw), v_sw[base + j]], axis=0), TN)
            w_col = jnp.sum(jnp.where(eye, jnp.broadcast_to(w_c[base + j], (2 * c, 2 * c)), 0.0),
                            axis=1, keepdims=True)
            new_st.append(w_col * st[j] + jnp.where(anti, delta, 0.0))
        st = new_st
    for j in tiles:
        st_ref[j] = st[j]


def _wkv(r, lw, k, v, kk, b):
    t, wdt = r.shape
    rows = WKV_SUB * CHUNK
    spec = pl.BlockSpec((rows, wdt), lambda i: (i, 0))
    return pl.pallas_call(
        _wkv_kernel,
        grid=(t // rows,),
        in_specs=[spec] * 6,
        out_specs=spec,
        out_shape=jax.ShapeDtypeStruct((t, wdt), F32),
        scratch_shapes=[pltpu.VMEM((N_TILES, LANES, LANES), F32)],
        compiler_params=_params(dimension_semantics=("arbitrary",)),
        name="wkv",
    )(r, lw, k, v, kk, b)


def _out_proj_kernel(ya_ref, o_ref, bon_ref, g_ref, gn_ref, ones_ref, w_ref, x_ref, mod_ref, ln_ref, out_ref,
                     *, tm, halves):
    ones = ones_ref[...]
    hm = tm // halves
    for h in range(halves):
        rows = slice(h * hm, (h + 1) * hm)
        ys = []
        for j in range(RWKV_WIDTH // SLAB):
            sl = slice(j * SLAB, (j + 1) * SLAB)
            o = o_ref[rows, sl]
            mu = _head_sum(o, ones) * (1.0 / HEAD_DIM)
            oc = o - mu
            var = _mm(oc * oc, ones) * (1.0 / HEAD_DIM)
            y = ((oc * lax.rsqrt(var + RWKV_GN_EPS) * gn_ref[0:1, sl] + gn_ref[1:2, sl]
                  + bon_ref[rows, sl].astype(F32)) * g_ref[rows, sl].astype(F32))
            ys.append(_bf(y))
        mix = (_dot(ya_ref[rows, :], w_ref[:ATTN_WIDTH, :])
               + _dot(jnp.concatenate(ys, axis=1), w_ref[ATTN_WIDTH:, :]))
        z = DEEPNORM_ALPHA * x_ref[rows, :] + mod_ref[2:3, :] * mix
        out_ref[rows, :] = _layer_norm(z, ln_ref[0:1, :], ln_ref[1:2, :])


def _out_proj(y_attn, o, bonus, g, gn, ones, w_cat, x, mod, ln, l):
    t, d = x.shape
    tm = 512
    wdt = RWKV_WIDTH
    row = lambda width: pl.BlockSpec((tm, width), lambda i: (i, 0))
    layer = lambda a: pl.BlockSpec((None,) + a.shape[1:], lambda i: (l, 0, 0))
    return pl.pallas_call(
        functools.partial(_out_proj_kernel, tm=tm, halves=1),
        grid=(t // tm,),
        in_specs=[row(ATTN_WIDTH), row(wdt), row(wdt), row(wdt), layer(gn),
                  pl.BlockSpec(ones.shape, lambda i: (0, 0)), layer(w_cat), row(d), layer(mod), layer(ln)],
        out_specs=row(d),
        out_shape=jax.ShapeDtypeStruct((t, d), F32),
        compiler_params=_params(dimension_semantics=("parallel",)),
        name="out_proj",
    )(y_attn, o, bonus, g, gn, ones, w_cat, x, mod, ln)


def _ffn_kernel(x_ref, mod_ref, ln_ref, wg_ref, wu_ref, wd_ref, out_ref, u_ref, acc_ref):
    j = pl.program_id(1)

    @pl.when(j == 0)
    def _():
        u_ref[...] = _bf(x_ref[...] * (1.0 + mod_ref[4:5, :]) + mod_ref[3:4, :])
        acc_ref[...] = jnp.zeros_like(acc_ref)

    u = u_ref[...]
    gate = _dot(u, wg_ref[...])
    up = _dot(u, wu_ref[...])
    h = _bf(gate * jax.nn.sigmoid(gate) * up)
    acc_ref[...] += _dot(h, wd_ref[...])

    @pl.when(j == pl.num_programs(1) - 1)
    def _():
        z = DEEPNORM_ALPHA * x_ref[...] + mod_ref[5:6, :] * acc_ref[...]
        out_ref[...] = _layer_norm(z, ln_ref[2:3, :], ln_ref[3:4, :])


def _ffn(x, mod, ln, w_gu, w_dn, l):
    t, d = x.shape
    f = w_dn.shape[1]
    tm, tf = 512, 512
    nf = f // tf
    return pl.pallas_call(
        _ffn_kernel,
        grid=(t // tm, nf),
        in_specs=[pl.BlockSpec((tm, d), lambda i, j: (i, 0)),
                  pl.BlockSpec((None,) + mod.shape[1:], lambda i, j: (l, 0, 0)),
                  pl.BlockSpec((None,) + ln.shape[1:], lambda i, j: (l, 0, 0)),
                  pl.BlockSpec((None, d, tf), lambda i, j: (l, 0, j)),
                  pl.BlockSpec((None, d, tf), lambda i, j: (l, 0, j + nf)),
                  pl.BlockSpec((None, tf, d), lambda i, j: (l, j, 0))],
        out_specs=pl.BlockSpec((tm, d), lambda i, j: (i, 0)),
        out_shape=jax.ShapeDtypeStruct((t, d), F32),
        scratch_shapes=[pltpu.VMEM((tm, d), BF16), pltpu.VMEM((tm, d), F32)],
        compiler_params=_params(dimension_semantics=("parallel", "arbitrary")),
        name="ffn",
    )(x, mod, ln, w_gu, w_gu, w_dn)


def _regroup_attention_weights(w_in, w_out, attn_sinks):
    depth, d, _ = w_in.shape
    wq = w_in[:, :, :ATTN_WIDTH].reshape(depth, d, 2, 2, 4, 2, HALF)
    wq = wq.transpose(0, 1, 2, 4, 5, 3, 6).reshape(depth, d, ATTN_WIDTH)
    wk = w_in[:, :, ATTN_WIDTH:ATTN_WIDTH + KV_WIDTH].reshape(depth, d, 2, 2, 2, HALF)
    wk = wk.transpose(0, 1, 2, 4, 3, 5).reshape(depth, d, KV_WIDTH)
    wo = w_out[:, :ATTN_WIDTH].reshape(depth, 2, 2, 4, HEAD_DIM, w_out.shape[2])
    wo = wo.transpose(0, 1, 3, 2, 4, 5).reshape(depth, ATTN_WIDTH, w_out.shape[2])
    sinks = attn_sinks.reshape(depth, 2, 2, 4).transpose(0, 1, 3, 2).reshape(-1)
    return wq, wk, wo, sinks


def _pad_axis(a, axis, before, total):
    pads = [(0, 0)] * a.ndim
    pads[axis] = (before, total - before - a.shape[axis])
    return jnp.pad(a, pads)


def kernel(x, c, positions, w_ada, b_ada, w_in, attn_sinks, rwkv_mu, rwkv_w0, rwkv_w2, rwkv_a0, rwkv_a2, rwkv_g2, rwkv_k_k, rwkv_k_a, rwkv_r_k, rwkv_gn_w, rwkv_gn_b, rwkv_v0, rwkv_v1, rwkv_v2, w_out, ln1_w, ln1_b, w_gate_up, w_down, ln2_w, ln2_b):
    bsz, t, d = x.shape
    assert bsz == 1 and d == D_MODEL
    depth = w_in.shape[0]
    rw = RWKV_WIDTH
    inv =ROPE_THETA ** (-jnp.arange(0, HEAD_DIM, 2, dtype=F32) / HEAD_DIM)
    ang = positions[0].astype(F32)[:, None] * inv
    cos, sin = jnp.cos(ang), jnp.sin(ang)
    cos_t = jnp.concatenate([cos, cos, cos, cos], axis=-1)
    sin_t = jnp.concatenate([-sin, -sin, sin, sin], axis=-1)

    wq, wk, wo, sinks_perm = _regroup_attention_weights(w_in, w_out, attn_sinks)
    w_cat = jnp.concatenate([_bf(wq), _bf(wk), _bf(w_in[:, :, ATTN_WIDTH + KV_WIDTH:ATTN_COLS]),
                             _bf(w_in[:, :, ATTN_COLS + 3 * rw:]),
                             jnp.zeros((depth, d, COL_R - COL_LORA - LORA_COLS), BF16),
                             _bf(w_in[:, :, ATTN_COLS:ATTN_COLS + 3 * rw])], axis=2)
    mu3 = rwkv_mu[:, :3 * rw].reshape(depth, 3, rw)
    mu_lora = _pad_axis(rwkv_mu[:, 3 * rw:], 1, 0, LORA_PAD).reshape(depth, 1, LORA_PAD)
    zero = jnp.zeros((depth, rw), F32)
    v0_all = _pad_axis(rwkv_v0, 0, 1, depth)
    params = jnp.stack([rwkv_w0, rwkv_a0, rwkv_k_k, rwkv_k_a, rwkv_r_k.reshape(depth, rw), v0_all, zero, zero],
                       axis=1)
    w2p = _pad_axis(rwkv_w2, 1, 0, LANES)
    a2p = _pad_axis(rwkv_a2, 1, DECAY_LORA, LANES)
    g2p = _pad_axis(rwkv_g2, 1, 0, 2 * LANES)
    v1p = _pad_axis(rwkv_v1, 2, 0, LANES)
    v2p = _pad_axis(rwkv_v2, 1, 0, LANES)
    seg = np.arange(SLAB) // HEAD_DIM
    ones = jnp.asarray(seg[:, None] == seg[None, :], BF16)
    gn = jnp.stack([rwkv_gn_w, rwkv_gn_b], axis=1)
    w_out_cat = jnp.concatenate([_bf(wo), _bf(w_out[:, ATTN_WIDTH:])], axis=1)
    ln = jnp.stack([ln1_w, ln1_b, ln2_w, ln2_b], axis=1)
    w_gu = w_gate_up.astype(BF16)
    w_dn = w_down.astype(BF16)

    mod = _adaln(c, w_ada, b_ada)
    xs = x[0]
    v_first = None
    for l in range(depth):
        proj = _in_proj(xs, mod, w_cat, l)
        y_attn = _attention(proj, sinks_perm, cos_t, sin_t, l)
        vres = None if l == 0 else (v_first, v1p, v2p)
        r, lw, k_mod, v, kk, b, g, bonus = _rwkv_prep(proj, mu3, mu_lora, params, w2p, a2p, g2p, ones, vres, l)
        if l == 0:
            v_first = v
        o = _wkv(r, lw, k_mod, v, kk, b)
        xs = _out_proj(y_attn, o, bonus, g, gn, ones, w_out_cat, xs, mod, ln, l)
        xs = _ffn(xs, mod, ln, w_gu, w_dn, l)
    return xs[None]
```

```python
import functools

import numpy as np
import jax
import jax.numpy as jnp
from jax import lax
from jax.experimental import pallas as pl
from jax.experimental.pallas import tpu as pltpu

F32 = jnp.float32
BF16 = jnp.bfloat16

D_MODEL = 2048
DEPTH = 2
HEAD_DIM = 64
HALF = HEAD_DIM // 2
ATTN_WIDTH = 1024
ATTN_HEADS = 16
KV_HEADS = 4
KV_WIDTH = KV_HEADS * HEAD_DIM
ATTN_COLS = ATTN_WIDTH + 2 * KV_WIDTH
WINDOW = 128
ROPE_THETA = 10000.0
RWKV_WIDTH = 1024
DECAY_LORA = 64
AAA_LORA = 64
GATE_LORA = 160
LORA_COLS = DECAY_LORA + AAA_LORA + GATE_LORA
MV_LORA = 32
RWKV_GN_EPS = 64e-5
FFN_HIDDEN = 5632
LN_EPS = 1e-5
N_MOD = 6
DEEPNORM_ALPHA = (2.0 * DEPTH) ** 0.25

LANES = 128
N_TILES = RWKV_WIDTH // LANES
CHUNK = 64
WKV_SUB = 2
VMEM_LIMIT = 56 * 1024 * 1024

COL_Q, COL_AK, COL_AV, COL_LORA = 0, 1024, 1280, 1536
COL_R, COL_K, COL_V = 2048, 3072, 4096
N_PROJ = 5120
LORA_PAD = 512
SLAB = 256
PROJ_TN = 512

NN = ((1,), (0,))
NT = ((1,), (1,))
TN = ((0,), (0,))


def _dot(a, b, dims=NN):
    return lax.dot_general(a, b, (dims, ((), ())), preferred_element_type=F32)


def _bf(z):
    return z.astype(BF16)


def _mm(a, b):
    return _dot(_bf(a), _bf(b))


def _cumsum_rows(tri_bf16, b):
    b0 = _bf(b)
    r1 = b - b0.astype(F32)
    b1 = _bf(r1)
    b2 = _bf(r1 - b1.astype(F32))
    return _dot(tri_bf16, b0) + (_dot(tri_bf16, b1) + _dot(tri_bf16, b2))


def _head_sum(a, ones_bf16):
    a0 = _bf(a)
    a1 = _bf(a - a0.astype(F32))
    return _dot(a0, ones_bf16) + _dot(a1, ones_bf16)


def _pair_dots(lhs, rhs):
    out = []
    for p in range(0, len(lhs), 2):
        r0, r1 = _bf(rhs[p]), _bf(rhs[p + 1])
        z = jnp.zeros_like(r0)
        bd = jnp.concatenate([jnp.concatenate([r0, z], axis=1), jnp.concatenate([z, r1], axis=1)], axis=0)
        res = _dot(jnp.concatenate([_bf(lhs[p]), _bf(lhs[p + 1])], axis=1), bd)
        n = r0.shape[1]
        out += [res[:, :n], res[:, n:]]
    return out


def _iota(shape, dim):
    return lax.broadcasted_iota(jnp.int32, shape, dim)


def _softplus(z):
    return jnp.maximum(z, 0.0) + jnp.log(1.0 + jnp.exp(-jnp.abs(z)))


def _layer_norm(z, w, b):
    mu = jnp.mean(z, axis=-1, keepdims=True)
    zc = z - mu
    var = jnp.mean(zc * zc, axis=-1, keepdims=True)
    return zc * lax.rsqrt(var + LN_EPS) * w + b


def _params(**kw):
    return pltpu.CompilerParams(vmem_limit_bytes=VMEM_LIMIT, **kw)


def _adaln_kernel(c_ref, w_ref, b_ref, o_ref, *, tn):
    cb = c_ref[...]
    cond = cb * jax.nn.sigmoid(cb)
    for j in range(tn // LANES):
        sl = slice(j * LANES, (j + 1) * LANES)
        o_ref[:, sl] = jnp.sum(w_ref[:, sl] * cond, axis=0, keepdims=True) + b_ref[:, sl]


def _adaln(c, w_ada, b_ada):
    depth, d, n = w_ada.shape
    tn = 1024
    c_b = jnp.broadcast_to(c.reshape(d, 1), (d, LANES))
    out = pl.pallas_call(
        functools.partial(_adaln_kernel, tn=tn),
        grid=(depth, n // tn),
        in_specs=[pl.BlockSpec((d, LANES), lambda l, j: (0, 0)),
                  pl.BlockSpec((None, d, tn), lambda l, j: (l, 0, j)),
                  pl.BlockSpec((None, 1, tn), lambda l, j: (l, 0, j))],
        out_specs=pl.BlockSpec((None, 1, tn), lambda l, j: (l, 0, j)),
        out_shape=jax.ShapeDtypeStruct((depth, 1, n), F32),
        compiler_params=_params(dimension_semantics=("parallel", "parallel")),
        name="adaln",
    )(c_b, w_ada, b_ada.reshape(depth, 1, n))
    return out.reshape(depth, N_MOD, d)


def _in_proj_kernel(x_ref, mod_ref, w_ref, o_ref, u_ref):
    @pl.when(pl.program_id(1) == 0)
    def _():
        u_ref[...] = _bf(x_ref[...] * (1.0 + mod_ref[1:2, :]) + mod_ref[0:1, :])

    o_ref[...] = _dot(u_ref[...], w_ref[...])


def _in_proj(x, mod, w_cat, l):
    t, d = x.shape
    tm, tn = 1024, PROJ_TN
    return pl.pallas_call(
        _in_proj_kernel,
        grid=(t // tm, N_PROJ // tn),
        in_specs=[pl.BlockSpec((tm, d), lambda i, j: (i, 0)),
                  pl.BlockSpec((None, N_MOD, d), lambda i, j: (l, 0, 0)),
                  pl.BlockSpec((None, d, tn), lambda i, j: (l, 0, j))],
        out_specs=pl.BlockSpec((tm, tn), lambda i, j: (i, j)),
        out_shape=jax.ShapeDtypeStruct((t, N_PROJ), F32),
        scratch_shapes=[pltpu.VMEM((tm, d), BF16)],
        compiler_params=_params(dimension_semantics=("parallel", "arbitrary")),
        name="in_proj",
    )(x, mod, w_cat)


def _rope(t, cos, sin, lo_half):
    rot = jnp.where(lo_half, pltpu.roll(t, LANES - HALF, axis=1), pltpu.roll(t, HALF, axis=1))
    return t * cos + rot * sin


def _attn_kernel(sink_ref, q_ref, kc_ref, kp_ref, vc_ref, vp_ref, cosc_ref, sinc_ref, cosp_ref, sinp_ref,
                 o_ref, *, l):
    blk = pl.program_id(0)
    w = WINDOW
    cos_c, sin_c = cosc_ref[...], sinc_ref[...]
    cos_p, sin_p = cosp_ref[...], sinp_ref[...]

    lane = _iota((w, LANES), 1)
    lo_half = (lane % HEAD_DIM) < HALF
    in_a = lane < HEAD_DIM
    lane2 = _iota((2 * w, LANES), 1)
    in_a2 = lane2 < HEAD_DIM

    qa = _iota((2 * w, 2 * w), 0) % w
    ks = _iota((2 * w, 2 * w), 1)
    valid = (ks > qa) & (ks <= qa + w) & ((ks >= w) | (blk > 0))
    row_is_a = _iota((2 * w, 1), 0) < w
    n_tiles = ATTN_HEADS // 2
    tiles = range(n_tiles)

    k_dup, v_dup = [], []
    for m in range(KV_HEADS // 2):
        sl_kv = slice(m * LANES, (m + 1) * LANES)
        k2 = jnp.concatenate([_rope(kp_ref[:, sl_kv], cos_p, sin_p, lo_half),
                              _rope(kc_ref[:, sl_kv], cos_c, sin_c, lo_half)], axis=0)
        v2 = jnp.concatenate([vp_ref[:, sl_kv], vc_ref[:, sl_kv]], axis=0)
        k2r, v2r = pltpu.roll(k2, HEAD_DIM, axis=1), pltpu.roll(v2, HEAD_DIM, axis=1)
        k_dup += [_bf(jnp.where(in_a2, k2, k2r)), _bf(jnp.where(in_a2, k2r, k2))]
        v_dup += [_bf(jnp.where(in_a2, v2, v2r)), _bf(jnp.where(in_a2, v2r, v2))]

    s = []
    for t in tiles:
        q = _rope(q_ref[:, t * LANES:(t + 1) * LANES], cos_c, sin_c, lo_half) * (HEAD_DIM ** -0.5)
        q_st = _bf(jnp.concatenate([jnp.where(in_a, q, 0.0), jnp.where(in_a, 0.0, q)], axis=0))
        s.append(jnp.where(valid, _dot(q_st, k_dup[t // 2], NT), -1e30))
    sink = [jnp.where(row_is_a, sink_ref[l * ATTN_HEADS + 2 * t], sink_ref[l * ATTN_HEADS + 2 * t + 1])
            for t in tiles]
    mx = [jnp.maximum(jnp.max(s[t], axis=-1, keepdims=True), sink[t]) for t in tiles]
    p = [jnp.exp(s[t] - mx[t]) for t in tiles]
    den = [jnp.sum(p[t], axis=-1, keepdims=True) + jnp.exp(sink[t] - mx[t]) for t in tiles]
    for t in tiles:
        o_st = _dot(_bf(p[t]), v_dup[t // 2]) * (1.0 / den[t])
        o_ref[:, t * LANES:(t + 1) * LANES] = _bf(jnp.where(in_a, o_st[:w], o_st[w:]))


def _attention(proj, sinks, cos_t, sin_t, l):
    t = proj.shape[0]
    w = WINDOW
    prev = lambda i, s: (jnp.maximum(i - 1, 0), 0)
    cur = lambda i, s: (i, 0)
    return pl.pallas_call(
        functools.partial(_attn_kernel, l=l),
        grid_spec=pltpu.PrefetchScalarGridSpec(
            num_scalar_prefetch=1,
            grid=(t // w,),
            in_specs=[pl.BlockSpec((w, ATTN_WIDTH), lambda i, s: (i, COL_Q // ATTN_WIDTH)),
                      pl.BlockSpec((w, KV_WIDTH), lambda i, s: (i, COL_AK // KV_WIDTH)),
                      pl.BlockSpec((w, KV_WIDTH), lambda i, s: (jnp.maximum(i - 1, 0), COL_AK // KV_WIDTH)),
                      pl.BlockSpec((w, KV_WIDTH), lambda i, s: (i, COL_AV // KV_WIDTH)),
                      pl.BlockSpec((w, KV_WIDTH), lambda i, s: (jnp.maximum(i - 1, 0), COL_AV // KV_WIDTH)),
                      pl.BlockSpec((w, LANES), cur), pl.BlockSpec((w, LANES), cur),
                      pl.BlockSpec((w, LANES), prev), pl.BlockSpec((w, LANES), prev)],
            out_specs=pl.BlockSpec((w, ATTN_WIDTH), lambda i, s: (i, 0)),
        ),
        out_shape=jax.ShapeDtypeStruct((t, ATTN_WIDTH), BF16),
        compiler_params=_params(dimension_semantics=("parallel",)),
        name="attention",
    )(sinks, proj, proj, proj, proj, proj, cos_t, sin_t, cos_t, sin_t)


def _prep_kernel(*refs, has_vres):
    if has_vres:
        (r_ref, k_ref, v_ref, l_ref, rp_ref, kp_ref, vp_ref, lp_ref, mu_ref, mul_ref, par_ref,
         w2_ref, a2_ref, g2_ref, ones_ref, vf_ref, v1_ref, v2_ref,
         ro_ref, lw_ref, ko_ref, vo_ref, kk_ref, b_ref, g_ref, bon_ref) = refs
    else:
        (r_ref, k_ref, v_ref, l_ref, rp_ref, kp_ref, vp_ref, lp_ref, mu_ref, mul_ref, par_ref,
         w2_ref, a2_ref, g2_ref, ones_ref,
         ro_ref, lw_ref, ko_ref, vo_ref, kk_ref, b_ref, g_ref, bon_ref) = refs
    first = pl.program_id(0) == 0

    def shifted_lerp(x_ref, p_ref, mu):
        x = x_ref[...]
        prev_row = jnp.where(first, 0.0, p_ref[7:8, :])
        row = _iota(x.shape, 0)
        xp = jnp.where(row == 0, prev_row, pltpu.roll(x, 1, axis=0))
        return x + (xp - x) * mu

    r = shifted_lerp(r_ref, rp_ref, mu_ref[0:1, :])
    k = shifted_lerp(k_ref, kp_ref, mu_ref[1:2, :])
    v = shifted_lerp(v_ref, vp_ref, mu_ref[2:3, :])
    lo = shifted_lerp(l_ref, lp_ref, mul_ref[...])

    w0, a0 = par_ref[0:1, :], par_ref[1:2, :]
    k_k, k_a, r_k = par_ref[2:3, :], par_ref[3:4, :], par_ref[4:5, :]

    wa = lo[:, 0:LANES]
    gd = lo[:, LANES:3 * LANES]
    w_pre = w0 + _mm(jnp.tanh(wa), w2_ref[...])
    w = -_softplus(-w_pre) - 0.5
    a = jax.nn.sigmoid(a0 + _mm(wa, a2_ref[...]))
    g = _mm(jax.nn.sigmoid(gd), g2_ref[...])
    if has_vres:
        v0 = par_ref[5:6, :]
        mix = jax.nn.sigmoid(v0 + _mm(_mm(v, v1_ref[...]), v2_ref[...]))
        v = v + (vf_ref[...].astype(F32) - v) * mix

    kk = k * k_k
    k_mod = k * (1.0 + (a - 1.0) * k_a)
    rk = r * k_mod * r_k
    ones = ones_ref[...]
    for j in range(RWKV_WIDTH // SLAB):
        sl = slice(j * SLAB, (j + 1) * SLAB)
        kk_j = kk[:, sl]
        ss = _mm(kk_j * kk_j, ones)
        kk_n = kk_j * lax.rsqrt(jnp.maximum(ss, 1e-24))
        kk_ref[:, sl] = _bf(kk_n)
        b_ref[:, sl] = _bf(kk_n * a[:, sl])
        bon_ref[:, sl] = _bf(_mm(rk[:, sl], ones) * v[:, sl])
    ro_ref[...] = _bf(r)
    lw_ref[...] = -jnp.exp(w)
    ko_ref[...] = _bf(k_mod)
    vo_ref[...] = _bf(v)
    g_ref[...] = _bf(g)


def _rwkv_prep(proj, mu3, mu_lora, params, w2p, a2p, g2p, ones, vres, l):
    t = proj.shape[0]
    tm = 256
    wdt = RWKV_WIDTH
    has_vres = vres is not None

    def col(cb, width):
        return pl.BlockSpec((tm, width), lambda i: (i, cb))

    def prev(cb, width):
        return pl.BlockSpec((8, width), lambda i: (jnp.maximum(i * (tm // 8) - 1, 0), cb))

    def layer(a, lyr=l):
        return pl.BlockSpec((None,) + a.shape[1:], lambda i: (lyr, 0, 0))

    in_specs = [col(COL_R // wdt, wdt), col(COL_K // wdt, wdt), col(COL_V // wdt, wdt),
                col(COL_LORA // LORA_PAD, LORA_PAD),
                prev(COL_R // wdt, wdt), prev(COL_K // wdt, wdt), prev(COL_V // wdt, wdt),
                prev(COL_LORA // LORA_PAD, LORA_PAD),
                layer(mu3), layer(mu_lora), layer(params), layer(w2p), layer(a2p), layer(g2p),
                pl.BlockSpec(ones.shape, lambda i: (0, 0))]
    args = [proj] * 8 + [mu3, mu_lora, params, w2p, a2p, g2p, ones]
    if has_vres:
        v_first, v1p, v2p = vres
        in_specs += [pl.BlockSpec((tm, wdt), lambda i: (i, 0)), layer(v1p, l - 1), layer(v2p, l - 1)]
        args += [v_first, v1p, v2p]
    out_spec = pl.BlockSpec((tm, wdt), lambda i: (i, 0))
    out_dtypes = [BF16, F32, BF16, BF16, BF16, BF16, BF16, BF16]
    return pl.pallas_call(
        functools.partial(_prep_kernel, has_vres=has_vres),
        grid=(t // tm,),
        in_specs=in_specs,
        out_specs=[out_spec] * 8,
        out_shape=[jax.ShapeDtypeStruct((t, wdt), dt) for dt in out_dtypes],
        compiler_params=_params(dimension_semantics=("parallel",)),
        name="rwkv_prep",
    )(*args)


def _wkv_kernel(r_ref, lw_ref, k_ref, v_ref, kk_ref, b_ref, o_ref, st_ref):
    c = CHUNK

    @pl.when(pl.program_id(0) == 0)
    def _():
        st_ref[...] = jnp.zeros_like(st_ref)

    row_t = _iota((c, c), 0)
    col_t = _iota((c, c), 1)
    l_incl = _bf(jnp.where(row_t >= col_t, 1.0, 0.0))

    lane = _iota((c, LANES), 1)
    m_a = lane < HEAD_DIM
    r2 = _iota((2 * c, 2 * c), 0)
    c2 = _iota((2 * c, 2 * c), 1)
    strict = r2 > c2
    incl = r2 >= c2
    eye = r2 == c2
    anti = (r2 < c) != (c2 < c)

    def stack(x):
        return jnp.concatenate([jnp.where(m_a, x, 0.0), jnp.where(m_a, 0.0, x)], axis=0)

    def stack_sw(x):
        return jnp.concatenate([jnp.where(m_a, 0.0, x), jnp.where(m_a, x, 0.0)], axis=0)

    tiles = range(N_TILES)
    sls = [slice(j * LANES, (j + 1) * LANES) for j in tiles]

    units = [(s, j) for s in range(WKV_SUB) for j in tiles]
    un = range(len(units))

    a_st, r_st, rhs, v_sw, v_st, bk_tail, w_c = [], [], [], [], [], [], []
    for s in range(WKV_SUB):
        rows = slice(s * c, (s + 1) * c)
        cum_all = _cumsum_rows(l_incl, lw_ref[rows, :])
        for j in tiles:
            sl = sls[j]
            lw = lw_ref[rows, sl]
            cum = cum_all[:, sl]
            cum_last = cum[c - 1:c, :]
            e_neg = jnp.exp(-cum)
            e_tail = jnp.exp(cum_last - cum)
            k = k_ref[rows, sl].astype(F32)
            b = b_ref[rows, sl].astype(F32)
            a_st.append(stack(-kk_ref[rows, sl].astype(F32) * jnp.exp(cum - lw)))
            r_st.append(_bf(stack(r_ref[rows, sl].astype(F32) * jnp.exp(cum))))
            rhs.append(_bf(jnp.concatenate([stack(b * e_neg), stack(k * e_neg)], axis=0)))
            vs = pltpu.roll(v_ref[rows, sl].astype(F32), 64, axis=1)
            v_sw.append(_bf(vs))
            v_st.append(_bf(stack_sw(vs)))
            bk_tail.append(_bf(jnp.concatenate([b * e_tail, k * e_tail], axis=0)))
            w_c.append(jnp.exp(cum_last))

    incl2 = jnp.concatenate([incl, incl], axis=1)
    d_ab, d_ak, d_r = [], [], []
    for u in un:
        gm = _dot(jnp.concatenate([_bf(a_st[u]), r_st[u]], axis=0), rhs[u], NT)
        d_ab.append(gm[:2 * c, :2 * c])
        d_ak.append(_bf(jnp.where(strict, gm[:2 * c, 2 * c:], 0.0)))
        d_r.append(_bf(jnp.where(incl2, gm[2 * c:, :], 0.0)))

    av = _pair_dots(d_ak, v_st)
    y0 = [a_st[u] + av[u] for u in un]
    blk = lambda s: (r2 // s) == (c2 // s)
    diag8 = strict & blk(8)
    d1 = [jnp.where(diag8, d_ab[u], 0.0) for u in un]
    p1 = [jnp.where(eye, 1.0, d1[u]) for u in un]
    d2 = _pair_dots(d1, d1)
    res = [_mm(d2[u], jnp.concatenate([p1[u], d2[u]], axis=1)) for u in un]
    p2 = [p1[u] + res[u][:, :2 * c] for u in un]
    d4p2 = _pair_dots([res[u][:, 2 * c:] for u in un], p2)
    tm = [p2[u] + d4p2[u] for u in un]
    for s in (8, 16, 32):
        off = strict & blk(2 * s) & jnp.logical_not(blk(s))
        xm = _pair_dots(tm, [jnp.where(off, d_ab[u], 0.0) for u in un])
        xt = _pair_dots(xm, tm)
        tm = [tm[u] + xt[u] for u in un]
    y = _pair_dots(tm, y0)

    st = [st_ref[j] for j in tiles]
    for s in range(WKV_SUB):
        rows = slice(s * c, (s + 1) * c)
        base = s * N_TILES
        ra, u_loc = [], []
        for j in tiles:
            yj = y[base + j]
            ah_st = jnp.concatenate([jnp.where(m_a, yj[:c], 0.0), jnp.where(m_a, 0.0, yj[c:])], axis=0)
            u_loc.append(jnp.concatenate([jnp.where(m_a, 0.0, yj[:c]), jnp.where(m_a, yj[c:], 0.0)], axis=0))
            ra.append(jnp.concatenate([r_st[base + j], _bf(ah_st)], axis=0))
        rs_as = _pair_dots(ra, st)
        u_st = [rs_as[j][2 * c:] + u_loc[j] for j in tiles]
        for j in tiles:
            o_st = rs_as[j][:2 * c] + _dot(d_r[base + j], jnp.concatenate([_bf(u_st[j]), v_st[base + j]], axis=0))
            o_ref[rows, sls[j]] = pltpu.roll(o_st[:c] + o_st[c:], 64, axis=1)
        new_st = []
        for j in tiles:
            u_sw = u_st[j][:c] + u_st[j][c:]
            delta = _dot(bk_tail[base + j], jnp.concatenate([_bf(u_sw), v_sw[base + j]], axis=0), TN)
            w_col = jnp.sum(jnp.where(eye, jnp.broadcast_to(w_c[base + j], (2 * c, 2 * c)), 0.0),
                            axis=1, keepdims=True)
            new_st.append(w_col * st[j] + jnp.where(anti, delta, 0.0))
        st = new_st
    for j in tiles:
        st_ref[j] = st[j]


def _wkv(r, lw, k, v, kk, b):
    t, wdt = r.shape
    rows = WKV_SUB * CHUNK
    spec = pl.BlockSpec((rows, wdt), lambda i: (i, 0))
    return pl.pallas_call(
        _wkv_kernel,
        grid=(t // rows,),
        in_specs=[spec] * 6,
        out_specs=spec,
        out_shape=jax.ShapeDtypeStruct((t, wdt), F32),
        scratch_shapes=[pltpu.VMEM((N_TILES, LANES, LANES), F32)],
        compiler_params=_params(dimension_semantics=("arbitrary",)),
        name="wkv",
    )(r, lw, k, v, kk, b)


def _out_proj_kernel(ya_ref, o_ref, bon_ref, g_ref, gn_ref, ones_ref, w_ref, x_ref, mod_ref, ln_ref, out_ref):
    ones = ones_ref[...]
    ys = []
    for j in range(RWKV_WIDTH // SLAB):
        sl = slice(j * SLAB, (j + 1) * SLAB)
        o = o_ref[:, sl]
        mu = _head_sum(o, ones) * (1.0 / HEAD_DIM)
        oc = o - mu
        var = _mm(oc * oc, ones) * (1.0 / HEAD_DIM)
        y = ((oc * lax.rsqrt(var + RWKV_GN_EPS) * gn_ref[0:1, sl] + gn_ref[1:2, sl] + bon_ref[:, sl].astype(F32))
             * g_ref[:, sl].astype(F32))
        ys.append(_bf(y))
    mix = _dot(ya_ref[...], w_ref[:ATTN_WIDTH, :]) + _dot(jnp.concatenate(ys, axis=1), w_ref[ATTN_WIDTH:, :])
    z = DEEPNORM_ALPHA * x_ref[...] + mod_ref[2:3, :] * mix
    out_ref[...] = _layer_norm(z, ln_ref[0:1, :], ln_ref[1:2, :])


def _out_proj(y_attn, o, bonus, g, gn, ones, w_cat, x, mod, ln, l):
    t, d = x.shape
    tm = 512
    wdt = RWKV_WIDTH
    row = lambda width: pl.BlockSpec((tm, width), lambda i: (i, 0))
    layer = lambda a: pl.BlockSpec((None,) + a.shape[1:], lambda i: (l, 0, 0))
    return pl.pallas_call(
        _out_proj_kernel,
        grid=(t // tm,),
        in_specs=[row(ATTN_WIDTH), row(wdt), row(wdt), row(wdt), layer(gn),
                  pl.BlockSpec(ones.shape, lambda i: (0, 0)), layer(w_cat), row(d), layer(mod), layer(ln)],
        out_specs=row(d),
        out_shape=jax.ShapeDtypeStruct((t, d), F32),
        compiler_params=_params(dimension_semantics=("parallel",)),
        name="out_proj",
    )(y_attn, o, bonus, g, gn, ones, w_cat, x, mod, ln)


def _ffn_kernel(x_ref, mod_ref, ln_ref, wg_ref, wu_ref, wd_ref, out_ref, u_ref, acc_ref):
    j = pl.program_id(1)

    @pl.when(j == 0)
    def _():
        u_ref[...] = _bf(x_ref[...] * (1.0 + mod_ref[4:5, :]) + mod_ref[3:4, :])
        acc_ref[...] = jnp.zeros_like(acc_ref)

    u = u_ref[...]
    gate = _dot(u, wg_ref[...])
    up = _dot(u, wu_ref[...])
    h = _bf(gate * jax.nn.sigmoid(gate) * up)
    acc_ref[...] += _dot(h, wd_ref[...])

    @pl.when(j == pl.num_programs(1) - 1)
    def _():
        z = DEEPNORM_ALPHA * x_ref[...] + mod_ref[5:6, :] * acc_ref[...]
        out_ref[...] = _layer_norm(z, ln_ref[2:3, :], ln_ref[3:4, :])


def _ffn(x, mod, ln, w_gu, w_dn, l):
    t, d = x.shape
    f = w_dn.shape[1]
    tm, tf = 512, 512
    nf = f // tf
    return pl.pallas_call(
        _ffn_kernel,
        grid=(t // tm, nf),
        in_specs=[pl.BlockSpec((tm, d), lambda i, j: (i, 0)),
                  pl.BlockSpec((None,) + mod.shape[1:], lambda i, j: (l, 0, 0)),
                  pl.BlockSpec((None,) + ln.shape[1:], lambda i, j: (l, 0, 0)),
                  pl.BlockSpec((None, d, tf), lambda i, j: (l, 0, j)),
                  pl.BlockSpec((None, d, tf), lambda i, j: (l, 0, j + nf)),
                  pl.BlockSpec((None, tf, d), lambda i, j: (l, j, 0))],
        out_specs=pl.BlockSpec((tm, d), lambda i, j: (i, 0)),
        out_shape=jax.ShapeDtypeStruct((t, d), F32),
        scratch_shapes=[pltpu.VMEM((tm, d), BF16), pltpu.VMEM((tm, d), F32)],
        compiler_params=_params(dimension_semantics=("parallel", "arbitrary")),
        name="ffn",
    )(x, mod, ln, w_gu, w_gu, w_dn)


def _pad_axis(a, axis, before, total):
    pads = [(0, 0)] * a.ndim
    pads[axis] = (before, total - before - a.shape[axis])
    return jnp.pad(a, pads)


def kernel(x, c, positions, w_ada, b_ada, w_in, attn_sinks, rwkv_mu, rwkv_w0, rwkv_w2, rwkv_a0, rwkv_a2, rwkv_g2, rwkv_k_k, rwkv_k_a, rwkv_r_k, rwkv_gn_w, rwkv_gn_b, rwkv_v0, rwkv_v1, rwkv_v2, w_out, ln1_w, ln1_b, w_gate_up, w_down, ln2_w, ln2_b):
    bsz, t, d = x.shape
    assert bsz == 1 and d == D_MODEL
    depth = w_in.shape[0]
    rw = RWKV_WIDTH
    inv =ROPE_THETA ** (-jnp.arange(0, HEAD_DIM, 2, dtype=F32) / HEAD_DIM)
    ang = positions[0].astype(F32)[:, None] * inv
    cos, sin = jnp.cos(ang), jnp.sin(ang)
    cos_t = jnp.concatenate([cos, cos, cos, cos], axis=-1)
    sin_t = jnp.concatenate([-sin, sin, -sin, sin], axis=-1)

    w_cat = jnp.concatenate([w_in[:, :, :ATTN_COLS], w_in[:, :, ATTN_COLS + 3 * rw:],
                             jnp.zeros((depth, d, COL_R - COL_LORA - LORA_COLS), F32),
                             w_in[:, :, ATTN_COLS:ATTN_COLS + 3 * rw]], axis=2).astype(BF16)
    sinks = attn_sinks.reshape(-1)
    mu3 = rwkv_mu[:, :3 * rw].reshape(depth, 3, rw)
    mu_lora = _pad_axis(rwkv_mu[:, 3 * rw:], 1, 0, LORA_PAD).reshape(depth, 1, LORA_PAD)
    zero = jnp.zeros((depth, rw), F32)
    v0_all = _pad_axis(rwkv_v0, 0, 1, depth)
    params = jnp.stack([rwkv_w0, rwkv_a0, rwkv_k_k, rwkv_k_a, rwkv_r_k.reshape(depth, rw), v0_all, zero, zero],
                       axis=1)
    w2p = _pad_axis(rwkv_w2, 1, 0, LANES)
    a2p = _pad_axis(rwkv_a2, 1, DECAY_LORA, LANES)
    g2p = _pad_axis(rwkv_g2, 1, 0, 2 * LANES)
    v1p = _pad_axis(rwkv_v1, 2, 0, LANES)
    v2p = _pad_axis(rwkv_v2, 1, 0, LANES)
    seg = np.arange(SLAB) // HEAD_DIM
    ones = jnp.asarray(seg[:, None] == seg[None, :], BF16)
    gn = jnp.stack([rwkv_gn_w, rwkv_gn_b], axis=1)
    w_out_bf = w_out.astype(BF16)
    ln = jnp.stack([ln1_w, ln1_b, ln2_w, ln2_b], axis=1)
    w_gu = w_gate_up.astype(BF16)
    w_dn = w_down.astype(BF16)

    mod = _adaln(c, w_ada, b_ada)
    xs = x[0]
    v_first = None
    for l in range(depth):
        proj = _in_proj(xs, mod, w_cat, l)
        y_attn = _attention(proj, sinks, cos_t, sin_t, l)
        vres = None if l == 0 else (v_first, v1p, v2p)
        r, lw, k_mod, v, kk, b, g, bonus = _rwkv_prep(proj, mu3, mu_lora, params, w2p, a2p, g2p, ones, vres, l)
        if l == 0:
            v_first = v
        o = _wkv(r, lw, k_mod, v, kk, b)
        xs = _out_proj(y_attn, o, bonus, g, gn, ones, w_out_bf, xs, mod, ln, l)
        xs = _ffn(xs, mod, ln, w_gu, w_dn, l)
    return xs[None]
```

```python
import functools

import numpy as np
import jax
import jax.numpy as jnp
from jax import lax
from jax.experimental import pallas as pl
from jax.experimental.pallas import tpu as pltpu

F32 = jnp.float32
BF16 = jnp.bfloat16

D_MODEL = 2048
DEPTH = 2
HEAD_DIM = 64
HALF = HEAD_DIM // 2
ATTN_WIDTH = 1024
ATTN_HEADS = 16
KV_HEADS = 4
KV_WIDTH = KV_HEADS * HEAD_DIM
ATTN_COLS = ATTN_WIDTH + 2 * KV_WIDTH
WINDOW = 128
ROPE_THETA = 10000.0
RWKV_WIDTH = 1024
DECAY_LORA = 64
AAA_LORA = 64
GATE_LORA = 160
LORA_COLS = DECAY_LORA + AAA_LORA + GATE_LORA
MV_LORA = 32
RWKV_GN_EPS = 64e-5
FFN_HIDDEN = 5632
LN_EPS = 1e-5
N_MOD = 6
DEEPNORM_ALPHA = (2.0 * DEPTH) ** 0.25

LANES = 128
N_TILES = RWKV_WIDTH // LANES
CHUNK = 64
WKV_SUB = 2
VMEM_LIMIT = 56 * 1024 * 1024

COL_Q, COL_AK, COL_AV, COL_LORA = 0, 1024, 1280, 1536
COL_R, COL_K, COL_V = 2048, 3072, 4096
N_PROJ = 5120
LORA_PAD = 512
SLAB = 256
PROJ_TN = 512

NN = ((1,), (0,))
NT = ((1,), (1,))
TN = ((0,), (0,))


def _dot(a, b, dims=NN):
    return lax.dot_general(a, b, (dims, ((), ())), preferred_element_type=F32)


def _bf(z):
    return z.astype(BF16)


def _mm(a, b):
    return _dot(_bf(a), _bf(b))


def _cumsum_rows(tri_bf16, b):
    b0 = _bf(b)
    r1 = b - b0.astype(F32)
    b1 = _bf(r1)
    b2 = _bf(r1 - b1.astype(F32))
    return _dot(tri_bf16, b0) + (_dot(tri_bf16, b1) + _dot(tri_bf16, b2))


def _head_sum(a, ones_bf16):
    a0 = _bf(a)
    a1 = _bf(a - a0.astype(F32))
    return _dot(a0, ones_bf16) + _dot(a1, ones_bf16)


def _pair_dots(lhs, rhs):
    out = []
    for p in range(0, len(lhs), 2):
        r0, r1 = _bf(rhs[p]), _bf(rhs[p + 1])
        z = jnp.zeros_like(r0)
        bd = jnp.concatenate([jnp.concatenate([r0, z], axis=1), jnp.concatenate([z, r1], axis=1)], axis=0)
        res = _dot(jnp.concatenate([_bf(lhs[p]), _bf(lhs[p + 1])], axis=1), bd)
        n = r0.shape[1]
        out += [res[:, :n], res[:, n:]]
    return out


def _iota(shape, dim):
    return lax.broadcasted_iota(jnp.int32, shape, dim)


def _softplus(z):
    return jnp.maximum(z, 0.0) + jnp.log(1.0 + jnp.exp(-jnp.abs(z)))


def _layer_norm(z, w, b):
    mu = jnp.mean(z, axis=-1, keepdims=True)
    zc = z - mu
    var = jnp.mean(zc * zc, axis=-1, keepdims=True)
    return zc * lax.rsqrt(var + LN_EPS) * w + b


def _params(**kw):
    return pltpu.CompilerParams(vmem_limit_bytes=VMEM_LIMIT, **kw)


def _adaln_kernel(c_ref, w_ref, b_ref, o_ref, *, tn):
    cb = c_ref[...]
    cond = cb * jax.nn.sigmoid(cb)
    for j in range(tn // LANES):
        sl = slice(j * LANES, (j + 1) * LANES)
        o_ref[:, sl] = jnp.sum(w_ref[:, sl] * cond, axis=0, keepdims=True) + b_ref[:, sl]


def _adaln(c, w_ada, b_ada):
    depth, d, n = w_ada.shape
    tn = 1024
    c_b = jnp.broadcast_to(c.reshape(d, 1), (d, LANES))
    out = pl.pallas_call(
        functools.partial(_adaln_kernel, tn=tn),
        grid=(depth, n // tn),
        in_specs=[pl.BlockSpec((d, LANES), lambda l, j: (0, 0)),
                  pl.BlockSpec((None, d, tn), lambda l, j: (l, 0, j)),
                  pl.BlockSpec((None, 1, tn), lambda l, j: (l, 0, j))],
        out_specs=pl.BlockSpec((None, 1, tn), lambda l, j: (l, 0, j)),
        out_shape=jax.ShapeDtypeStruct((depth, 1, n), F32),
        compiler_params=_params(dimension_semantics=("parallel", "parallel")),
        name="adaln",
    )(c_b, w_ada, b_ada.reshape(depth, 1, n))
    return out.reshape(depth, N_MOD, d)


def _in_proj_kernel(x_ref, mod_ref, w_ref, o_ref, u_ref):
    @pl.when(pl.program_id(1) == 0)
    def _():
        u_ref[...] = _bf(x_ref[...] * (1.0 + mod_ref[1:2, :]) + mod_ref[0:1, :])

    o_ref[...] = _dot(u_ref[...], w_ref[...])


def _in_proj(x, mod, w_pad, l):
    t, d = x.shape
    tm, tn = 1024, PROJ_TN
    lora_out, lora_w = COL_LORA // tn, (ATTN_COLS + 3 * RWKV_WIDTH) // tn

    def w_block(j):
        return jnp.where(j < lora_out, j, jnp.where(j == lora_out, lora_w, j - 1))

    return pl.pallas_call(
        _in_proj_kernel,
        grid=(t // tm, N_PROJ // tn),
        in_specs=[pl.BlockSpec((tm, d), lambda i, j: (i, 0)),
                  pl.BlockSpec((None, N_MOD, d), lambda i, j: (l, 0, 0)),
                  pl.BlockSpec((None, d, tn), lambda i, j: (l, 0, w_block(j)))],
        out_specs=pl.BlockSpec((tm, tn), lambda i, j: (i, j)),
        out_shape=jax.ShapeDtypeStruct((t, N_PROJ), F32),
        scratch_shapes=[pltpu.VMEM((tm, d), BF16)],
        compiler_params=_params(dimension_semantics=("parallel", "arbitrary")),
        name="in_proj",
    )(x, mod, w_pad)


def _rope(t, cos, sin, lo_half):
    rot = jnp.where(lo_half, pltpu.roll(t, LANES - HALF, axis=1), pltpu.roll(t, HALF, axis=1))
    return t * cos + rot * sin


def _attn_kernel(sink_ref, q_ref, kc_ref, kp_ref, vc_ref, vp_ref, cosc_ref, sinc_ref, cosp_ref, sinp_ref,
                 o_ref, *, l):
    blk = pl.program_id(0)
    w = WINDOW
    cos_c, sin_c = cosc_ref[...], sinc_ref[...]
    cos_p, sin_p = cosp_ref[...], sinp_ref[...]

    lane = _iota((w, LANES), 1)
    lo_half = (lane % HEAD_DIM) < HALF
    in_a = lane < HEAD_DIM
    lane2 = _iota((2 * w, LANES), 1)
    in_a2 = lane2 < HEAD_DIM

    qa = _iota((2 * w, 2 * w), 0) % w
    ks = _iota((2 * w, 2 * w), 1)
    valid = (ks > qa) & (ks <= qa + w) & ((ks >= w) | (blk > 0))
    row_is_a = _iota((2 * w, 1), 0) < w
    n_tiles = ATTN_HEADS // 2
    tiles = range(n_tiles)

    ones_kv = jnp.ones((2 * w, LANES), BF16)
    k_dup, v_dup = [], []
    for m in range(KV_HEADS // 2):
        sl_kv = slice(m * LANES, (m + 1) * LANES)
        k2 = jnp.concatenate([_rope(kp_ref[:, sl_kv], cos_p, sin_p, lo_half),
                              _rope(kc_ref[:, sl_kv], cos_c, sin_c, lo_half)], axis=0)
        v2 = jnp.concatenate([vp_ref[:, sl_kv], vc_ref[:, sl_kv]], axis=0)
        k2r, v2r = pltpu.roll(k2, HEAD_DIM, axis=1), pltpu.roll(v2, HEAD_DIM, axis=1)
        k_dup += [_bf(jnp.where(in_a2, k2, k2r)), _bf(jnp.where(in_a2, k2r, k2))]
        v_dup += [jnp.concatenate([_bf(jnp.where(in_a2, v2, v2r)), ones_kv], axis=1),
                  jnp.concatenate([_bf(jnp.where(in_a2, v2r, v2)), ones_kv], axis=1)]

    s = []
    for t in tiles:
        q = _rope(q_ref[:, t * LANES:(t + 1) * LANES], cos_c, sin_c, lo_half) * (HEAD_DIM ** -0.5)
        q_st = _bf(jnp.concatenate([jnp.where(in_a, q, 0.0), jnp.where(in_a, 0.0, q)], axis=0))
        s.append(jnp.where(valid, _dot(q_st, k_dup[t // 2], NT), -1e30))
    sink = [jnp.where(row_is_a, sink_ref[l * ATTN_HEADS + 2 * t], sink_ref[l * ATTN_HEADS + 2 * t + 1])
            for t in tiles]
    mx = [jnp.maximum(jnp.max(s[t], axis=-1, keepdims=True), sink[t]) for t in tiles]
    p = [jnp.exp(s[t] - mx[t]) for t in tiles]
    for t in tiles:
        ov = _dot(_bf(p[t]), v_dup[t // 2])
        o_st = ov[:, :LANES] * (1.0 / (ov[:, LANES:] + jnp.exp(sink[t] - mx[t])))
        o_ref[:, t * LANES:(t + 1) * LANES] = _bf(jnp.where(in_a, o_st[:w], o_st[w:]))


def _attention(proj, sinks, cos_t, sin_t, l):
    t = proj.shape[0]
    w = WINDOW
    prev = lambda i, s: (jnp.maximum(i - 1, 0), 0)
    cur = lambda i, s: (i, 0)
    return pl.pallas_call(
        functools.partial(_attn_kernel, l=l),
        grid_spec=pltpu.PrefetchScalarGridSpec(
            num_scalar_prefetch=1,
            grid=(t // w,),
            in_specs=[pl.BlockSpec((w, ATTN_WIDTH), lambda i, s: (i, COL_Q // ATTN_WIDTH)),
                      pl.BlockSpec((w, KV_WIDTH), lambda i, s: (i, COL_AK // KV_WIDTH)),
                      pl.BlockSpec((w, KV_WIDTH), lambda i, s: (jnp.maximum(i - 1, 0), COL_AK // KV_WIDTH)),
                      pl.BlockSpec((w, KV_WIDTH), lambda i, s: (i, COL_AV // KV_WIDTH)),
                      pl.BlockSpec((w, KV_WIDTH), lambda i, s: (jnp.maximum(i - 1, 0), COL_AV // KV_WIDTH)),
                      pl.BlockSpec((w, LANES), cur), pl.BlockSpec((w, LANES), cur),
                      pl.BlockSpec((w, LANES), prev), pl.BlockSpec((w, LANES), prev)],
            out_specs=pl.BlockSpec((w, ATTN_WIDTH), lambda i, s: (i, 0)),
        ),
        out_shape=jax.ShapeDtypeStruct((t, ATTN_WIDTH), BF16),
        compiler_params=_params(dimension_semantics=("parallel",)),
        name="attention",
    )(sinks, proj, proj, proj, proj, proj, cos_t, sin_t, cos_t, sin_t)


def _prep_kernel(*refs, has_vres):
    if has_vres:
        (r_ref, k_ref, v_ref, l_ref, rp_ref, kp_ref, vp_ref, lp_ref, mu_ref, mul_ref, par_ref,
         w2_ref, a2_ref, g2_ref, ones_ref, vf_ref, v1_ref, v2_ref,
         ro_ref, lw_ref, ko_ref, vo_ref, kk_ref, b_ref, g_ref, bon_ref) = refs
    else:
        (r_ref, k_ref, v_ref, l_ref, rp_ref, kp_ref, vp_ref, lp_ref, mu_ref, mul_ref, par_ref,
         w2_ref, a2_ref, g2_ref, ones_ref,
         ro_ref, lw_ref, ko_ref, vo_ref, kk_ref, b_ref, g_ref, bon_ref) = refs
    first = pl.program_id(0) == 0

    def shifted_lerp(x_ref, p_ref, mu):
        x = x_ref[...]
        prev_row = jnp.where(first, 0.0, p_ref[7:8, :])
        row = _iota(x.shape, 0)
        xp = jnp.where(row == 0, prev_row, pltpu.roll(x, 1, axis=0))
        return x + (xp - x) * mu

    r = shifted_lerp(r_ref, rp_ref, mu_ref[0:1, :])
    k = shifted_lerp(k_ref, kp_ref, mu_ref[1:2, :])
    v = shifted_lerp(v_ref, vp_ref, mu_ref[2:3, :])
    lo = shifted_lerp(l_ref, lp_ref, mul_ref[...])

    w0, a0 = par_ref[0:1, :], par_ref[1:2, :]
    k_k, k_a, r_k = par_ref[2:3, :], par_ref[3:4, :], par_ref[4:5, :]

    wa = lo[:, 0:LANES]
    gd = lo[:, LANES:3 * LANES]
    w_pre = w0 + _mm(jnp.tanh(wa), w2_ref[...])
    w = -_softplus(-w_pre) - 0.5
    a = jax.nn.sigmoid(a0 + _mm(wa, a2_ref[...]))
    g = _mm(jax.nn.sigmoid(gd), g2_ref[...])
    if has_vres:
        v0 = par_ref[5:6, :]
        mix = jax.nn.sigmoid(v0 + _mm(_mm(v, v1_ref[...]), v2_ref[...]))
        v = v + (vf_ref[...].astype(F32) - v) * mix

    kk = k * k_k
    k_mod = k * (1.0 + (a - 1.0) * k_a)
    rk = r * k_mod * r_k
    ones = ones_ref[...]
    for j in range(RWKV_WIDTH // SLAB):
        sl = slice(j * SLAB, (j + 1) * SLAB)
        kk_j = kk[:, sl]
        ss = _mm(kk_j * kk_j, ones)
        kk_n = kk_j * lax.rsqrt(jnp.maximum(ss, 1e-24))
        kk_ref[:, sl] = _bf(kk_n)
        b_ref[:, sl] = _bf(kk_n * a[:, sl])
        bon_ref[:, sl] = _bf(_mm(rk[:, sl], ones) * v[:, sl])
    ro_ref[...] = _bf(r)
    lw_ref[...] = -jnp.exp(w)
    ko_ref[...] = _bf(k_mod)
    vo_ref[...] = _bf(v)
    g_ref[...] = _bf(g)


def _rwkv_prep(proj, mu3, mu_lora, params, w2p, a2p, g2p, ones, vres, l):
    t = proj.shape[0]
    tm = 256
    wdt = RWKV_WIDTH
    has_vres = vres is not None

    def col(cb, width):
        return pl.BlockSpec((tm, width), lambda i: (i, cb))

    def prev(cb, width):
        return pl.BlockSpec((8, width), lambda i: (jnp.maximum(i * (tm // 8) - 1, 0), cb))

    def layer(a, lyr=l):
        return pl.BlockSpec((None,) + a.shape[1:], lambda i: (lyr, 0, 0))

    in_specs = [col(COL_R // wdt, wdt), col(COL_K // wdt, wdt), col(COL_V // wdt, wdt),
                col(COL_LORA // LORA_PAD, LORA_PAD),
                prev(COL_R // wdt, wdt), prev(COL_K // wdt, wdt), prev(COL_V // wdt, wdt),
                prev(COL_LORA // LORA_PAD, LORA_PAD),
                layer(mu3), layer(mu_lora), layer(params), layer(w2p), layer(a2p), layer(g2p),
                pl.BlockSpec(ones.shape, lambda i: (0, 0))]
    args = [proj] * 8 + [mu3, mu_lora, params, w2p, a2p, g2p, ones]
    if has_vres:
        v_first, v1p, v2p = vres
        in_specs += [pl.BlockSpec((tm, wdt), lambda i: (i, 0)), layer(v1p, l - 1), layer(v2p, l - 1)]
        args += [v_first, v1p, v2p]
    out_spec = pl.BlockSpec((tm, wdt), lambda i: (i, 0))
    out_dtypes = [BF16, F32, BF16, BF16, BF16, BF16, BF16, BF16]
    return pl.pallas_call(
        functools.partial(_prep_kernel, has_vres=has_vres),
        grid=(t // tm,),
        in_specs=in_specs,
        out_specs=[out_spec] * 8,
        out_shape=[jax.ShapeDtypeStruct((t, wdt), dt) for dt in out_dtypes],
        compiler_params=_params(dimension_semantics=("parallel",)),
        name="rwkv_prep",
    )(*args)


def _wkv_kernel(r_ref, lw_ref, k_ref, v_ref, kk_ref, b_ref, o_ref, st_ref):
    c = CHUNK

    @pl.when(pl.program_id(0) == 0)
    def _():
        st_ref[...] = jnp.zeros_like(st_ref)

    row_t = _iota((c, c), 0)
    col_t = _iota((c, c), 1)
    l_incl = _bf(jnp.where(row_t >= col_t, 1.0, 0.0))

    lane = _iota((c, LANES), 1)
    m_a = lane < HEAD_DIM
    r2 = _iota((2 * c, 2 * c), 0)
    c2 = _iota((2 * c, 2 * c), 1)
    strict = r2 > c2
    incl = r2 >= c2
    eye = r2 == c2
    anti = (r2 < c) != (c2 < c)

    def stack(x):
        return jnp.concatenate([jnp.where(m_a, x, 0.0), jnp.where(m_a, 0.0, x)], axis=0)

    def stack_sw(x):
        return jnp.concatenate([jnp.where(m_a, 0.0, x), jnp.where(m_a, x, 0.0)], axis=0)

    tiles = range(N_TILES)
    sls = [slice(j * LANES, (j + 1) * LANES) for j in tiles]

    incl2 = jnp.concatenate([incl, incl], axis=1)
    blk = lambda s: (r2 // s) == (c2 // s)
    diag8 = strict & blk(8)

    def local_part(s, out):
        rows = slice(s * c, (s + 1) * c)
        cum_all = _cumsum_rows(l_incl, lw_ref[rows, :])
        a_st, r_st, rhs, v_sw, v_st, bk_tail, w_c = [], [], [], [], [], [], []
        for j in tiles:
            sl = sls[j]
            lw = lw_ref[rows, sl]
            cum = cum_all[:, sl]
            cum_last = cum[c - 1:c, :]
            e_neg = jnp.exp(-cum)
            e_tail = jnp.exp(cum_last - cum)
            k = k_ref[rows, sl].astype(F32)
            b = b_ref[rows, sl].astype(F32)
            a_st.append(stack(-kk_ref[rows, sl].astype(F32) * jnp.exp(cum - lw)))
            r_st.append(_bf(stack(r_ref[rows, sl].astype(F32) * jnp.exp(cum))))
            rhs.append(_bf(jnp.concatenate([stack(b * e_neg), stack(k * e_neg)], axis=0)))
            vs = pltpu.roll(v_ref[rows, sl].astype(F32), 64, axis=1)
            v_sw.append(_bf(vs))
            v_st.append(_bf(stack_sw(vs)))
            bk_tail.append(_bf(jnp.concatenate([b * e_tail, k * e_tail], axis=0)))
            w_c.append(jnp.exp(cum_last))
        yield
        d_ab, d_ak, d_r = [], [], []
        for j in tiles:
            gm = _dot(jnp.concatenate([_bf(a_st[j]), r_st[j]], axis=0), rhs[j], NT)
            d_ab.append(gm[:2 * c, :2 * c])
            d_ak.append(_bf(jnp.where(strict, gm[:2 * c, 2 * c:], 0.0)))
            d_r.append(_bf(jnp.where(incl2, gm[2 * c:, :], 0.0)))
        yield
        av = _pair_dots(d_ak, v_st)
        y0 = [a_st[j] + av[j] for j in tiles]
        d1 = [jnp.where(diag8, d_ab[j], 0.0) for j in tiles]
        p1 = [jnp.where(eye, 1.0, d1[j]) for j in tiles]
        d2 = _pair_dots(d1, d1)
        yield
        res = [_mm(d2[j], jnp.concatenate([p1[j], d2[j]], axis=1)) for j in tiles]
        p2 = [p1[j] + res[j][:, :2 * c] for j in tiles]
        yield
        d4p2 = _pair_dots([res[j][:, 2 * c:] for j in tiles], p2)
        tm = [p2[j] + d4p2[j] for j in tiles]
        for bs in (8, 16, 32):
            yield
            off = strict & blk(2 * bs) & jnp.logical_not(blk(bs))
            xm = _pair_dots(tm, [jnp.where(off, d_ab[j], 0.0) for j in tiles])
            yield
            xt = _pair_dots(xm, tm)
            tm = [tm[j] + xt[j] for j in tiles]
        yield
        out.update(y=_pair_dots(tm, y0), r_st=r_st, d_r=d_r, v_st=v_st, v_sw=v_sw, bk_tail=bk_tail, w_c=w_c)

    def state_part(s, loc, st):
        rows = slice(s * c, (s + 1) * c)
        ra, u_loc = [], []
        for j in tiles:
            yj = loc["y"][j]
            ah_st = jnp.concatenate([jnp.where(m_a, yj[:c], 0.0), jnp.where(m_a, 0.0, yj[c:])], axis=0)
            u_loc.append(jnp.concatenate([jnp.where(m_a, 0.0, yj[:c]), jnp.where(m_a, yj[c:], 0.0)], axis=0))
            ra.append(jnp.concatenate([loc["r_st"][j], _bf(ah_st)], axis=0))
        rs_as = _pair_dots(ra, st)
        u_st = [rs_as[j][2 * c:] + u_loc[j] for j in tiles]
        for j in tiles:
            o_st = rs_as[j][:2 * c] + _dot(loc["d_r"][j], jnp.concatenate([_bf(u_st[j]), loc["v_st"][j]], axis=0))
            o_ref[rows, sls[j]] = pltpu.roll(o_st[:c] + o_st[c:], 64, axis=1)
        new_st = []
        for j in tiles:
            u_sw = u_st[j][:c] + u_st[j][c:]
            delta = _dot(loc["bk_tail"][j], jnp.concatenate([_bf(u_sw), loc["v_sw"][j]], axis=0), TN)
            w_col = jnp.sum(jnp.where(eye, jnp.broadcast_to(loc["w_c"][j], (2 * c, 2 * c)), 0.0),
                            axis=1, keepdims=True)
            new_st.append(w_col * st[j] + jnp.where(anti, delta, 0.0))
        return new_st

    locs = [dict() for _ in range(WKV_SUB)]
    live = [local_part(s, locs[s]) for s in range(WKV_SUB)]
    while live:
        live = [g for g in live if next(g, "done") != "done"]
    st = [st_ref[j] for j in tiles]
    for s in range(WKV_SUB):
        st = state_part(s, locs[s], st)
    for j in tiles:
        st_ref[j] = st[j]


def _wkv(r, lw, k, v, kk, b):
    t, wdt = r.shape
    rows = WKV_SUB * CHUNK
    spec = pl.BlockSpec((rows, wdt), lambda i: (i, 0))
    return pl.pallas_call(
        _wkv_kernel,
        grid=(t // rows,),
        in_specs=[spec] * 6,
        out_specs=spec,
        out_shape=jax.ShapeDtypeStruct((t, wdt), F32),
        scratch_shapes=[pltpu.VMEM((N_TILES, LANES, LANES), F32)],
        compiler_params=_params(dimension_semantics=("arbitrary",)),
        name="wkv",
    )(r, lw, k, v, kk, b)


def _out_proj_kernel(ya_ref, o_ref, bon_ref, g_ref, gn_ref, ones_ref, w_ref, x_ref, mod_ref, ln_ref, out_ref):
    ones = ones_ref[...]
    ys = []
    for j in range(RWKV_WIDTH // SLAB):
        sl = slice(j * SLAB, (j + 1) * SLAB)
        o = o_ref[:, sl]
        mu = _head_sum(o, ones) * (1.0 / HEAD_DIM)
        oc = o - mu
        var = _mm(oc * oc, ones) * (1.0 / HEAD_DIM)
        y = ((oc * lax.rsqrt(var + RWKV_GN_EPS) * gn_ref[0:1, sl] + gn_ref[1:2, sl] + bon_ref[:, sl].astype(F32))
             * g_ref[:, sl].astype(F32))
        ys.append(_bf(y))
    mix = _dot(ya_ref[...], w_ref[:ATTN_WIDTH, :]) + _dot(jnp.concatenate(ys, axis=1), w_ref[ATTN_WIDTH:, :])
    z = DEEPNORM_ALPHA * x_ref[...] + mod_ref[2:3, :] * mix
    out_ref[...] = _layer_norm(z, ln_ref[0:1, :], ln_ref[1:2, :])


def _out_proj(y_attn, o, bonus, g, gn, ones, w_cat, x, mod, ln, l):
    t, d = x.shape
    tm = 512
    wdt = RWKV_WIDTH
    row = lambda width: pl.BlockSpec((tm, width), lambda i: (i, 0))
    layer = lambda a: pl.BlockSpec((None,) + a.shape[1:], lambda i: (l, 0, 0))
    return pl.pallas_call(
        _out_proj_kernel,
        grid=(t // tm,),
        in_specs=[row(ATTN_WIDTH), row(wdt), row(wdt), row(wdt), layer(gn),
                  pl.BlockSpec(ones.shape, lambda i: (0, 0)), layer(w_cat), row(d), layer(mod), layer(ln)],
        out_specs=row(d),
        out_shape=jax.ShapeDtypeStruct((t, d), F32),
        compiler_params=_params(dimension_semantics=("parallel",)),
        name="out_proj",
    )(y_attn, o, bonus, g, gn, ones, w_cat, x, mod, ln)


def _ffn_kernel(x_ref, mod_ref, ln_ref, wg_ref, wu_ref, wd_ref, out_ref, u_ref, acc_ref):
    j = pl.program_id(1)

    @pl.when(j == 0)
    def _():
        u_ref[...] = _bf(x_ref[...] * (1.0 + mod_ref[4:5, :]) + mod_ref[3:4, :])
        acc_ref[...] = jnp.zeros_like(acc_ref)

    u = u_ref[...]
    gate = _dot(u, wg_ref[...])
    up = _dot(u, wu_ref[...])
    h = _bf(gate * jax.nn.sigmoid(gate) * up)
    acc_ref[...] += _dot(h, wd_ref[...])

    @pl.when(j == pl.num_programs(1) - 1)
    def _():
        z = DEEPNORM_ALPHA * x_ref[...] + mod_ref[5:6, :] * acc_ref[...]
        out_ref[...] = _layer_norm(z, ln_ref[2:3, :], ln_ref[3:4, :])


def _ffn(x, mod, ln, w_gu, w_dn, l):
    t, d = x.shape
    f = w_dn.shape[1]
    tm, tf = 512, 512
    nf = f // tf
    return pl.pallas_call(
        _ffn_kernel,
        grid=(t // tm, nf),
        in_specs=[pl.BlockSpec((tm, d), lambda i, j: (i, 0)),
                  pl.BlockSpec((None,) + mod.shape[1:], lambda i, j: (l, 0, 0)),
                  pl.BlockSpec((None,) + ln.shape[1:], lambda i, j: (l, 0, 0)),
                  pl.BlockSpec((None, d, tf), lambda i, j: (l, 0, j)),
                  pl.BlockSpec((None, d, tf), lambda i, j: (l, 0, j + nf)),
                  pl.BlockSpec((None, tf, d), lambda i, j: (l, j, 0))],
        out_specs=pl.BlockSpec((tm, d), lambda i, j: (i, 0)),
        out_shape=jax.ShapeDtypeStruct((t, d), F32),
        scratch_shapes=[pltpu.VMEM((tm, d), BF16), pltpu.VMEM((tm, d), F32)],
        compiler_params=_params(dimension_semantics=("parallel", "arbitrary")),
        name="ffn",
    )(x, mod, ln, w_gu, w_gu, w_dn)


def _pad_axis(a, axis, before, total):
    pads = [(0, 0)] * a.ndim
    pads[axis] = (before, total - before - a.shape[axis])
    return jnp.pad(a, pads)


def kernel(x, c, positions, w_ada, b_ada, w_in, attn_sinks, rwkv_mu, rwkv_w0, rwkv_w2, rwkv_a0, rwkv_a2, rwkv_g2, rwkv_k_k, rwkv_k_a, rwkv_r_k, rwkv_gn_w, rwkv_gn_b, rwkv_v0, rwkv_v1, rwkv_v2, w_out, ln1_w, ln1_b, w_gate_up, w_down, ln2_w, ln2_b):
    bsz, t, d = x.shape
    assert bsz == 1 and d == D_MODEL
    depth = w_in.shape[0]
    rw = RWKV_WIDTH
    inv =ROPE_THETA ** (-jnp.arange(0, HEAD_DIM, 2, dtype=F32) / HEAD_DIM)
    ang = positions[0].astype(F32)[:, None] * inv
    cos, sin = jnp.cos(ang), jnp.sin(ang)
    cos_t = jnp.concatenate([cos, cos, cos, cos], axis=-1)
    sin_t = jnp.concatenate([-sin, sin, -sin, sin], axis=-1)

    w_pad = _pad_axis(w_in.astype(BF16), 2, 0, N_PROJ)
    sinks = attn_sinks.reshape(-1)
    mu3 = rwkv_mu[:, :3 * rw].reshape(depth, 3, rw)
    mu_lora = _pad_axis(rwkv_mu[:, 3 * rw:], 1, 0, LORA_PAD).reshape(depth, 1, LORA_PAD)
    zero = jnp.zeros((depth, rw), F32)
    v0_all = _pad_axis(rwkv_v0, 0, 1, depth)
    params = jnp.stack([rwkv_w0, rwkv_a0, rwkv_k_k, rwkv_k_a, rwkv_r_k.reshape(depth, rw), v0_all, zero, zero],
                       axis=1)
    w2p = _pad_axis(rwkv_w2, 1, 0, LANES)
    a2p = _pad_axis(rwkv_a2, 1, DECAY_LORA, LANES)
    g2p = _pad_axis(rwkv_g2, 1, 0, 2 * LANES)
    v1p = _pad_axis(rwkv_v1, 2, 0, LANES)
    v2p = _pad_axis(rwkv_v2, 1, 0, LANES)
    seg = np.arange(SLAB) // HEAD_DIM
    ones = jnp.asarray(seg[:, None] == seg[None, :], BF16)
    gn = jnp.stack([rwkv_gn_w, rwkv_gn_b], axis=1)
    w_out_bf = w_out.astype(BF16)
    ln = jnp.stack([ln1_w, ln1_b, ln2_w, ln2_b], axis=1)
    w_gu = w_gate_up.astype(BF16)
    w_dn = w_down.astype(BF16)

    mod = _adaln(c, w_ada, b_ada)
    xs = x[0]
    v_first = None
    for l in range(depth):
        proj = _in_proj(xs, mod, w_pad, l)
        y_attn = _attention(proj, sinks, cos_t, sin_t, l)
        vres = None if l == 0 else (v_first, v1p, v2p)
        r, lw, k_mod, v, kk, b, g, bonus = _rwkv_prep(proj, mu3, mu_lora, params, w2p, a2p, g2p, ones, vres, l)
        if l == 0:
            v_first = v
        o = _wkv(r, lw, k_mod, v, kk, b)
        xs = _out_proj(y_attn, o, bonus, g, gn, ones, w_out_bf, xs, mod, ln, l)
        xs = _ffn(xs, mod, ln, w_gu, w_dn, l)
    return xs[None]
```

```python
import functools

import numpy as np
import jax
import jax.numpy as jnp
from jax import lax
from jax.experimental import pallas as pl
from jax.experimental.pallas import tpu as pltpu

F32 = jnp.float32
BF16 = jnp.bfloat16

D_MODEL = 2048
DEPTH = 2
HEAD_DIM = 64
HALF = HEAD_DIM // 2
ATTN_WIDTH = 1024
ATTN_HEADS = 16
KV_HEADS = 4
KV_WIDTH = KV_HEADS * HEAD_DIM
ATTN_COLS = ATTN_WIDTH + 2 * KV_WIDTH
WINDOW = 128
ROPE_THETA = 10000.0
RWKV_WIDTH = 1024
DECAY_LORA = 64
AAA_LORA = 64
GATE_LORA = 160
LORA_COLS = DECAY_LORA + AAA_LORA + GATE_LORA
MV_LORA = 32
RWKV_GN_EPS = 64e-5
FFN_HIDDEN = 5632
LN_EPS = 1e-5
N_MOD = 6
DEEPNORM_ALPHA = (2.0 * DEPTH) ** 0.25

LANES = 128
N_TILES = RWKV_WIDTH // LANES
CHUNK = 64
WKV_SUB = 2
OUT_PARTS = 2
VMEM_LIMIT = 56 * 1024 * 1024

COL_Q, COL_AK, COL_AV, COL_LORA = 0, 1024, 1280, 1536
COL_R, COL_K, COL_V = 2048, 3072, 4096
N_PROJ = 5120
LORA_PAD = 512
SLAB = 256
PROJ_TN = 512

NN = ((1,), (0,))
NT = ((1,), (1,))
TN = ((0,), (0,))


def _dot(a, b, dims=NN):
    return lax.dot_general(a, b, (dims, ((), ())), preferred_element_type=F32)


def _bf(z):
    return z.astype(BF16)


def _mm(a, b):
    return _dot(_bf(a), _bf(b))


def _cumsum_rows(tri_bf16, b):
    b0 = _bf(b)
    r1 = b - b0.astype(F32)
    b1 = _bf(r1)
    b2 = _bf(r1 - b1.astype(F32))
    return _dot(tri_bf16, b0) + (_dot(tri_bf16, b1) + _dot(tri_bf16, b2))


def _head_sum(a, ones_bf16):
    a0 = _bf(a)
    a1 = _bf(a - a0.astype(F32))
    return _dot(a0, ones_bf16) + _dot(a1, ones_bf16)


def _pair_dots(lhs, rhs):
    out = []
    for p in range(0, len(lhs), 2):
        r0, r1 = _bf(rhs[p]), _bf(rhs[p + 1])
        z = jnp.zeros_like(r0)
        bd = jnp.concatenate([jnp.concatenate([r0, z], axis=1), jnp.concatenate([z, r1], axis=1)], axis=0)
        res = _dot(jnp.concatenate([_bf(lhs[p]), _bf(lhs[p + 1])], axis=1), bd)
        n = r0.shape[1]
        out += [res[:, :n], res[:, n:]]
    return out


def _iota(shape, dim):
    return lax.broadcasted_iota(jnp.int32, shape, dim)


def _layer_norm(z, w, b):
    mu = jnp.mean(z, axis=-1, keepdims=True)
    zc = z - mu
    var = jnp.mean(zc * zc, axis=-1, keepdims=True)
    return zc * lax.rsqrt(var + LN_EPS) * w + b


def _params(**kw):
    return pltpu.CompilerParams(vmem_limit_bytes=VMEM_LIMIT, **kw)


def _adaln_kernel(c_ref, w_ref, b_ref, o_ref, *, tn):
    cb = c_ref[...]
    cond = cb * jax.nn.sigmoid(cb)
    for j in range(tn // LANES):
        sl = slice(j * LANES, (j + 1) * LANES)
        o_ref[:, sl] = jnp.sum(w_ref[:, sl] * cond, axis=0, keepdims=True) + b_ref[:, sl]


def _adaln(c, w_ada, b_ada):
    depth, d, n = w_ada.shape
    tn = 1024
    c_b = jnp.broadcast_to(c.reshape(d, 1), (d, LANES))
    out = pl.pallas_call(
        functools.partial(_adaln_kernel, tn=tn),
        grid=(depth, n // tn),
        in_specs=[pl.BlockSpec((d, LANES), lambda l, j: (0, 0)),
                  pl.BlockSpec((None, d, tn), lambda l, j: (l, 0, j)),
                  pl.BlockSpec((None, 1, tn), lambda l, j: (l, 0, j))],
        out_specs=pl.BlockSpec((None, 1, tn), lambda l, j: (l, 0, j)),
        out_shape=jax.ShapeDtypeStruct((depth, 1, n), F32),
        compiler_params=_params(dimension_semantics=("parallel", "parallel")),
        name="adaln",
    )(c_b, w_ada, b_ada.reshape(depth, 1, n))
    return out.reshape(depth, N_MOD, d)


def _in_proj_kernel(x_ref, mod_ref, w_ref, o_ref, u_ref):
    @pl.when(pl.program_id(1) == 0)
    def _():
        u_ref[...] = _bf(x_ref[...] * (1.0 + mod_ref[1:2, :]) + mod_ref[0:1, :])

    o_ref[...] = _dot(u_ref[...], w_ref[...])


def _in_proj(x, mod, w_pad, l):
    t, d = x.shape
    tm, tn = 1024, PROJ_TN
    lora_out, lora_w = COL_LORA // tn, (ATTN_COLS + 3 * RWKV_WIDTH) // tn

    def w_block(j):
        return jnp.where(j < lora_out, j, jnp.where(j == lora_out, lora_w, j - 1))

    return pl.pallas_call(
        _in_proj_kernel,
        grid=(t // tm, N_PROJ // tn),
        in_specs=[pl.BlockSpec((tm, d), lambda i, j: (i, 0)),
                  pl.BlockSpec((None, N_MOD, d), lambda i, j: (l, 0, 0)),
                  pl.BlockSpec((None, d, tn), lambda i, j: (l, 0, w_block(j)))],
        out_specs=pl.BlockSpec((tm, tn), lambda i, j: (i, j)),
        out_shape=jax.ShapeDtypeStruct((t, N_PROJ), F32),
        scratch_shapes=[pltpu.VMEM((tm, d), BF16)],
        compiler_params=_params(dimension_semantics=("parallel", "arbitrary")),
        name="in_proj",
    )(x, mod, w_pad)


def _rope(t, cos, sin, lo_half):
    rot = jnp.where(lo_half, pltpu.roll(t, LANES - HALF, axis=1), pltpu.roll(t, HALF, axis=1))
    return t * cos + rot * sin


def _attn_kernel(sink_ref, q_ref, kc_ref, kp_ref, vc_ref, vp_ref, cosc_ref, sinc_ref, cosp_ref, sinp_ref,
                 o_ref, *, l):
    blk = pl.program_id(0)
    w = WINDOW
    cos_c, sin_c = cosc_ref[...], sinc_ref[...]
    cos_p, sin_p = cosp_ref[...], sinp_ref[...]

    lane = _iota((w, LANES), 1)
    lo_half = (lane % HEAD_DIM) < HALF
    in_a = lane < HEAD_DIM
    lane2 = _iota((2 * w, LANES), 1)
    in_a2 = lane2 < HEAD_DIM

    qa = _iota((2 * w, 2 * w), 0) % w
    ks = _iota((2 * w, 2 * w), 1)
    valid = (ks > qa) & (ks <= qa + w) & ((ks >= w) | (blk > 0))
    row_is_a = _iota((2 * w, 1), 0) < w
    n_tiles = ATTN_HEADS // 2
    tiles = range(n_tiles)

    ones_kv = jnp.ones((2 * w, LANES), BF16)
    k_dup, v_dup = [], []
    for m in range(KV_HEADS // 2):
        sl_kv = slice(m * LANES, (m + 1) * LANES)
        k2 = jnp.concatenate([_rope(kp_ref[:, sl_kv], cos_p, sin_p, lo_half),
                              _rope(kc_ref[:, sl_kv], cos_c, sin_c, lo_half)], axis=0)
        v2 = jnp.concatenate([vp_ref[:, sl_kv], vc_ref[:, sl_kv]], axis=0)
        k2r, v2r = pltpu.roll(k2, HEAD_DIM, axis=1), pltpu.roll(v2, HEAD_DIM, axis=1)
        k_dup += [_bf(jnp.where(in_a2, k2, k2r)), _bf(jnp.where(in_a2, k2r, k2))]
        v_dup += [jnp.concatenate([_bf(jnp.where(in_a2, v2, v2r)), ones_kv], axis=1),
                  jnp.concatenate([_bf(jnp.where(in_a2, v2r, v2)), ones_kv], axis=1)]

    s = []
    for t in tiles:
        q = _rope(q_ref[:, t * LANES:(t + 1) * LANES], cos_c, sin_c, lo_half) * (HEAD_DIM ** -0.5)
        q_st = _bf(jnp.concatenate([jnp.where(in_a, q, 0.0), jnp.where(in_a, 0.0, q)], axis=0))
        s.append(jnp.where(valid, _dot(q_st, k_dup[t // 2], NT), -1e30))
    sink = [jnp.where(row_is_a, sink_ref[l * ATTN_HEADS + 2 * t], sink_ref[l * ATTN_HEADS + 2 * t + 1])
            for t in tiles]
    mx = [jnp.maximum(jnp.max(s[t], axis=-1, keepdims=True), sink[t]) for t in tiles]
    p = [jnp.exp(s[t] - mx[t]) for t in tiles]
    for t in tiles:
        ov = _dot(_bf(p[t]), v_dup[t // 2])
        o_st = ov[:, :LANES] * (1.0 / (ov[:, LANES:] + jnp.exp(sink[t] - mx[t])))
        o_ref[:, t * LANES:(t + 1) * LANES] = _bf(jnp.where(in_a, o_st[:w], o_st[w:]))


def _attention(proj, sinks, cos_t, sin_t, l):
    t = proj.shape[0]
    w = WINDOW
    prev = lambda i, s: (jnp.maximum(i - 1, 0), 0)
    cur = lambda i, s: (i, 0)
    return pl.pallas_call(
        functools.partial(_attn_kernel, l=l),
        grid_spec=pltpu.PrefetchScalarGridSpec(
            num_scalar_prefetch=1,
            grid=(t // w,),
            in_specs=[pl.BlockSpec((w, ATTN_WIDTH), lambda i, s: (i, COL_Q // ATTN_WIDTH)),
                      pl.BlockSpec((w, KV_WIDTH), lambda i, s: (i, COL_AK // KV_WIDTH)),
                      pl.BlockSpec((w, KV_WIDTH), lambda i, s: (jnp.maximum(i - 1, 0), COL_AK // KV_WIDTH)),
                      pl.BlockSpec((w, KV_WIDTH), lambda i, s: (i, COL_AV // KV_WIDTH)),
                      pl.BlockSpec((w, KV_WIDTH), lambda i, s: (jnp.maximum(i - 1, 0), COL_AV // KV_WIDTH)),
                      pl.BlockSpec((w, LANES), cur), pl.BlockSpec((w, LANES), cur),
                      pl.BlockSpec((w, LANES), prev), pl.BlockSpec((w, LANES), prev)],
            out_specs=pl.BlockSpec((w, ATTN_WIDTH), lambda i, s: (i, 0)),
        ),
        out_shape=jax.ShapeDtypeStruct((t, ATTN_WIDTH), BF16),
        compiler_params=_params(dimension_semantics=("parallel",)),
        name="attention",
    )(sinks, proj, proj, proj, proj, proj, cos_t, sin_t, cos_t, sin_t)


def _prep_kernel(*refs, has_vres):
    if has_vres:
        (r_ref, k_ref, v_ref, l_ref, rp_ref, kp_ref, vp_ref, lp_ref, mu_ref, mul_ref, par_ref,
         w2_ref, a2_ref, g2_ref, ones_ref, vf_ref, v1_ref, v2_ref,
         ro_ref, lw_ref, ko_ref, vo_ref, kk_ref, b_ref, g_ref, bon_ref) = refs
    else:
        (r_ref, k_ref, v_ref, l_ref, rp_ref, kp_ref, vp_ref, lp_ref, mu_ref, mul_ref, par_ref,
         w2_ref, a2_ref, g2_ref, ones_ref,
         ro_ref, lw_ref, ko_ref, vo_ref, kk_ref, b_ref, g_ref, bon_ref) = refs
    first = pl.program_id(0) == 0

    def shifted_lerp(x_ref, p_ref, mu):
        x = x_ref[...]
        prev_row = jnp.where(first, 0.0, p_ref[7:8, :])
        xr = pltpu.roll(x, 1, axis=0)
        row = _iota((8, x.shape[1]), 0)
        xp = jnp.concatenate([jnp.where(row == 0, prev_row, xr[0:8]), xr[8:]], axis=0)
        return x + (xp - x) * mu

    r = shifted_lerp(r_ref, rp_ref, mu_ref[0:1, :])
    k = shifted_lerp(k_ref, kp_ref, mu_ref[1:2, :])
    v = shifted_lerp(v_ref, vp_ref, mu_ref[2:3, :])
    lo = shifted_lerp(l_ref, lp_ref, mul_ref[...])

    w0, a0 = par_ref[0:1, :], par_ref[1:2, :]
    k_k, k_a, r_k = par_ref[2:3, :], par_ref[3:4, :], par_ref[4:5, :]

    wa = lo[:, 0:LANES]
    gd = lo[:, LANES:3 * LANES]
    log_decay = -float(np.exp(-0.5)) * jax.nn.sigmoid(w0 + _mm(jnp.tanh(wa), w2_ref[...]))
    a = jax.nn.sigmoid(a0 + _mm(wa, a2_ref[...]))
    g = _mm(jax.nn.sigmoid(gd), g2_ref[...])
    if has_vres:
        v0 = par_ref[5:6, :]
        mix = jax.nn.sigmoid(v0 + _mm(_mm(v, v1_ref[...]), v2_ref[...]))
        v = v + (vf_ref[...].astype(F32) - v) * mix

    kk = k * k_k
    k_mod = k * (1.0 + (a - 1.0) * k_a)
    rk = r * k_mod * r_k
    ones = ones_ref[...]
    for j in range(RWKV_WIDTH // SLAB):
        sl = slice(j * SLAB, (j + 1) * SLAB)
        kk_j = kk[:, sl]
        ss = _mm(kk_j * kk_j, ones)
        kk_n = kk_j * lax.rsqrt(jnp.maximum(ss, 1e-24))
        kk_ref[:, sl] = _bf(kk_n)
        b_ref[:, sl] = _bf(kk_n * a[:, sl])
        bon_ref[:, sl] = _bf(_mm(rk[:, sl], ones) * v[:, sl])
    ro_ref[...] = _bf(r)
    lw_ref[...] = log_decay
    ko_ref[...] = _bf(k_mod)
    vo_ref[...] = _bf(v)
    g_ref[...] = _bf(g)


def _rwkv_prep(proj, mu3, mu_lora, params, w2p, a2p, g2p, ones, vres, l):
    t = proj.shape[0]
    tm = 256
    wdt = RWKV_WIDTH
    has_vres = vres is not None

    def col(cb, width):
        return pl.BlockSpec((tm, width), lambda i: (i, cb))

    def prev(cb, width):
        return pl.BlockSpec((8, width), lambda i: (jnp.maximum(i * (tm // 8) - 1, 0), cb))

    def layer(a, lyr=l):
        return pl.BlockSpec((None,) + a.shape[1:], lambda i: (lyr, 0, 0))

    in_specs = [col(COL_R // wdt, wdt), col(COL_K // wdt, wdt), col(COL_V // wdt, wdt),
                col(COL_LORA // LORA_PAD, LORA_PAD),
                prev(COL_R // wdt, wdt), prev(COL_K // wdt, wdt), prev(COL_V // wdt, wdt),
                prev(COL_LORA // LORA_PAD, LORA_PAD),
                layer(mu3), layer(mu_lora), layer(params), layer(w2p), layer(a2p), layer(g2p),
                pl.BlockSpec(ones.shape, lambda i: (0, 0))]
    args = [proj] * 8 + [mu3, mu_lora, params, w2p, a2p, g2p, ones]
    if has_vres:
        v_first, v1p, v2p = vres
        in_specs += [pl.BlockSpec((tm, wdt), lambda i: (i, 0)), layer(v1p, l - 1), layer(v2p, l - 1)]
        args += [v_first, v1p, v2p]
    out_spec = pl.BlockSpec((tm, wdt), lambda i: (i, 0))
    out_dtypes = [BF16, F32, BF16, BF16, BF16, BF16, BF16, BF16]
    return pl.pallas_call(
        functools.partial(_prep_kernel, has_vres=has_vres),
        grid=(t // tm,),
        in_specs=in_specs,
        out_specs=[out_spec] * 8,
        out_shape=[jax.ShapeDtypeStruct((t, wdt), dt) for dt in out_dtypes],
        compiler_params=_params(dimension_semantics=("parallel",)),
        name="rwkv_prep",
    )(*args)


def _wkv_kernel(r_ref, lw_ref, k_ref, v_ref, kk_ref, b_ref, o_ref, st_ref):
    c = CHUNK

    @pl.when(pl.program_id(0) == 0)
    def _():
        st_ref[...] = jnp.zeros_like(st_ref)

    row_t = _iota((c, c), 0)
    col_t = _iota((c, c), 1)
    l_incl = _bf(jnp.where(row_t >= col_t, 1.0, 0.0))

    lane = _iota((c, LANES), 1)
    m_a = lane < HEAD_DIM
    r2 = _iota((2 * c, 2 * c), 0)
    c2 = _iota((2 * c, 2 * c), 1)
    strict = r2 > c2
    incl = r2 >= c2
    eye = r2 == c2
    anti = (r2 < c) != (c2 < c)

    def stack(x):
        return jnp.concatenate([jnp.where(m_a, x, 0.0), jnp.where(m_a, 0.0, x)], axis=0)

    def stack_sw(x):
        return jnp.concatenate([jnp.where(m_a, 0.0, x), jnp.where(m_a, x, 0.0)], axis=0)

    tiles = range(N_TILES)
    sls = [slice(j * LANES, (j + 1) * LANES) for j in tiles]

    incl2 = jnp.concatenate([incl, incl], axis=1)
    blk = lambda s: (r2 // s) == (c2 // s)
    diag8 = strict & blk(8)

    def local_part(s, out):
        rows = slice(s * c, (s + 1) * c)
        cum_all = _cumsum_rows(l_incl, lw_ref[rows, :])
        a_st, r_st, rhs, v_sw, v_st, bk_tail, w_c = [], [], [], [], [], [], []
        for j in tiles:
            sl = sls[j]
            lw = lw_ref[rows, sl]
            cum = cum_all[:, sl]
            cum_last = cum[c - 1:c, :]
            e_neg = jnp.exp(-cum)
            e_tail = jnp.exp(cum_last - cum)
            k = k_ref[rows, sl].astype(F32)
            b = b_ref[rows, sl].astype(F32)
            a_st.append(stack(-kk_ref[rows, sl].astype(F32) * jnp.exp(cum - lw)))
            r_st.append(_bf(stack(r_ref[rows, sl].astype(F32) * jnp.exp(cum))))
            rhs.append(_bf(jnp.concatenate([stack(b * e_neg), stack(k * e_neg)], axis=0)))
            vs = pltpu.roll(v_ref[rows, sl].astype(F32), 64, axis=1)
            v_sw.append(_bf(vs))
            v_st.append(_bf(stack_sw(vs)))
            bk_tail.append(_bf(jnp.concatenate([b * e_tail, k * e_tail], axis=0)))
            w_c.append(jnp.exp(cum_last))
        yield
        d_ab, d_ak, d_r = [], [], []
        for j in tiles:
            gm = _dot(jnp.concatenate([_bf(a_st[j]), r_st[j]], axis=0), rhs[j], NT)
            d_ab.append(gm[:2 * c, :2 * c])
            d_ak.append(_bf(jnp.where(strict, gm[:2 * c, 2 * c:], 0.0)))
            d_r.append(_bf(jnp.where(incl2, gm[2 * c:, :], 0.0)))
        yield
        av = _pair_dots(d_ak, v_st)
        y0 = [a_st[j] + av[j] for j in tiles]
        d1 = [jnp.where(diag8, d_ab[j], 0.0) for j in tiles]
        p1 = [jnp.where(eye, 1.0, d1[j]) for j in tiles]
        d2 = _pair_dots(d1, d1)
        yield
        res = [_mm(d2[j], jnp.concatenate([p1[j], d2[j]], axis=1)) for j in tiles]
        p2 = [p1[j] + res[j][:, :2 * c] for j in tiles]
        yield
        d4p2 = _pair_dots([res[j][:, 2 * c:] for j in tiles], p2)
        tm = [p2[j] + d4p2[j] for j in tiles]
        for bs in (8, 16, 32):
            yield
            off = strict & blk(2 * bs) & jnp.logical_not(blk(bs))
            xm = _pair_dots(tm, [jnp.where(off, d_ab[j], 0.0) for j in tiles])
            yield
            xt = _pair_dots(xm, tm)
            tm = [tm[j] + xt[j] for j in tiles]
        yield
        out.update(y=_pair_dots(tm, y0), r_st=r_st, d_r=d_r, v_st=v_st, v_sw=v_sw, bk_tail=bk_tail, w_c=w_c)

    def state_part(s, loc, st):
        rows = slice(s * c, (s + 1) * c)
        ra, u_loc = [], []
        for j in tiles:
            yj = loc["y"][j]
            ah_st = jnp.concatenate([jnp.where(m_a, yj[:c], 0.0), jnp.where(m_a, 0.0, yj[c:])], axis=0)
            u_loc.append(jnp.concatenate([jnp.where(m_a, 0.0, yj[:c]), jnp.where(m_a, yj[c:], 0.0)], axis=0))
            ra.append(jnp.concatenate([loc["r_st"][j], _bf(ah_st)], axis=0))
        rs_as = _pair_dots(ra, st)
        u_st = [rs_as[j][2 * c:] + u_loc[j] for j in tiles]
        for j in tiles:
            o_st = rs_as[j][:2 * c] + _dot(loc["d_r"][j], jnp.concatenate([_bf(u_st[j]), loc["v_st"][j]], axis=0))
            o_ref[rows, sls[j]] = pltpu.roll(o_st[:c] + o_st[c:], 64, axis=1)
        new_st = []
        for j in tiles:
            u_sw = u_st[j][:c] + u_st[j][c:]
            delta = _dot(loc["bk_tail"][j], jnp.concatenate([_bf(u_sw), loc["v_sw"][j]], axis=0), TN)
            w_col = jnp.sum(jnp.where(eye, jnp.broadcast_to(loc["w_c"][j], (2 * c, 2 * c)), 0.0),
                            axis=1, keepdims=True)
            new_st.append(w_col * st[j] + jnp.where(anti, delta, 0.0))
        return new_st

    locs = [dict() for _ in range(WKV_SUB)]
    live = [local_part(s, locs[s]) for s in range(WKV_SUB)]
    while live:
        live = [g for g in live if next(g, "done") != "done"]
    st = [st_ref[j] for j in tiles]
    for s in range(WKV_SUB):
        st = state_part(s, locs[s], st)
    for j in tiles:
        st_ref[j] = st[j]


def _wkv(r, lw, k, v, kk, b):
    t, wdt = r.shape
    rows = WKV_SUB * CHUNK
    spec = pl.BlockSpec((rows, wdt), lambda i: (i, 0))
    return pl.pallas_call(
        _wkv_kernel,
        grid=(t // rows,),
        in_specs=[spec] * 6,
        out_specs=spec,
        out_shape=jax.ShapeDtypeStruct((t, wdt), F32),
        scratch_shapes=[pltpu.VMEM((N_TILES, LANES, LANES), F32)],
        compiler_params=_params(dimension_semantics=("arbitrary",)),
        name="wkv",
    )(r, lw, k, v, kk, b)


def _out_proj_kernel(ya_ref, o_ref, bon_ref, g_ref, gn_ref, ones_ref, w_ref, x_ref, mod_ref, ln_ref, out_ref,
                     *, parts):
    ones = ones_ref[...]
    pm = out_ref.shape[0] // parts
    slabs = [slice(j * SLAB, (j + 1) * SLAB) for j in range(RWKV_WIDTH // SLAB)]
    part_rows = [slice(p * pm, (p + 1) * pm) for p in range(parts)]
    oc = [[None] * len(slabs) for _ in range(parts)]
    var = [[None] * len(slabs) for _ in range(parts)]
    for p, rows in enumerate(part_rows):
        for j, sl in enumerate(slabs):
            o = o_ref[rows, sl]
            oc[p][j] = o - _head_sum(o, ones) * (1.0 / HEAD_DIM)
    for p in range(parts):
        for j in range(len(slabs)):
            var[p][j] = _mm(oc[p][j] * oc[p][j], ones) * (1.0 / HEAD_DIM)
    for p, rows in enumerate(part_rows):
        ys = []
        for j, sl in enumerate(slabs):
            y = ((oc[p][j] * lax.rsqrt(var[p][j] + RWKV_GN_EPS) * gn_ref[0:1, sl] + gn_ref[1:2, sl]
                  + bon_ref[rows, sl].astype(F32)) * g_ref[rows, sl].astype(F32))
            ys.append(_bf(y))
        mix = (_dot(ya_ref[rows, :], w_ref[:ATTN_WIDTH, :])
               + _dot(jnp.concatenate(ys, axis=1), w_ref[ATTN_WIDTH:, :]))
        z = DEEPNORM_ALPHA * x_ref[rows, :] + mod_ref[2:3, :] * mix
        out_ref[rows, :] = _layer_norm(z, ln_ref[0:1, :], ln_ref[1:2, :])


def _out_proj(y_attn, o, bonus, g, gn, ones, w_cat, x, mod, ln, l):
    t, d = x.shape
    tm = 512
    wdt = RWKV_WIDTH
    row = lambda width: pl.BlockSpec((tm, width), lambda i: (i, 0))
    layer = lambda a: pl.BlockSpec((None,) + a.shape[1:], lambda i: (l, 0, 0))
    return pl.pallas_call(
        functools.partial(_out_proj_kernel, parts=OUT_PARTS),
        grid=(t // tm,),
        in_specs=[row(ATTN_WIDTH), row(wdt), row(wdt), row(wdt), layer(gn),
                  pl.BlockSpec(ones.shape, lambda i: (0, 0)), layer(w_cat), row(d), layer(mod), layer(ln)],
        out_specs=row(d),
        out_shape=jax.ShapeDtypeStruct((t, d), F32),
        compiler_params=_params(dimension_semantics=("parallel",)),
        name="out_proj",
    )(y_attn, o, bonus, g, gn, ones, w_cat, x, mod, ln)


def _ffn_kernel(x_ref, mod_ref, ln_ref, wg_ref, wu_ref, wd_ref, out_ref, u_ref, acc_ref):
    j = pl.program_id(1)

    @pl.when(j == 0)
    def _():
        u_ref[...] = _bf(x_ref[...] * (1.0 + mod_ref[4:5, :]) + mod_ref[3:4, :])
        acc_ref[...] = jnp.zeros_like(acc_ref)

    u = u_ref[...]
    gate = _dot(u, wg_ref[...])
    up = _dot(u, wu_ref[...])
    h = _bf(gate * jax.nn.sigmoid(gate) * up)
    acc_ref[...] += _dot(h, wd_ref[...])

    @pl.when(j == pl.num_programs(1) - 1)
    def _():
        z = DEEPNORM_ALPHA * x_ref[...] + mod_ref[5:6, :] * acc_ref[...]
        out_ref[...] = _layer_norm(z, ln_ref[2:3, :], ln_ref[3:4, :])


def _ffn(x, mod, ln, w_gu, w_dn, l):
    t, d = x.shape
    f = w_dn.shape[1]
    tm, tf = 512, 512
    nf = f // tf
    return pl.pallas_call(
        _ffn_kernel,
        grid=(t // tm, nf),
        in_specs=[pl.BlockSpec((tm, d), lambda i, j: (i, 0)),
                  pl.BlockSpec((None,) + mod.shape[1:], lambda i, j: (l, 0, 0)),
                  pl.BlockSpec((None,) + ln.shape[1:], lambda i, j: (l, 0, 0)),
                  pl.BlockSpec((None, d, tf), lambda i, j: (l, 0, j)),
                  pl.BlockSpec((None, d, tf), lambda i, j: (l, 0, j + nf)),
                  pl.BlockSpec((None, tf, d), lambda i, j: (l, j, 0))],
        out_specs=pl.BlockSpec((tm, d), lambda i, j: (i, 0)),
        out_shape=jax.ShapeDtypeStruct((t, d), F32),
        scratch_shapes=[pltpu.VMEM((tm, d), BF16), pltpu.VMEM((tm, d), F32)],
        compiler_params=_params(dimension_semantics=("parallel", "arbitrary")),
        name="ffn",
    )(x, mod, ln, w_gu, w_gu, w_dn)


def _pad_axis(a, axis, before, total):
    pads = [(0, 0)] * a.ndim
    pads[axis] = (before, total - before - a.shape[axis])
    return jnp.pad(a, pads)


def kernel(x, c, positions, w_ada, b_ada, w_in, attn_sinks, rwkv_mu, rwkv_w0, rwkv_w2, rwkv_a0, rwkv_a2, rwkv_g2, rwkv_k_k, rwkv_k_a, rwkv_r_k, rwkv_gn_w, rwkv_gn_b, rwkv_v0, rwkv_v1, rwkv_v2, w_out, ln1_w, ln1_b, w_gate_up, w_down, ln2_w, ln2_b):
    bsz, t, d = x.shape
    assert bsz == 1 and d == D_MODEL
    depth = w_in.shape[0]
    rw = RWKV_WIDTH
    inv =ROPE_THETA ** (-jnp.arange(0, HEAD_DIM, 2, dtype=F32) / HEAD_DIM)
    ang = positions[0].astype(F32)[:, None] * inv
    cos, sin = jnp.cos(ang), jnp.sin(ang)
    cos_t = jnp.concatenate([cos, cos, cos, cos], axis=-1)
    sin_t = jnp.concatenate([-sin, sin, -sin, sin], axis=-1)

    w_pad = _pad_axis(w_in, 2, 0, N_PROJ).astype(BF16)
    sinks = attn_sinks.reshape(-1)
    mu3 = rwkv_mu[:, :3 * rw].reshape(depth, 3, rw)
    mu_lora = _pad_axis(rwkv_mu[:, 3 * rw:], 1, 0, LORA_PAD).reshape(depth, 1, LORA_PAD)
    zero = jnp.zeros((depth, rw), F32)
    v0_all = _pad_axis(rwkv_v0, 0, 1, depth)
    params = jnp.stack([rwkv_w0, rwkv_a0, rwkv_k_k, rwkv_k_a, rwkv_r_k.reshape(depth, rw), v0_all, zero, zero],
                       axis=1)
    w2p = _pad_axis(rwkv_w2, 1, 0, LANES)
    a2p = _pad_axis(rwkv_a2, 1, DECAY_LORA, LANES)
    g2p = _pad_axis(rwkv_g2, 1, 0, 2 * LANES)
    v1p = _pad_axis(rwkv_v1, 2, 0, LANES)
    v2p = _pad_axis(rwkv_v2, 1, 0, LANES)
    seg = np.arange(SLAB) // HEAD_DIM
    ones = jnp.asarray(seg[:, None] == seg[None, :], BF16)
    gn = jnp.stack([rwkv_gn_w, rwkv_gn_b], axis=1)
    w_out_bf = w_out.astype(BF16)
    ln = jnp.stack([ln1_w, ln1_b, ln2_w, ln2_b], axis=1)
    w_gu = w_gate_up.astype(BF16)
    w_dn = w_down.astype(BF16)

    mod = _adaln(c, w_ada, b_ada)
    xs = x[0]
    v_first = None
    for l in range(depth):
        proj = _in_proj(xs, mod, w_pad, l)
        y_attn = _attention(proj, sinks, cos_t, sin_t, l)
        vres = None if l == 0 else (v_first, v1p, v2p)
        r, lw, k_mod, v, kk, b, g, bonus = _rwkv_prep(proj, mu3, mu_lora, params, w2p, a2p, g2p, ones, vres, l)
        if l == 0:
            v_first = v
        o = _wkv(r, lw, k_mod, v, kk, b)
        xs = _out_proj(y_attn, o, bonus, g, gn, ones, w_out_bf, xs, mod, ln, l)
        xs = _ffn(xs, mod, ln, w_gu, w_dn, l)
    return xs[None]
```

```python
import functools

import numpy as np
import jax
import jax.numpy as jnp
from jax import lax
from jax.experimental import pallas as pl
from jax.experimental.pallas import tpu as pltpu

F32 = jnp.float32
BF16 = jnp.bfloat16

D_MODEL = 2048
DEPTH = 2
HEAD_DIM = 64
HALF = HEAD_DIM // 2
ATTN_WIDTH = 1024
ATTN_HEADS = 16
KV_HEADS = 4
KV_WIDTH = KV_HEADS * HEAD_DIM
ATTN_COLS = ATTN_WIDTH + 2 * KV_WIDTH
WINDOW = 128
ROPE_THETA = 10000.0
RWKV_WIDTH = 1024
DECAY_LORA = 64
AAA_LORA = 64
GATE_LORA = 160
LORA_COLS = DECAY_LORA + AAA_LORA + GATE_LORA
MV_LORA = 32
RWKV_GN_EPS = 64e-5
FFN_HIDDEN = 5632
LN_EPS = 1e-5
N_MOD = 6
DEEPNORM_ALPHA = (2.0 * DEPTH) ** 0.25

LANES = 128
N_TILES = RWKV_WIDTH // LANES
CHUNK = 64
WKV_SUB = 2
OUT_PARTS = 2
VMEM_LIMIT = 56 * 1024 * 1024

COL_Q, COL_AK, COL_AV, COL_LORA = 0, 1024, 1280, 1536
COL_R, COL_K, COL_V = 2048, 3072, 4096
N_PROJ = 5120
LORA_PAD = 512
SLAB = 256
PROJ_TN = 512

NN = ((1,), (0,))
NT = ((1,), (1,))
TN = ((0,), (0,))


def _dot(a, b, dims=NN):
    return lax.dot_general(a, b, (dims, ((), ())), preferred_element_type=F32)


def _bf(z):
    return z.astype(BF16)


def _mm(a, b):
    return _dot(_bf(a), _bf(b))


def _cumsum_rows(tri_bf16, b):
    b0 = _bf(b)
    r1 = b - b0.astype(F32)
    b1 = _bf(r1)
    b2 = _bf(r1 - b1.astype(F32))
    return _dot(tri_bf16, b0) + (_dot(tri_bf16, b1) + _dot(tri_bf16, b2))


def _head_sum(a, ones_bf16):
    a0 = _bf(a)
    a1 = _bf(a - a0.astype(F32))
    return _dot(a0, ones_bf16) + _dot(a1, ones_bf16)


def _pair_dots(lhs, rhs):
    out = []
    for p in range(0, len(lhs), 2):
        r0, r1 = _bf(rhs[p]), _bf(rhs[p + 1])
        z = jnp.zeros_like(r0)
        bd = jnp.concatenate([jnp.concatenate([r0, z], axis=1), jnp.concatenate([z, r1], axis=1)], axis=0)
        res = _dot(jnp.concatenate([_bf(lhs[p]), _bf(lhs[p + 1])], axis=1), bd)
        n = r0.shape[1]
        out += [res[:, :n], res[:, n:]]
    return out


def _iota(shape, dim):
    return lax.broadcasted_iota(jnp.int32, shape, dim)


def _layer_norm(z, w, b):
    mu = jnp.mean(z, axis=-1, keepdims=True)
    zc = z - mu
    var = jnp.mean(zc * zc, axis=-1, keepdims=True)
    return zc * lax.rsqrt(var + LN_EPS) * w + b


def _params(**kw):
    return pltpu.CompilerParams(vmem_limit_bytes=VMEM_LIMIT, **kw)


def _adaln_kernel(c_ref, w_ref, b_ref, o_ref, *, tn):
    cb = c_ref[...]
    cond = cb * jax.nn.sigmoid(cb)
    for j in range(tn // LANES):
        sl = slice(j * LANES, (j + 1) * LANES)
        o_ref[:, sl] = jnp.sum(w_ref[:, sl] * cond, axis=0, keepdims=True) + b_ref[:, sl]


def _adaln(c, w_ada, b_ada):
    depth, d, n = w_ada.shape
    tn = 1024
    c_b = jnp.broadcast_to(c.reshape(d, 1), (d, LANES))
    out = pl.pallas_call(
        functools.partial(_adaln_kernel, tn=tn),
        grid=(depth, n // tn),
        in_specs=[pl.BlockSpec((d, LANES), lambda l, j: (0, 0)),
                  pl.BlockSpec((None, d, tn), lambda l, j: (l, 0, j)),
                  pl.BlockSpec((None, 1, tn), lambda l, j: (l, 0, j))],
        out_specs=pl.BlockSpec((None, 1, tn), lambda l, j: (l, 0, j)),
        out_shape=jax.ShapeDtypeStruct((depth, 1, n), F32),
        compiler_params=_params(dimension_semantics=("parallel", "parallel")),
        name="adaln",
    )(c_b, w_ada, b_ada.reshape(depth, 1, n))
    return out.reshape(depth, N_MOD, d)


def _in_proj_kernel(x_ref, mod_ref, w_ref, wgu_ref, wdn_ref, wout_ref, o_ref, wgu_o, wdn_o, wout_o, u_ref):
    @pl.when(pl.program_id(1) == 0)
    def _():
        u_ref[...] = _bf(x_ref[...] * (1.0 + mod_ref[1:2, :]) + mod_ref[0:1, :])

    wgu_o[...] = _bf(wgu_ref[...])
    wdn_o[...] = _bf(wdn_ref[...])
    wout_o[...] = _bf(wout_ref[...])
    o_ref[...] = _dot(u_ref[...], w_ref[...])


def _in_proj(x, mod, w_pad, w_gate_up, w_down, w_out, l):
    t, d = x.shape
    tm, tn = 1024, PROJ_TN
    lora_out, lora_w = COL_LORA // tn, (ATTN_COLS + 3 * RWKV_WIDTH) // tn
    n_j = N_PROJ // tn
    steps = (t // tm) * n_j

    def w_block(j):
        return jnp.where(j < lora_out, j, jnp.where(j == lora_out, lora_w, j - 1))

    def cast_specs(w, n_blocks):
        assert n_blocks <= steps and w.shape[1] % (16 * n_blocks) == 0
        rows = w.shape[1] // n_blocks
        idx = lambda i, j: (jnp.minimum(i * n_j + j, n_blocks - 1), 0)
        return (pl.BlockSpec((None, rows, w.shape[2]), lambda i, j: (l,) + idx(i, j)),
                pl.BlockSpec((rows, w.shape[2]), idx),
                jax.ShapeDtypeStruct(w.shape[1:], BF16))

    casts = [cast_specs(w_gate_up, 64), cast_specs(w_down, 32), cast_specs(w_out, 64)]
    return pl.pallas_call(
        _in_proj_kernel,
        grid=(t // tm, n_j),
        in_specs=[pl.BlockSpec((tm, d), lambda i, j: (i, 0)),
                  pl.BlockSpec((None, N_MOD, d), lambda i, j: (l, 0, 0)),
                  pl.BlockSpec((None, d, tn), lambda i, j: (l, 0, w_block(j)))] + [c[0] for c in casts],
        out_specs=[pl.BlockSpec((tm, tn), lambda i, j: (i, j))] + [c[1] for c in casts],
        out_shape=[jax.ShapeDtypeStruct((t, N_PROJ), F32)] + [c[2] for c in casts],
        scratch_shapes=[pltpu.VMEM((tm, d), BF16)],
        compiler_params=_params(dimension_semantics=("arbitrary", "arbitrary")),
        name="in_proj",
    )(x, mod, w_pad, w_gate_up, w_down, w_out)


def _rope(t, cos, sin, lo_half):
    rot = jnp.where(lo_half, pltpu.roll(t, LANES - HALF, axis=1), pltpu.roll(t, HALF, axis=1))
    return t * cos + rot * sin


def _attn_kernel(sink_ref, q_ref, kc_ref, kp_ref, vc_ref, vp_ref, cosc_ref, sinc_ref, cosp_ref, sinp_ref,
                 o_ref, *, l):
    blk = pl.program_id(0)
    w = WINDOW
    cos_c, sin_c = cosc_ref[...], sinc_ref[...]
    cos_p, sin_p = cosp_ref[...], sinp_ref[...]

    lane = _iota((w, LANES), 1)
    lo_half = (lane % HEAD_DIM) < HALF
    in_a = lane < HEAD_DIM
    lane2 = _iota((2 * w, LANES), 1)
    in_a2 = lane2 < HEAD_DIM

    qa = _iota((2 * w, 2 * w), 0) % w
    ks = _iota((2 * w, 2 * w), 1)
    valid = (ks > qa) & (ks <= qa + w) & ((ks >= w) | (blk > 0))
    row_is_a = _iota((2 * w, 1), 0) < w
    n_tiles = ATTN_HEADS // 2
    tiles = range(n_tiles)

    ones_kv = jnp.ones((2 * w, LANES), BF16)
    k_dup, v_dup = [], []
    for m in range(KV_HEADS // 2):
        sl_kv = slice(m * LANES, (m + 1) * LANES)
        k2 = jnp.concatenate([_rope(kp_ref[:, sl_kv], cos_p, sin_p, lo_half),
                              _rope(kc_ref[:, sl_kv], cos_c, sin_c, lo_half)], axis=0)
        v2 = jnp.concatenate([vp_ref[:, sl_kv], vc_ref[:, sl_kv]], axis=0)
        k2r, v2r = pltpu.roll(k2, HEAD_DIM, axis=1), pltpu.roll(v2, HEAD_DIM, axis=1)
        k_dup += [_bf(jnp.where(in_a2, k2, k2r)), _bf(jnp.where(in_a2, k2r, k2))]
        v_dup += [jnp.concatenate([_bf(jnp.where(in_a2, v2, v2r)), ones_kv], axis=1),
                  jnp.concatenate([_bf(jnp.where(in_a2, v2r, v2)), ones_kv], axis=1)]

    s = []
    for t in tiles:
        q = _rope(q_ref[:, t * LANES:(t + 1) * LANES], cos_c, sin_c, lo_half) * (HEAD_DIM ** -0.5)
        q_st = _bf(jnp.concatenate([jnp.where(in_a, q, 0.0), jnp.where(in_a, 0.0, q)], axis=0))
        s.append(jnp.where(valid, _dot(q_st, k_dup[t // 2], NT), -1e30))
    sink = [jnp.where(row_is_a, sink_ref[l * ATTN_HEADS + 2 * t], sink_ref[l * ATTN_HEADS + 2 * t + 1])
            for t in tiles]
    mx = [jnp.maximum(jnp.max(s[t], axis=-1, keepdims=True), sink[t]) for t in tiles]
    p = [jnp.exp(s[t] - mx[t]) for t in tiles]
    for t in tiles:
        ov = _dot(_bf(p[t]), v_dup[t // 2])
        o_st = ov[:, :LANES] * (1.0 / (ov[:, LANES:] + jnp.exp(sink[t] - mx[t])))
        o_ref[:, t * LANES:(t + 1) * LANES] = _bf(jnp.where(in_a, o_st[:w], o_st[w:]))


def _attention(proj, sinks, cos_t, sin_t, l):
    t = proj.shape[0]
    w = WINDOW
    prev = lambda i, s: (jnp.maximum(i - 1, 0), 0)
    cur = lambda i, s: (i, 0)
    return pl.pallas_call(
        functools.partial(_attn_kernel, l=l),
        grid_spec=pltpu.PrefetchScalarGridSpec(
            num_scalar_prefetch=1,
            grid=(t // w,),
            in_specs=[pl.BlockSpec((w, ATTN_WIDTH), lambda i, s: (i, COL_Q // ATTN_WIDTH)),
                      pl.BlockSpec((w, KV_WIDTH), lambda i, s: (i, COL_AK // KV_WIDTH)),
                      pl.BlockSpec((w, KV_WIDTH), lambda i, s: (jnp.maximum(i - 1, 0), COL_AK // KV_WIDTH)),
                      pl.BlockSpec((w, KV_WIDTH), lambda i, s: (i, COL_AV // KV_WIDTH)),
                      pl.BlockSpec((w, KV_WIDTH), lambda i, s: (jnp.maximum(i - 1, 0), COL_AV // KV_WIDTH)),
                      pl.BlockSpec((w, LANES), cur), pl.BlockSpec((w, LANES), cur),
                      pl.BlockSpec((w, LANES), prev), pl.BlockSpec((w, LANES), prev)],
            out_specs=pl.BlockSpec((w, ATTN_WIDTH), lambda i, s: (i, 0)),
        ),
        out_shape=jax.ShapeDtypeStruct((t, ATTN_WIDTH), BF16),
        compiler_params=_params(dimension_semantics=("parallel",)),
        name="attention",
    )(sinks, proj, proj, proj, proj, proj, cos_t, sin_t, cos_t, sin_t)


def _prep_kernel(*refs, has_vres):
    if has_vres:
        (r_ref, k_ref, v_ref, l_ref, rp_ref, kp_ref, vp_ref, lp_ref, mu_ref, mul_ref, par_ref,
         w2_ref, a2_ref, g2_ref, ones_ref, vf_ref, v1_ref, v2_ref,
         ro_ref, lw_ref, ko_ref, vo_ref, kk_ref, b_ref, g_ref, bon_ref) = refs
    else:
        (r_ref, k_ref, v_ref, l_ref, rp_ref, kp_ref, vp_ref, lp_ref, mu_ref, mul_ref, par_ref,
         w2_ref, a2_ref, g2_ref, ones_ref,
         ro_ref, lw_ref, ko_ref, vo_ref, kk_ref, b_ref, g_ref, bon_ref) = refs
    first = pl.program_id(0) == 0

    def shifted_lerp(x_ref, p_ref, mu):
        x = x_ref[...]
        prev_row = jnp.where(first, 0.0, p_ref[7:8, :])
        xr = pltpu.roll(x, 1, axis=0)
        row = _iota((8, x.shape[1]), 0)
        xp = jnp.concatenate([jnp.where(row == 0, prev_row, xr[0:8]), xr[8:]], axis=0)
        return x + (xp - x) * mu

    r = shifted_lerp(r_ref, rp_ref, mu_ref[0:1, :])
    k = shifted_lerp(k_ref, kp_ref, mu_ref[1:2, :])
    v = shifted_lerp(v_ref, vp_ref, mu_ref[2:3, :])
    lo = shifted_lerp(l_ref, lp_ref, mul_ref[...])

    w0, a0 = par_ref[0:1, :], par_ref[1:2, :]
    k_k, k_a, r_k = par_ref[2:3, :], par_ref[3:4, :], par_ref[4:5, :]

    wa = lo[:, 0:LANES]
    gd = lo[:, LANES:3 * LANES]
    log_decay = -float(np.exp(-0.5)) * jax.nn.sigmoid(w0 + _mm(jnp.tanh(wa), w2_ref[...]))
    a = jax.nn.sigmoid(a0 + _mm(wa, a2_ref[...]))
    g = _mm(jax.nn.sigmoid(gd), g2_ref[...])
    if has_vres:
        v0 = par_ref[5:6, :]
        mix = jax.nn.sigmoid(v0 + _mm(_mm(v, v1_ref[...]), v2_ref[...]))
        v = v + (vf_ref[...].astype(F32) - v) * mix

    kk = k * k_k
    k_mod = k * (1.0 + (a - 1.0) * k_a)
    rk = r * k_mod * r_k
    ones = ones_ref[...]
    for j in range(RWKV_WIDTH // SLAB):
        sl = slice(j * SLAB, (j + 1) * SLAB)
        kk_j = kk[:, sl]
        ss = _mm(kk_j * kk_j, ones)
        kk_n = kk_j * lax.rsqrt(jnp.maximum(ss, 1e-24))
        kk_ref[:, sl] = _bf(kk_n)
        b_ref[:, sl] = _bf(kk_n * a[:, sl])
        bon_ref[:, sl] = _bf(_mm(rk[:, sl], ones) * v[:, sl])
    ro_ref[...] = _bf(r)
    lw_ref[...] = log_decay
    ko_ref[...] = _bf(k_mod)
    vo_ref[...] = _bf(v)
    g_ref[...] = _bf(g)


def _rwkv_prep(proj, mu3, mu_lora, params, w2p, a2p, g2p, ones, vres, l):
    t = proj.shape[0]
    tm = 256
    wdt = RWKV_WIDTH
    has_vres = vres is not None

    def col(cb, width):
        return pl.BlockSpec((tm, width), lambda i: (i, cb))

    def prev(cb, width):
        return pl.BlockSpec((8, width), lambda i: (jnp.maximum(i * (tm // 8) - 1, 0), cb))

    def layer(a, lyr=l):
        return pl.BlockSpec((None,) + a.shape[1:], lambda i: (lyr, 0, 0))

    in_specs = [col(COL_R // wdt, wdt), col(COL_K // wdt, wdt), col(COL_V // wdt, wdt),
                col(COL_LORA // LORA_PAD, LORA_PAD),
                prev(COL_R // wdt, wdt), prev(COL_K // wdt, wdt), prev(COL_V // wdt, wdt),
                prev(COL_LORA // LORA_PAD, LORA_PAD),
                layer(mu3), layer(mu_lora), layer(params), layer(w2p), layer(a2p), layer(g2p),
                pl.BlockSpec(ones.shape, lambda i: (0, 0))]
    args = [proj] * 8 + [mu3, mu_lora, params, w2p, a2p, g2p, ones]
    if has_vres:
        v_first, v1p, v2p = vres
        in_specs += [pl.BlockSpec((tm, wdt), lambda i: (i, 0)), layer(v1p, l - 1), layer(v2p, l - 1)]
        args += [v_first, v1p, v2p]
    out_spec = pl.BlockSpec((tm, wdt), lambda i: (i, 0))
    out_dtypes = [BF16, F32, BF16, BF16, BF16, BF16, BF16, BF16]
    return pl.pallas_call(
        functools.partial(_prep_kernel, has_vres=has_vres),
        grid=(t // tm,),
        in_specs=in_specs,
        out_specs=[out_spec] * 8,
        out_shape=[jax.ShapeDtypeStruct((t, wdt), dt) for dt in out_dtypes],
        compiler_params=_params(dimension_semantics=("parallel",)),
        name="rwkv_prep",
    )(*args)


def _wkv_kernel(r_ref, lw_ref, k_ref, v_ref, kk_ref, b_ref, o_ref, st_ref):
    c = CHUNK

    @pl.when(pl.program_id(0) == 0)
    def _():
        st_ref[...] = jnp.zeros_like(st_ref)

    row_t = _iota((c, c), 0)
    col_t = _iota((c, c), 1)
    l_incl = _bf(jnp.where(row_t >= col_t, 1.0, 0.0))

    lane = _iota((c, LANES), 1)
    m_a = lane < HEAD_DIM
    r2 = _iota((2 * c, 2 * c), 0)
    c2 = _iota((2 * c, 2 * c), 1)
    strict = r2 > c2
    incl = r2 >= c2
    eye = r2 == c2
    anti = (r2 < c) != (c2 < c)

    def stack(x):
        return jnp.concatenate([jnp.where(m_a, x, 0.0), jnp.where(m_a, 0.0, x)], axis=0)

    def stack_sw(x):
        return jnp.concatenate([jnp.where(m_a, 0.0, x), jnp.where(m_a, x, 0.0)], axis=0)

    tiles = range(N_TILES)
    sls = [slice(j * LANES, (j + 1) * LANES) for j in tiles]

    incl2 = jnp.concatenate([incl, incl], axis=1)
    blk = lambda s: (r2 // s) == (c2 // s)
    diag8 = strict & blk(8)

    def local_part(s, out):
        rows = slice(s * c, (s + 1) * c)
        cum_all = _cumsum_rows(l_incl, lw_ref[rows, :])
        a_st, r_st, rhs, v_sw, v_st, bk_tail, w_c = [], [], [], [], [], [], []
        for j in tiles:
            sl = sls[j]
            lw = lw_ref[rows, sl]
            cum = cum_all[:, sl]
            cum_last = cum[c - 1:c, :]
            e_neg = jnp.exp(-cum)
            e_tail = jnp.exp(cum_last - cum)
            k = k_ref[rows, sl].astype(F32)
            b = b_ref[rows, sl].astype(F32)
            a_st.append(stack(-kk_ref[rows, sl].astype(F32) * jnp.exp(cum - lw)))
            r_st.append(_bf(stack(r_ref[rows, sl].astype(F32) * jnp.exp(cum))))
            rhs.append(_bf(jnp.concatenate([stack(b * e_neg), stack(k * e_neg)], axis=0)))
            vs = pltpu.roll(v_ref[rows, sl].astype(F32), 64, axis=1)
            v_sw.append(_bf(vs))
            v_st.append(_bf(stack_sw(vs)))
            bk_tail.append(_bf(jnp.concatenate([b * e_tail, k * e_tail], axis=0)))
            w_c.append(jnp.exp(cum_last))
        yield
        d_ab, d_ak, d_r = [], [], []
        for j in tiles:
            gm = _dot(jnp.concatenate([_bf(a_st[j]), r_st[j]], axis=0), rhs[j], NT)
            d_ab.append(gm[:2 * c, :2 * c])
            d_ak.append(_bf(jnp.where(strict, gm[:2 * c, 2 * c:], 0.0)))
            d_r.append(_bf(jnp.where(incl2, gm[2 * c:, :], 0.0)))
        yield
        av = _pair_dots(d_ak, v_st)
        y0 = [a_st[j] + av[j] for j in tiles]
        d1 = [jnp.where(diag8, d_ab[j], 0.0) for j in tiles]
        p1 = [jnp.where(eye, 1.0, d1[j]) for j in tiles]
        d2 = _pair_dots(d1, d1)
        yield
        res = [_mm(d2[j], jnp.concatenate([p1[j], d2[j]], axis=1)) for j in tiles]
        p2 = [p1[j] + res[j][:, :2 * c] for j in tiles]
        yield
        d4p2 = _pair_dots([res[j][:, 2 * c:] for j in tiles], p2)
        tm = [p2[j] + d4p2[j] for j in tiles]
        for bs in (8, 16, 32):
            yield
            off = strict & blk(2 * bs) & jnp.logical_not(blk(bs))
            xm = _pair_dots(tm, [jnp.where(off, d_ab[j], 0.0) for j in tiles])
            yield
            xt = _pair_dots(xm, tm)
            tm = [tm[j] + xt[j] for j in tiles]
        yield
        out.update(y=_pair_dots(tm, y0), r_st=r_st, d_r=d_r, v_st=v_st, v_sw=v_sw, bk_tail=bk_tail, w_c=w_c)

    def state_part(s, loc, st):
        rows = slice(s * c, (s + 1) * c)
        ra, u_loc = [], []
        for j in tiles:
            yj = loc["y"][j]
            ah_st = jnp.concatenate([jnp.where(m_a, yj[:c], 0.0), jnp.where(m_a, 0.0, yj[c:])], axis=0)
            u_loc.append(jnp.concatenate([jnp.where(m_a, 0.0, yj[:c]), jnp.where(m_a, yj[c:], 0.0)], axis=0))
            ra.append(jnp.concatenate([loc["r_st"][j], _bf(ah_st)], axis=0))
        rs_as = _pair_dots(ra, st)
        u_st = [rs_as[j][2 * c:] + u_loc[j] for j in tiles]
        for j in tiles:
            o_st = rs_as[j][:2 * c] + _dot(loc["d_r"][j], jnp.concatenate([_bf(u_st[j]), loc["v_st"][j]], axis=0))
            o_ref[rows, sls[j]] = pltpu.roll(o_st[:c] + o_st[c:], 64, axis=1)
        new_st = []
        for j in tiles:
            u_sw = u_st[j][:c] + u_st[j][c:]
            delta = _dot(loc["bk_tail"][j], jnp.concatenate([_bf(u_sw), loc["v_sw"][j]], axis=0), TN)
            w_col = jnp.sum(jnp.where(eye, jnp.broadcast_to(loc["w_c"][j], (2 * c, 2 * c)), 0.0),
                            axis=1, keepdims=True)
            new_st.append(w_col * st[j] + jnp.where(anti, delta, 0.0))
        return new_st

    locs = [dict() for _ in range(WKV_SUB)]
    live = [local_part(s, locs[s]) for s in range(WKV_SUB)]
    while live:
        live = [g for g in live if next(g, "done") != "done"]
    st = [st_ref[j] for j in tiles]
    for s in range(WKV_SUB):
        st = state_part(s, locs[s], st)
    for j in tiles:
        st_ref[j] = st[j]


def _wkv(r, lw, k, v, kk, b):
    t, wdt = r.shape
    rows = WKV_SUB * CHUNK
    spec = pl.BlockSpec((rows, wdt), lambda i: (i, 0))
    return pl.pallas_call(
        _wkv_kernel,
        grid=(t // rows,),
        in_specs=[spec] * 6,
        out_specs=spec,
        out_shape=jax.ShapeDtypeStruct((t, wdt), F32),
        scratch_shapes=[pltpu.VMEM((N_TILES, LANES, LANES), F32)],
        compiler_params=_params(dimension_semantics=("arbitrary",)),
        name="wkv",
    )(r, lw, k, v, kk, b)


def _out_proj_kernel(ya_ref, o_ref, bon_ref, g_ref, gn_ref, ones_ref, w_ref, x_ref, mod_ref, ln_ref, out_ref,
                     *, parts):
    ones = ones_ref[...]
    pm = out_ref.shape[0] // parts
    slabs = [slice(j * SLAB, (j + 1) * SLAB) for j in range(RWKV_WIDTH // SLAB)]
    part_rows = [slice(p * pm, (p + 1) * pm) for p in range(parts)]
    oc = [[None] * len(slabs) for _ in range(parts)]
    var = [[None] * len(slabs) for _ in range(parts)]
    for p, rows in enumerate(part_rows):
        for j, sl in enumerate(slabs):
            o = o_ref[rows, sl]
            oc[p][j] = o - _head_sum(o, ones) * (1.0 / HEAD_DIM)
    for p in range(parts):
        for j in range(len(slabs)):
            var[p][j] = _mm(oc[p][j] * oc[p][j], ones) * (1.0 / HEAD_DIM)
    for p, rows in enumerate(part_rows):
        ys = []
        for j, sl in enumerate(slabs):
            y = ((oc[p][j] * lax.rsqrt(var[p][j] + RWKV_GN_EPS) * gn_ref[0:1, sl] + gn_ref[1:2, sl]
                  + bon_ref[rows, sl].astype(F32)) * g_ref[rows, sl].astype(F32))
            ys.append(_bf(y))
        mix = (_dot(ya_ref[rows, :], w_ref[:ATTN_WIDTH, :])
               + _dot(jnp.concatenate(ys, axis=1), w_ref[ATTN_WIDTH:, :]))
        z = DEEPNORM_ALPHA * x_ref[rows, :] + mod_ref[2:3, :] * mix
        out_ref[rows, :] = _layer_norm(z, ln_ref[0:1, :], ln_ref[1:2, :])


def _out_proj(y_attn, o, bonus, g, gn, ones, w_cat, x, mod, ln, l):
    t, d = x.shape
    tm = 512
    wdt = RWKV_WIDTH
    row = lambda width: pl.BlockSpec((tm, width), lambda i: (i, 0))
    layer = lambda a: pl.BlockSpec((None,) + a.shape[1:], lambda i: (l, 0, 0))
    return pl.pallas_call(
        functools.partial(_out_proj_kernel, parts=OUT_PARTS),
        grid=(t // tm,),
        in_specs=[row(ATTN_WIDTH), row(wdt), row(wdt), row(wdt), layer(gn),
                  pl.BlockSpec(ones.shape, lambda i: (0, 0)), pl.BlockSpec(w_cat.shape, lambda i: (0, 0)), row(d),
                  layer(mod), layer(ln)],
        out_specs=row(d),
        out_shape=jax.ShapeDtypeStruct((t, d), F32),
        compiler_params=_params(dimension_semantics=("parallel",)),
        name="out_proj",
    )(y_attn, o, bonus, g, gn, ones, w_cat, x, mod, ln)


def _ffn_kernel(x_ref, mod_ref, ln_ref, wg_ref, wu_ref, wd_ref, out_ref, u_ref, acc_ref):
    j = pl.program_id(1)

    @pl.when(j == 0)
    def _():
        u_ref[...] = _bf(x_ref[...] * (1.0 + mod_ref[4:5, :]) + mod_ref[3:4, :])
        acc_ref[...] = jnp.zeros_like(acc_ref)

    u = u_ref[...]
    gate = _dot(u, wg_ref[...])
    up = _dot(u, wu_ref[...])
    h = _bf(gate * jax.nn.sigmoid(gate) * up)
    acc_ref[...] += _dot(h, wd_ref[...])

    @pl.when(j == pl.num_programs(1) - 1)
    def _():
        z = DEEPNORM_ALPHA * x_ref[...] + mod_ref[5:6, :] * acc_ref[...]
        out_ref[...] = _layer_norm(z, ln_ref[2:3, :], ln_ref[3:4, :])


def _ffn(x, mod, ln, w_gu, w_dn, l):
    t, d = x.shape
    f = w_dn.shape[0]
    tm, tf = 512, 512
    nf = f // tf
    return pl.pallas_call(
        _ffn_kernel,
        grid=(t // tm, nf),
        in_specs=[pl.BlockSpec((tm, d), lambda i, j: (i, 0)),
                  pl.BlockSpec((None,) + mod.shape[1:], lambda i, j: (l, 0, 0)),
                  pl.BlockSpec((None,) + ln.shape[1:], lambda i, j: (l, 0, 0)),
                  pl.BlockSpec((d, tf), lambda i, j: (0, j)),
                  pl.BlockSpec((d, tf), lambda i, j: (0, j + nf)),
                  pl.BlockSpec((tf, d), lambda i, j: (j, 0))],
        out_specs=pl.BlockSpec((tm, d), lambda i, j: (i, 0)),
        out_shape=jax.ShapeDtypeStruct((t, d), F32),
        scratch_shapes=[pltpu.VMEM((tm, d), BF16), pltpu.VMEM((tm, d), F32)],
        compiler_params=_params(dimension_semantics=("parallel", "arbitrary")),
        name="ffn",
    )(x, mod, ln, w_gu, w_gu, w_dn)


def _pad_axis(a, axis, before, total):
    pads = [(0, 0)] * a.ndim
    pads[axis] = (before, total - before - a.shape[axis])
    return jnp.pad(a, pads)


def kernel(x, c, positions, w_ada, b_ada, w_in, attn_sinks, rwkv_mu, rwkv_w0, rwkv_w2, rwkv_a0, rwkv_a2, rwkv_g2, rwkv_k_k, rwkv_k_a, rwkv_r_k, rwkv_gn_w, rwkv_gn_b, rwkv_v0, rwkv_v1, rwkv_v2, w_out, ln1_w, ln1_b, w_gate_up, w_down, ln2_w, ln2_b):
    bsz, t, d = x.shape
    assert bsz == 1 and d == D_MODEL
    depth = w_in.shape[0]
    rw = RWKV_WIDTH
    inv =ROPE_THETA ** (-jnp.arange(0, HEAD_DIM, 2, dtype=F32) / HEAD_DIM)
    ang = positions[0].astype(F32)[:, None] * inv
    cos, sin = jnp.cos(ang), jnp.sin(ang)
    cos_t = jnp.concatenate([cos, cos, cos, cos], axis=-1)
    sin_t = jnp.concatenate([-sin, sin, -sin, sin], axis=-1)

    w_pad = _pad_axis(w_in, 2, 0, N_PROJ).astype(BF16)
    sinks = attn_sinks.reshape(-1)
    mu3 = rwkv_mu[:, :3 * rw].reshape(depth, 3, rw)
    mu_lora = _pad_axis(rwkv_mu[:, 3 * rw:], 1, 0, LORA_PAD).reshape(depth, 1, LORA_PAD)
    zero = jnp.zeros((depth, rw), F32)
    v0_all = _pad_axis(rwkv_v0, 0, 1, depth)
    params = jnp.stack([rwkv_w0, rwkv_a0, rwkv_k_k, rwkv_k_a, rwkv_r_k.reshape(depth, rw), v0_all, zero, zero],
                       axis=1)
    w2p = _pad_axis(rwkv_w2, 1, 0, LANES)
    a2p = _pad_axis(rwkv_a2, 1, DECAY_LORA, LANES)
    g2p = _pad_axis(rwkv_g2, 1, 0, 2 * LANES)
    v1p = _pad_axis(rwkv_v1, 2, 0, LANES)
    v2p = _pad_axis(rwkv_v2, 1, 0, LANES)
    seg = np.arange(SLAB) // HEAD_DIM
    ones = jnp.asarray(seg[:, None] == seg[None, :], BF16)
    gn = jnp.stack([rwkv_gn_w, rwkv_gn_b], axis=1)
    ln = jnp.stack([ln1_w, ln1_b, ln2_w, ln2_b], axis=1)

    mod = _adaln(c, w_ada, b_ada)
    xs = x[0]
    v_first = None
    for l in range(depth):
        proj, w_gu, w_dn, w_out_bf = _in_proj(xs, mod, w_pad, w_gate_up, w_down, w_out, l)
        y_attn = _attention(proj, sinks, cos_t, sin_t, l)
        vres = None if l == 0 else (v_first, v1p, v2p)
        r, lw, k_mod, v, kk, b, g, bonus = _rwkv_prep(proj, mu3, mu_lora, params, w2p, a2p, g2p, ones, vres, l)
        if l == 0:
            v_first = v
        o = _wkv(r, lw, k_mod, v, kk, b)
        xs = _out_proj(y_attn, o, bonus, g, gn, ones, w_out_bf, xs, mod, ln, l)
        xs = _ffn(xs, mod, ln, w_gu, w_dn, l)
    return xs[None]
```

```python
import functools

import numpy as np
import jax
import jax.numpy as jnp
from jax import lax
from jax.experimental import pallas as pl
from jax.experimental.pallas import tpu as pltpu

F32 = jnp.float32
BF16 = jnp.bfloat16

D_MODEL = 2048
DEPTH = 2
HEAD_DIM = 64
HALF = HEAD_DIM // 2
ATTN_WIDTH = 1024
ATTN_HEADS = 16
KV_HEADS = 4
KV_WIDTH = KV_HEADS * HEAD_DIM
ATTN_COLS = ATTN_WIDTH + 2 * KV_WIDTH
WINDOW = 128
ROPE_THETA = 10000.0
RWKV_WIDTH = 1024
DECAY_LORA = 64
AAA_LORA = 64
GATE_LORA = 160
LORA_COLS = DECAY_LORA + AAA_LORA + GATE_LORA
MV_LORA = 32
RWKV_GN_EPS = 64e-5
FFN_HIDDEN = 5632
LN_EPS = 1e-5
N_MOD = 6
DEEPNORM_ALPHA = (2.0 * DEPTH) ** 0.25

LANES = 128
N_TILES = RWKV_WIDTH // LANES
CHUNK = 64
WKV_SUB = 2
OUT_PARTS = 2
VMEM_LIMIT = 56 * 1024 * 1024

COL_Q, COL_AK, COL_AV, COL_LORA = 0, 1024, 1280, 1536
COL_R, COL_K, COL_V = 2048, 3072, 4096
N_PROJ = 5120
LORA_PAD = 512
SLAB = 256
PROJ_TN = 512

NN = ((1,), (0,))
NT = ((1,), (1,))
TN = ((0,), (0,))


def _dot(a, b, dims=NN):
    return lax.dot_general(a, b, (dims, ((), ())), preferred_element_type=F32)


def _bf(z):
    return z.astype(BF16)


def _mm(a, b):
    return _dot(_bf(a), _bf(b))


def _cumsum_rows(tri_bf16, b):
    b0 = _bf(b)
    r1 = b - b0.astype(F32)
    b1 = _bf(r1)
    b2 = _bf(r1 - b1.astype(F32))
    return _dot(tri_bf16, b0) + (_dot(tri_bf16, b1) + _dot(tri_bf16, b2))


def _head_sum(a, ones_bf16):
    a0 = _bf(a)
    a1 = _bf(a - a0.astype(F32))
    return _dot(a0, ones_bf16) + _dot(a1, ones_bf16)


def _pair_dots(lhs, rhs):
    out = []
    for p in range(0, len(lhs), 2):
        r0, r1 = _bf(rhs[p]), _bf(rhs[p + 1])
        z = jnp.zeros_like(r0)
        bd = jnp.concatenate([jnp.concatenate([r0, z], axis=1), jnp.concatenate([z, r1], axis=1)], axis=0)
        res = _dot(jnp.concatenate([_bf(lhs[p]), _bf(lhs[p + 1])], axis=1), bd)
        n = r0.shape[1]
        out += [res[:, :n], res[:, n:]]
    return out


def _iota(shape, dim):
    return lax.broadcasted_iota(jnp.int32, shape, dim)


def _layer_norm(z, w, b):
    mu = jnp.mean(z, axis=-1, keepdims=True)
    zc = z - mu
    var = jnp.mean(zc * zc, axis=-1, keepdims=True)
    return zc * lax.rsqrt(var + LN_EPS) * w + b


def _params(**kw):
    return pltpu.CompilerParams(vmem_limit_bytes=VMEM_LIMIT, **kw)


def _adaln_kernel(c_ref, w_ref, b_ref, o_ref, *, tn):
    cb = c_ref[...]
    cond = cb * jax.nn.sigmoid(cb)
    for j in range(tn // LANES):
        sl = slice(j * LANES, (j + 1) * LANES)
        o_ref[:, sl] = jnp.sum(w_ref[:, sl] * cond, axis=0, keepdims=True) + b_ref[:, sl]


def _adaln(c, w_ada, b_ada):
    depth, d, n = w_ada.shape
    tn = 1024
    c_b = jnp.broadcast_to(c.reshape(d, 1), (d, LANES))
    out = pl.pallas_call(
        functools.partial(_adaln_kernel, tn=tn),
        grid=(depth, n // tn),
        in_specs=[pl.BlockSpec((d, LANES), lambda l, j: (0, 0)),
                  pl.BlockSpec((None, d, tn), lambda l, j: (l, 0, j)),
                  pl.BlockSpec((None, 1, tn), lambda l, j: (l, 0, j))],
        out_specs=pl.BlockSpec((None, 1, tn), lambda l, j: (l, 0, j)),
        out_shape=jax.ShapeDtypeStruct((depth, 1, n), F32),
        compiler_params=_params(dimension_semantics=("parallel", "parallel")),
        name="adaln",
    )(c_b, w_ada, b_ada.reshape(depth, 1, n))
    return out.reshape(depth, N_MOD, d)


def _in_proj_kernel(x_ref, mod_ref, w_ref, o_ref, u_ref):
    @pl.when(pl.program_id(1) == 0)
    def _():
        u_ref[...] = _bf(x_ref[...] * (1.0 + mod_ref[1:2, :]) + mod_ref[0:1, :])

    o_ref[...] = _dot(u_ref[...], w_ref[...])


def _in_proj(x, mod, w_pad, l):
    t, d = x.shape
    tm, tn = 1024, PROJ_TN
    lora_out, lora_w = COL_LORA // tn, (ATTN_COLS + 3 * RWKV_WIDTH) // tn

    def w_block(j):
        return jnp.where(j < lora_out, j, jnp.where(j == lora_out, lora_w, j - 1))

    return pl.pallas_call(
        _in_proj_kernel,
        grid=(t // tm, N_PROJ // tn),
        in_specs=[pl.BlockSpec((tm, d), lambda i, j: (i, 0)),
                  pl.BlockSpec((None, N_MOD, d), lambda i, j: (l, 0, 0)),
                  pl.BlockSpec((None, d, tn), lambda i, j: (l, 0, w_block(j)))],
        out_specs=pl.BlockSpec((tm, tn), lambda i, j: (i, j)),
        out_shape=jax.ShapeDtypeStruct((t, N_PROJ), F32),
        scratch_shapes=[pltpu.VMEM((tm, d), BF16)],
        compiler_params=_params(dimension_semantics=("parallel", "arbitrary")),
        name="in_proj",
    )(x, mod, w_pad)


def _rope(t, cos, sin, lo_half):
    rot = jnp.where(lo_half, pltpu.roll(t, LANES - HALF, axis=1), pltpu.roll(t, HALF, axis=1))
    return t * cos + rot * sin


def _attn_kernel(sink_ref, q_ref, kc_ref, kp_ref, vc_ref, vp_ref, cosc_ref, sinc_ref, cosp_ref, sinp_ref,
                 o_ref, *, l):
    blk = pl.program_id(0)
    w = WINDOW
    cos_c, sin_c = cosc_ref[...], sinc_ref[...]
    cos_p, sin_p = cosp_ref[...], sinp_ref[...]

    lane = _iota((w, LANES), 1)
    lo_half = (lane % HEAD_DIM) < HALF
    in_a = lane < HEAD_DIM
    lane2 = _iota((2 * w, LANES), 1)
    in_a2 = lane2 < HEAD_DIM

    qa = _iota((2 * w, 2 * w), 0) % w
    ks = _iota((2 * w, 2 * w), 1)
    valid = (ks > qa) & (ks <= qa + w) & ((ks >= w) | (blk > 0))
    row_is_a = _iota((2 * w, 1), 0) < w
    n_tiles = ATTN_HEADS // 2
    tiles = range(n_tiles)

    ones_kv = jnp.ones((2 * w, LANES), BF16)
    k_dup, v_dup = [], []
    for m in range(KV_HEADS // 2):
        sl_kv = slice(m * LANES, (m + 1) * LANES)
        k2 = jnp.concatenate([_rope(kp_ref[:, sl_kv], cos_p, sin_p, lo_half),
                              _rope(kc_ref[:, sl_kv], cos_c, sin_c, lo_half)], axis=0)
        v2 = jnp.concatenate([vp_ref[:, sl_kv], vc_ref[:, sl_kv]], axis=0)
        k2r, v2r = pltpu.roll(k2, HEAD_DIM, axis=1), pltpu.roll(v2, HEAD_DIM, axis=1)
        k_dup += [_bf(jnp.where(in_a2, k2, k2r)), _bf(jnp.where(in_a2, k2r, k2))]
        v_dup += [jnp.concatenate([_bf(jnp.where(in_a2, v2, v2r)), ones_kv], axis=1),
                  jnp.concatenate([_bf(jnp.where(in_a2, v2r, v2)), ones_kv], axis=1)]

    s = []
    for t in tiles:
        q = _rope(q_ref[:, t * LANES:(t + 1) * LANES], cos_c, sin_c, lo_half) * (HEAD_DIM ** -0.5)
        q_st = _bf(jnp.concatenate([jnp.where(in_a, q, 0.0), jnp.where(in_a, 0.0, q)], axis=0))
        s.append(jnp.where(valid, _dot(q_st, k_dup[t // 2], NT), -1e30))
    sink = [jnp.where(row_is_a, sink_ref[l * ATTN_HEADS + 2 * t], sink_ref[l * ATTN_HEADS + 2 * t + 1])
            for t in tiles]
    mx = [jnp.maximum(jnp.max(s[t], axis=-1, keepdims=True), sink[t]) for t in tiles]
    p = [jnp.exp(s[t] - mx[t]) for t in tiles]
    for t in tiles:
        ov = _dot(_bf(p[t]), v_dup[t // 2])
        o_st = ov[:, :LANES] * (1.0 / (ov[:, LANES:] + jnp.exp(sink[t] - mx[t])))
        o_ref[:, t * LANES:(t + 1) * LANES] = _bf(jnp.where(in_a, o_st[:w], o_st[w:]))


def _attention(proj, sinks, cos_t, sin_t, l):
    t = proj.shape[0]
    w = WINDOW
    prev = lambda i, s: (jnp.maximum(i - 1, 0), 0)
    cur = lambda i, s: (i, 0)
    return pl.pallas_call(
        functools.partial(_attn_kernel, l=l),
        grid_spec=pltpu.PrefetchScalarGridSpec(
            num_scalar_prefetch=1,
            grid=(t // w,),
            in_specs=[pl.BlockSpec((w, ATTN_WIDTH), lambda i, s: (i, COL_Q // ATTN_WIDTH)),
                      pl.BlockSpec((w, KV_WIDTH), lambda i, s: (i, COL_AK // KV_WIDTH)),
                      pl.BlockSpec((w, KV_WIDTH), lambda i, s: (jnp.maximum(i - 1, 0), COL_AK // KV_WIDTH)),
                      pl.BlockSpec((w, KV_WIDTH), lambda i, s: (i, COL_AV // KV_WIDTH)),
                      pl.BlockSpec((w, KV_WIDTH), lambda i, s: (jnp.maximum(i - 1, 0), COL_AV // KV_WIDTH)),
                      pl.BlockSpec((w, LANES), cur), pl.BlockSpec((w, LANES), cur),
                      pl.BlockSpec((w, LANES), prev), pl.BlockSpec((w, LANES), prev)],
            out_specs=pl.BlockSpec((w, ATTN_WIDTH), lambda i, s: (i, 0)),
        ),
        out_shape=jax.ShapeDtypeStruct((t, ATTN_WIDTH), BF16),
        compiler_params=_params(dimension_semantics=("parallel",)),
        name="attention",
    )(sinks, proj, proj, proj, proj, proj, cos_t, sin_t, cos_t, sin_t)


def _prep_kernel(*refs, has_vres):
    if has_vres:
        (r_ref, k_ref, v_ref, l_ref, rp_ref, kp_ref, vp_ref, lp_ref, mu_ref, mul_ref, par_ref,
         w2_ref, a2_ref, g2_ref, ones_ref, vf_ref, v1_ref, v2_ref,
         ro_ref, lw_ref, ko_ref, vo_ref, kk_ref, b_ref, g_ref, bon_ref) = refs
    else:
        (r_ref, k_ref, v_ref, l_ref, rp_ref, kp_ref, vp_ref, lp_ref, mu_ref, mul_ref, par_ref,
         w2_ref, a2_ref, g2_ref, ones_ref,
         ro_ref, lw_ref, ko_ref, vo_ref, kk_ref, b_ref, g_ref, bon_ref) = refs
    first = pl.program_id(0) == 0

    def shifted_lerp(x_ref, p_ref, mu):
        x = x_ref[...]
        prev_row = jnp.where(first, 0.0, p_ref[7:8, :])
        xr = pltpu.roll(x, 1, axis=0)
        row = _iota((8, x.shape[1]), 0)
        xp = jnp.concatenate([jnp.where(row == 0, prev_row, xr[0:8]), xr[8:]], axis=0)
        return x + (xp - x) * mu

    r = shifted_lerp(r_ref, rp_ref, mu_ref[0:1, :])
    k = shifted_lerp(k_ref, kp_ref, mu_ref[1:2, :])
    v = shifted_lerp(v_ref, vp_ref, mu_ref[2:3, :])
    lo = shifted_lerp(l_ref, lp_ref, mul_ref[...])

    w0, a0 = par_ref[0:1, :], par_ref[1:2, :]
    k_k, k_a, r_k = par_ref[2:3, :], par_ref[3:4, :], par_ref[4:5, :]

    wa = lo[:, 0:LANES]
    gd = lo[:, LANES:3 * LANES]
    log_decay = -float(np.exp(-0.5)) * jax.nn.sigmoid(w0 + _mm(jnp.tanh(wa), w2_ref[...]))
    a = jax.nn.sigmoid(a0 + _mm(wa, a2_ref[...]))
    g = _mm(jax.nn.sigmoid(gd), g2_ref[...])
    if has_vres:
        v0 = par_ref[5:6, :]
        mix = jax.nn.sigmoid(v0 + _mm(_mm(v, v1_ref[...]), v2_ref[...]))
        v = v + (vf_ref[...].astype(F32) - v) * mix

    kk = k * k_k
    k_mod = k * (1.0 + (a - 1.0) * k_a)
    rk = r * k_mod * r_k
    ones = ones_ref[...]
    for j in range(RWKV_WIDTH // SLAB):
        sl = slice(j * SLAB, (j + 1) * SLAB)
        kk_j = kk[:, sl]
        ss = _mm(kk_j * kk_j, ones)
        kk_n = kk_j * lax.rsqrt(jnp.maximum(ss, 1e-24))
        kk_ref[:, sl] = _bf(kk_n)
        b_ref[:, sl] = _bf(kk_n * a[:, sl])
        bon_ref[:, sl] = _bf(_mm(rk[:, sl], ones) * v[:, sl])
    ro_ref[...] = _bf(r)
    lw_ref[...] = log_decay
    ko_ref[...] = _bf(k_mod)
    vo_ref[...] = _bf(v)
    g_ref[...] = _bf(g)


def _rwkv_prep(proj, mu3, mu_lora, params, w2p, a2p, g2p, ones, vres, l):
    t = proj.shape[0]
    tm = 256
    wdt = RWKV_WIDTH
    has_vres = vres is not None

    def col(cb, width):
        return pl.BlockSpec((tm, width), lambda i: (i, cb))

    def prev(cb, width):
        return pl.BlockSpec((8, width), lambda i: (jnp.maximum(i * (tm // 8) - 1, 0), cb))

    def layer(a, lyr=l):
        return pl.BlockSpec((None,) + a.shape[1:], lambda i: (lyr, 0, 0))

    in_specs = [col(COL_R // wdt, wdt), col(COL_K // wdt, wdt), col(COL_V // wdt, wdt),
                col(COL_LORA // LORA_PAD, LORA_PAD),
                prev(COL_R // wdt, wdt), prev(COL_K // wdt, wdt), prev(COL_V // wdt, wdt),
                prev(COL_LORA // LORA_PAD, LORA_PAD),
                layer(mu3), layer(mu_lora), layer(params), layer(w2p), layer(a2p), layer(g2p),
                pl.BlockSpec(ones.shape, lambda i: (0, 0))]
    args = [proj] * 8 + [mu3, mu_lora, params, w2p, a2p, g2p, ones]
    if has_vres:
        v_first, v1p, v2p = vres
        in_specs += [pl.BlockSpec((tm, wdt), lambda i: (i, 0)), layer(v1p, l - 1), layer(v2p, l - 1)]
        args += [v_first, v1p, v2p]
    out_spec = pl.BlockSpec((tm, wdt), lambda i: (i, 0))
    out_dtypes = [BF16, F32, BF16, BF16, BF16, BF16, BF16, BF16]
    return pl.pallas_call(
        functools.partial(_prep_kernel, has_vres=has_vres),
        grid=(t // tm,),
        in_specs=in_specs,
        out_specs=[out_spec] * 8,
        out_shape=[jax.ShapeDtypeStruct((t, wdt), dt) for dt in out_dtypes],
        compiler_params=_params(dimension_semantics=("parallel",)),
        name="rwkv_prep",
    )(*args)


def _wkv_kernel(r_ref, lw_ref, k_ref, v_ref, kk_ref, b_ref, wgu_ref, wdn_ref, wout_ref,
                o_ref, wgu_o, wdn_o, wout_o, st_ref):
    c = CHUNK

    @pl.when(pl.program_id(0) == 0)
    def _():
        st_ref[...] = jnp.zeros_like(st_ref)

    wgu_o[...] = _bf(wgu_ref[...])
    wdn_o[...] = _bf(wdn_ref[...])
    wout_o[...] = _bf(wout_ref[...])

    row_t = _iota((c, c), 0)
    col_t = _iota((c, c), 1)
    l_incl = _bf(jnp.where(row_t >= col_t, 1.0, 0.0))

    lane = _iota((c, LANES), 1)
    m_a = lane < HEAD_DIM
    r2 = _iota((2 * c, 2 * c), 0)
    c2 = _iota((2 * c, 2 * c), 1)
    strict = r2 > c2
    incl = r2 >= c2
    eye = r2 == c2
    anti = (r2 < c) != (c2 < c)

    def stack(x):
        return jnp.concatenate([jnp.where(m_a, x, 0.0), jnp.where(m_a, 0.0, x)], axis=0)

    def stack_sw(x):
        return jnp.concatenate([jnp.where(m_a, 0.0, x), jnp.where(m_a, x, 0.0)], axis=0)

    tiles = range(N_TILES)
    sls = [slice(j * LANES, (j + 1) * LANES) for j in tiles]

    incl2 = jnp.concatenate([incl, incl], axis=1)
    blk = lambda s: (r2 // s) == (c2 // s)
    diag8 = strict & blk(8)

    def local_part(s, out):
        rows = slice(s * c, (s + 1) * c)
        cum_all = _cumsum_rows(l_incl, lw_ref[rows, :])
        a_st, r_st, rhs, v_sw, v_st, bk_tail, w_c = [], [], [], [], [], [], []
        for j in tiles:
            sl = sls[j]
            lw = lw_ref[rows, sl]
            cum = cum_all[:, sl]
            cum_last = cum[c - 1:c, :]
            e_neg = jnp.exp(-cum)
            e_tail = jnp.exp(cum_last - cum)
            k = k_ref[rows, sl].astype(F32)
            b = b_ref[rows, sl].astype(F32)
            a_st.append(stack(-kk_ref[rows, sl].astype(F32) * jnp.exp(cum - lw)))
            r_st.append(_bf(stack(r_ref[rows, sl].astype(F32) * jnp.exp(cum))))
            rhs.append(_bf(jnp.concatenate([stack(b * e_neg), stack(k * e_neg)], axis=0)))
            vs = pltpu.roll(v_ref[rows, sl].astype(F32), 64, axis=1)
            v_sw.append(_bf(vs))
            v_st.append(_bf(stack_sw(vs)))
            bk_tail.append(_bf(jnp.concatenate([b * e_tail, k * e_tail], axis=0)))
            w_c.append(jnp.exp(cum_last))
        yield
        d_ab, d_ak, d_r = [], [], []
        for j in tiles:
            gm = _dot(jnp.concatenate([_bf(a_st[j]), r_st[j]], axis=0), rhs[j], NT)
            d_ab.append(gm[:2 * c, :2 * c])
            d_ak.append(_bf(jnp.where(strict, gm[:2 * c, 2 * c:], 0.0)))
            d_r.append(_bf(jnp.where(incl2, gm[2 * c:, :], 0.0)))
        yield
        av = _pair_dots(d_ak, v_st)
        y0 = [a_st[j] + av[j] for j in tiles]
        d1 = [jnp.where(diag8, d_ab[j], 0.0) for j in tiles]
        p1 = [jnp.where(eye, 1.0, d1[j]) for j in tiles]
        d2 = _pair_dots(d1, d1)
        yield
        res = [_mm(d2[j], jnp.concatenate([p1[j], d2[j]], axis=1)) for j in tiles]
        p2 = [p1[j] + res[j][:, :2 * c] for j in tiles]
        yield
        d4p2 = _pair_dots([res[j][:, 2 * c:] for j in tiles], p2)
        tm = [p2[j] + d4p2[j] for j in tiles]
        for bs in (8, 16, 32):
            yield
            off = strict & blk(2 * bs) & jnp.logical_not(blk(bs))
            xm = _pair_dots(tm, [jnp.where(off, d_ab[j], 0.0) for j in tiles])
            yield
            xt = _pair_dots(xm, tm)
            tm = [tm[j] + xt[j] for j in tiles]
        yield
        out.update(y=_pair_dots(tm, y0), r_st=r_st, d_r=d_r, v_st=v_st, v_sw=v_sw, bk_tail=bk_tail, w_c=w_c)

    def state_part(s, loc, st):
        rows = slice(s * c, (s + 1) * c)
        ra, u_loc = [], []
        for j in tiles:
            yj = loc["y"][j]
            ah_st = jnp.concatenate([jnp.where(m_a, yj[:c], 0.0), jnp.where(m_a, 0.0, yj[c:])], axis=0)
            u_loc.append(jnp.concatenate([jnp.where(m_a, 0.0, yj[:c]), jnp.where(m_a, yj[c:], 0.0)], axis=0))
            ra.append(jnp.concatenate([loc["r_st"][j], _bf(ah_st)], axis=0))
        rs_as = _pair_dots(ra, st)
        u_st = [rs_as[j][2 * c:] + u_loc[j] for j in tiles]
        for j in tiles:
            o_st = rs_as[j][:2 * c] + _dot(loc["d_r"][j], jnp.concatenate([_bf(u_st[j]), loc["v_st"][j]], axis=0))
            o_ref[rows, sls[j]] = pltpu.roll(o_st[:c] + o_st[c:], 64, axis=1)
        new_st = []
        for j in tiles:
            u_sw = u_st[j][:c] + u_st[j][c:]
            delta = _dot(loc["bk_tail"][j], jnp.concatenate([_bf(u_sw), loc["v_sw"][j]], axis=0), TN)
            w_col = jnp.sum(jnp.where(eye, jnp.broadcast_to(loc["w_c"][j], (2 * c, 2 * c)), 0.0),
                            axis=1, keepdims=True)
            new_st.append(w_col * st[j] + jnp.where(anti, delta, 0.0))
        return new_st

    locs = [dict() for _ in range(WKV_SUB)]
    live = [local_part(s, locs[s]) for s in range(WKV_SUB)]
    while live:
        live = [g for g in live if next(g, "done") != "done"]
    st = [st_ref[j] for j in tiles]
    for s in range(WKV_SUB):
        st = state_part(s, locs[s], st)
    for j in tiles:
        st_ref[j] = st[j]


def _wkv(r, lw, k, v, kk, b, w_gate_up, w_down, w_out, l):
    t, wdt = r.shape
    rows = WKV_SUB * CHUNK
    steps = t // rows
    spec = pl.BlockSpec((rows, wdt), lambda i: (i, 0))

    def cast_specs(w, n_blocks):
        assert n_blocks <= steps and w.shape[1] % (16 * n_blocks) == 0
        blk = w.shape[1] // n_blocks
        return (pl.BlockSpec((None, blk, w.shape[2]), lambda i: (l, jnp.minimum(i, n_blocks - 1), 0)),
                pl.BlockSpec((blk, w.shape[2]), lambda i: (jnp.minimum(i, n_blocks - 1), 0)),
                jax.ShapeDtypeStruct(w.shape[1:], BF16))

    casts = [cast_specs(w_gate_up, 64), cast_specs(w_down, 32), cast_specs(w_out, 64)]
    return pl.pallas_call(
        _wkv_kernel,
        grid=(steps,),
        in_specs=[spec] * 6 + [c[0] for c in casts],
        out_specs=[spec] + [c[1] for c in casts],
        out_shape=[jax.ShapeDtypeStruct((t, wdt), F32)] + [c[2] for c in casts],
        scratch_shapes=[pltpu.VMEM((N_TILES, LANES, LANES), F32)],
        compiler_params=_params(dimension_semantics=("arbitrary",)),
        name="wkv",
    )(r, lw, k, v, kk, b, w_gate_up, w_down, w_out)


def _out_proj_kernel(ya_ref, o_ref, bon_ref, g_ref, gn_ref, ones_ref, w_ref, x_ref, mod_ref, ln_ref, out_ref,
                     *, parts):
    ones = ones_ref[...]
    pm = out_ref.shape[0] // parts
    slabs = [slice(j * SLAB, (j + 1) * SLAB) for j in range(RWKV_WIDTH // SLAB)]
    part_rows = [slice(p * pm, (p + 1) * pm) for p in range(parts)]
    oc = [[None] * len(slabs) for _ in range(parts)]
    var = [[None] * len(slabs) for _ in range(parts)]
    for p, rows in enumerate(part_rows):
        for j, sl in enumerate(slabs):
            o = o_ref[rows, sl]
            oc[p][j] = o - _head_sum(o, ones) * (1.0 / HEAD_DIM)
    for p in range(parts):
        for j in range(len(slabs)):
            var[p][j] = _mm(oc[p][j] * oc[p][j], ones) * (1.0 / HEAD_DIM)
    for p, rows in enumerate(part_rows):
        ys = []
        for j, sl in enumerate(slabs):
            y = ((oc[p][j] * lax.rsqrt(var[p][j] + RWKV_GN_EPS) * gn_ref[0:1, sl] + gn_ref[1:2, sl]
                  + bon_ref[rows, sl].astype(F32)) * g_ref[rows, sl].astype(F32))
            ys.append(_bf(y))
        mix = (_dot(ya_ref[rows, :], w_ref[:ATTN_WIDTH, :])
               + _dot(jnp.concatenate(ys, axis=1), w_ref[ATTN_WIDTH:, :]))
        z = DEEPNORM_ALPHA * x_ref[rows, :] + mod_ref[2:3, :] * mix
        out_ref[rows, :] = _layer_norm(z, ln_ref[0:1, :], ln_ref[1:2, :])


def _out_proj(y_attn, o, bonus, g, gn, ones, w_cat, x, mod, ln, l):
    t, d = x.shape
    tm = 512
    wdt = RWKV_WIDTH
    row = lambda width: pl.BlockSpec((tm, width), lambda i: (i, 0))
    layer = lambda a: pl.BlockSpec((None,) + a.shape[1:], lambda i: (l, 0, 0))
    return pl.pallas_call(
        functools.partial(_out_proj_kernel, parts=OUT_PARTS),
        grid=(t // tm,),
        in_specs=[row(ATTN_WIDTH), row(wdt), row(wdt), row(wdt), layer(gn),
                  pl.BlockSpec(ones.shape, lambda i: (0, 0)), pl.BlockSpec(w_cat.shape, lambda i: (0, 0)), row(d),
                  layer(mod), layer(ln)],
        out_specs=row(d),
        out_shape=jax.ShapeDtypeStruct((t, d), F32),
        compiler_params=_params(dimension_semantics=("parallel",)),
        name="out_proj",
    )(y_attn, o, bonus, g, gn, ones, w_cat, x, mod, ln)


def _ffn_kernel(x_ref, mod_ref, ln_ref, wg_ref, wu_ref, wd_ref, out_ref, u_ref, acc_ref):
    j = pl.program_id(1)

    @pl.when(j == 0)
    def _():
        u_ref[...] = _bf(x_ref[...] * (1.0 + mod_ref[4:5, :]) + mod_ref[3:4, :])
        acc_ref[...] = jnp.zeros_like(acc_ref)

    u = u_ref[...]
    gate = _dot(u, wg_ref[...])
    up = _dot(u, wu_ref[...])
    h = _bf(gate * jax.nn.sigmoid(gate) * up)
    acc_ref[...] += _dot(h, wd_ref[...])

    @pl.when(j == pl.num_programs(1) - 1)
    def _():
        z = DEEPNORM_ALPHA * x_ref[...] + mod_ref[5:6, :] * acc_ref[...]
        out_ref[...] = _layer_norm(z, ln_ref[2:3, :], ln_ref[3:4, :])


def _ffn(x, mod, ln, w_gu, w_dn, l):
    t, d = x.shape
    f = w_dn.shape[0]
    tm, tf = 512, 512
    nf = f // tf
    return pl.pallas_call(
        _ffn_kernel,
        grid=(t // tm, nf),
        in_specs=[pl.BlockSpec((tm, d), lambda i, j: (i, 0)),
                  pl.BlockSpec((None,) + mod.shape[1:], lambda i, j: (l, 0, 0)),
                  pl.BlockSpec((None,) + ln.shape[1:], lambda i, j: (l, 0, 0)),
                  pl.BlockSpec((d, tf), lambda i, j: (0, j)),
                  pl.BlockSpec((d, tf), lambda i, j: (0, j + nf)),
                  pl.BlockSpec((tf, d), lambda i, j: (j, 0))],
        out_specs=pl.BlockSpec((tm, d), lambda i, j: (i, 0)),
        out_shape=jax.ShapeDtypeStruct((t, d), F32),
        scratch_shapes=[pltpu.VMEM((tm, d), BF16), pltpu.VMEM((tm, d), F32)],
        compiler_params=_params(dimension_semantics=("parallel", "arbitrary")),
        name="ffn",
    )(x, mod, ln, w_gu, w_gu, w_dn)


def _pad_axis(a, axis, before, total):
    pads = [(0, 0)] * a.ndim
    pads[axis] = (before, total - before - a.shape[axis])
    return jnp.pad(a, pads)


def kernel(x, c, positions, w_ada, b_ada, w_in, attn_sinks, rwkv_mu, rwkv_w0, rwkv_w2, rwkv_a0, rwkv_a2, rwkv_g2, rwkv_k_k, rwkv_k_a, rwkv_r_k, rwkv_gn_w, rwkv_gn_b, rwkv_v0, rwkv_v1, rwkv_v2, w_out, ln1_w, ln1_b, w_gate_up, w_down, ln2_w, ln2_b):
    bsz, t, d = x.shape
    assert bsz == 1 and d == D_MODEL
    depth = w_in.shape[0]
    rw = RWKV_WIDTH
    inv =ROPE_THETA ** (-jnp.arange(0, HEAD_DIM, 2, dtype=F32) / HEAD_DIM)
    ang = positions[0].astype(F32)[:, None] * inv
    cos, sin = jnp.cos(ang), jnp.sin(ang)
    cos_t = jnp.concatenate([cos, cos, cos, cos], axis=-1)
    sin_t = jnp.concatenate([-sin, sin, -sin, sin], axis=-1)

    w_pad = _pad_axis(w_in, 2, 0, N_PROJ).astype(BF16)
    sinks = attn_sinks.reshape(-1)
    mu3 = rwkv_mu[:, :3 * rw].reshape(depth, 3, rw)
    mu_lora = _pad_axis(rwkv_mu[:, 3 * rw:], 1, 0, LORA_PAD).reshape(depth, 1, LORA_PAD)
    zero = jnp.zeros((depth, rw), F32)
    v0_all = _pad_axis(rwkv_v0, 0, 1, depth)
    params = jnp.stack([rwkv_w0, rwkv_a0, rwkv_k_k, rwkv_k_a, rwkv_r_k.reshape(depth, rw), v0_all, zero, zero],
                       axis=1)
    w2p = _pad_axis(rwkv_w2, 1, 0, LANES)
    a2p = _pad_axis(rwkv_a2, 1, DECAY_LORA, LANES)
    g2p = _pad_axis(rwkv_g2, 1, 0, 2 * LANES)
    v1p = _pad_axis(rwkv_v1, 2, 0, LANES)
    v2p = _pad_axis(rwkv_v2, 1, 0, LANES)
    seg = np.arange(SLAB) // HEAD_DIM
    ones = jnp.asarray(seg[:, None] == seg[None, :], BF16)
    gn = jnp.stack([rwkv_gn_w, rwkv_gn_b], axis=1)
    ln = jnp.stack([ln1_w, ln1_b, ln2_w, ln2_b], axis=1)

    mod = _adaln(c, w_ada, b_ada)
    xs = x[0]
    v_first = None
    for l in range(depth):
        proj = _in_proj(xs, mod, w_pad, l)
        y_attn = _attention(proj, sinks, cos_t, sin_t, l)
        vres = None if l == 0 else (v_first, v1p, v2p)
        r, lw, k_mod, v, kk, b, g, bonus = _rwkv_prep(proj, mu3, mu_lora, params, w2p, a2p, g2p, ones, vres, l)
        if l == 0:
            v_first = v
        o, w_gu, w_dn, w_out_bf = _wkv(r, lw, k_mod, v, kk, b, w_gate_up, w_down, w_out, l)
        xs = _out_proj(y_attn, o, bonus, g, gn, ones, w_out_bf, xs, mod, ln, l)
        xs = _ffn(xs, mod, ln, w_gu, w_dn, l)
    return xs[None]
```

```python
import functools

import numpy as np
import jax
import jax.numpy as jnp
from jax import lax
from jax.experimental import pallas as pl
from jax.experimental.pallas import tpu as pltpu

F32 = jnp.float32
BF16 = jnp.bfloat16

D_MODEL = 2048
DEPTH = 2
HEAD_DIM = 64
HALF = HEAD_DIM // 2
ATTN_WIDTH = 1024
ATTN_HEADS = 16
KV_HEADS = 4
KV_WIDTH = KV_HEADS * HEAD_DIM
ATTN_COLS = ATTN_WIDTH + 2 * KV_WIDTH
WINDOW = 128
ROPE_THETA = 10000.0
RWKV_WIDTH = 1024
DECAY_LORA = 64
AAA_LORA = 64
GATE_LORA = 160
LORA_COLS = DECAY_LORA + AAA_LORA + GATE_LORA
MV_LORA = 32
RWKV_GN_EPS = 64e-5
FFN_HIDDEN = 5632
LN_EPS = 1e-5
N_MOD = 6
DEEPNORM_ALPHA = (2.0 * DEPTH) ** 0.25

LANES = 128
N_TILES = RWKV_WIDTH // LANES
CHUNK = 64
WKV_SUB = 2
OUT_PARTS = 2
FFN_LN_ROWS = 64
VMEM_LIMIT = 56 * 1024 * 1024

COL_Q, COL_AK, COL_AV, COL_LORA = 0, 1024, 1280, 1536
COL_R, COL_K, COL_V = 2048, 3072, 4096
N_PROJ = 5120
LORA_PAD = 512
SLAB = 256
PROJ_TN = 512

NN = ((1,), (0,))
NT = ((1,), (1,))
TN = ((0,), (0,))


def _dot(a, b, dims=NN):
    return lax.dot_general(a, b, (dims, ((), ())), preferred_element_type=F32)


def _bf(z):
    return z.astype(BF16)


def _mm(a, b):
    return _dot(_bf(a), _bf(b))


def _cumsum_rows(tri_bf16, b):
    b0 = _bf(b)
    r1 = b - b0.astype(F32)
    b1 = _bf(r1)
    b2 = _bf(r1 - b1.astype(F32))
    return _dot(tri_bf16, b0) + (_dot(tri_bf16, b1) + _dot(tri_bf16, b2))


def _head_sum(a, ones_bf16):
    a0 = _bf(a)
    a1 = _bf(a - a0.astype(F32))
    return _dot(a0, ones_bf16) + _dot(a1, ones_bf16)


def _pair_dots(lhs, rhs):
    out = []
    for p in range(0, len(lhs), 2):
        r0, r1 = _bf(rhs[p]), _bf(rhs[p + 1])
        z = jnp.zeros_like(r0)
        bd = jnp.concatenate([jnp.concatenate([r0, z], axis=1), jnp.concatenate([z, r1], axis=1)], axis=0)
        res = _dot(jnp.concatenate([_bf(lhs[p]), _bf(lhs[p + 1])], axis=1), bd)
        n = r0.shape[1]
        out += [res[:, :n], res[:, n:]]
    return out


def _iota(shape, dim):
    return lax.broadcasted_iota(jnp.int32, shape, dim)


def _layer_norm(z, w, b):
    mu = jnp.mean(z, axis=-1, keepdims=True)
    zc = z - mu
    var = jnp.mean(zc * zc, axis=-1, keepdims=True)
    return zc * lax.rsqrt(var + LN_EPS) * w + b


def _params(**kw):
    return pltpu.CompilerParams(vmem_limit_bytes=VMEM_LIMIT, **kw)


def _adaln_kernel(c_ref, w_ref, b_ref, o_ref, *, tn):
    cb = c_ref[...]
    cond = cb * jax.nn.sigmoid(cb)
    for j in range(tn // LANES):
        sl = slice(j * LANES, (j + 1) * LANES)
        o_ref[:, sl] = jnp.sum(w_ref[:, sl] * cond, axis=0, keepdims=True) + b_ref[:, sl]


def _adaln(c, w_ada, b_ada):
    depth, d, n = w_ada.shape
    tn = 1024
    c_b = jnp.broadcast_to(c.reshape(d, 1), (d, LANES))
    out = pl.pallas_call(
        functools.partial(_adaln_kernel, tn=tn),
        grid=(depth, n // tn),
        in_specs=[pl.BlockSpec((d, LANES), lambda l, j: (0, 0)),
                  pl.BlockSpec((None, d, tn), lambda l, j: (l, 0, j)),
                  pl.BlockSpec((None, 1, tn), lambda l, j: (l, 0, j))],
        out_specs=pl.BlockSpec((None, 1, tn), lambda l, j: (l, 0, j)),
        out_shape=jax.ShapeDtypeStruct((depth, 1, n), F32),
        compiler_params=_params(dimension_semantics=("parallel", "parallel")),
        name="adaln",
    )(c_b, w_ada, b_ada.reshape(depth, 1, n))
    return out.reshape(depth, N_MOD, d)


def _in_proj_kernel(x_ref, mod_ref, w_ref, o_ref, u_ref):
    @pl.when(pl.program_id(1) == 0)
    def _():
        u_ref[...] = _bf(x_ref[...] * (1.0 + mod_ref[1:2, :]) + mod_ref[0:1, :])

    o_ref[...] = _dot(u_ref[...], w_ref[...])


def _in_proj(x, mod, w_pad, l):
    t, d = x.shape
    tm, tn = 1024, PROJ_TN
    lora_out, lora_w = COL_LORA // tn, (ATTN_COLS + 3 * RWKV_WIDTH) // tn

    def w_block(j):
        return jnp.where(j < lora_out, j, jnp.where(j == lora_out, lora_w, j - 1))

    return pl.pallas_call(
        _in_proj_kernel,
        grid=(t // tm, N_PROJ // tn),
        in_specs=[pl.BlockSpec((tm, d), lambda i, j: (i, 0)),
                  pl.BlockSpec((None, N_MOD, d), lambda i, j: (l, 0, 0)),
                  pl.BlockSpec((None, d, tn), lambda i, j: (l, 0, w_block(j)))],
        out_specs=pl.BlockSpec((tm, tn), lambda i, j: (i, j)),
        out_shape=jax.ShapeDtypeStruct((t, N_PROJ), F32),
        scratch_shapes=[pltpu.VMEM((tm, d), BF16)],
        compiler_params=_params(dimension_semantics=("parallel", "arbitrary")),
        name="in_proj",
    )(x, mod, w_pad)


def _rope(t, cos, sin, lo_half):
    rot = jnp.where(lo_half, pltpu.roll(t, LANES - HALF, axis=1), pltpu.roll(t, HALF, axis=1))
    return t * cos + rot * sin


def _attn_kernel(sink_ref, q_ref, kc_ref, kp_ref, vc_ref, vp_ref, cosc_ref, sinc_ref, cosp_ref, sinp_ref,
                 o_ref, *, l):
    blk = pl.program_id(0)
    w = WINDOW
    cos_c, sin_c = cosc_ref[...], sinc_ref[...]
    cos_p, sin_p = cosp_ref[...], sinp_ref[...]

    lane = _iota((w, LANES), 1)
    lo_half = (lane % HEAD_DIM) < HALF
    in_a = lane < HEAD_DIM
    lane2 = _iota((2 * w, LANES), 1)
    in_a2 = lane2 < HEAD_DIM

    qa = _iota((2 * w, 2 * w), 0) % w
    ks = _iota((2 * w, 2 * w), 1)
    valid = (ks > qa) & (ks <= qa + w) & ((ks >= w) | (blk > 0))
    row_is_a = _iota((2 * w, 1), 0) < w
    n_tiles = ATTN_HEADS // 2
    tiles = range(n_tiles)

    ones_kv = jnp.ones((2 * w, LANES), BF16)
    k_dup, v_dup = [], []
    for m in range(KV_HEADS // 2):
        sl_kv = slice(m * LANES, (m + 1) * LANES)
        k2 = jnp.concatenate([_rope(kp_ref[:, sl_kv], cos_p, sin_p, lo_half),
                              _rope(kc_ref[:, sl_kv], cos_c, sin_c, lo_half)], axis=0)
        v2 = jnp.concatenate([vp_ref[:, sl_kv], vc_ref[:, sl_kv]], axis=0)
        k2r, v2r = pltpu.roll(k2, HEAD_DIM, axis=1), pltpu.roll(v2, HEAD_DIM, axis=1)
        k_dup += [_bf(jnp.where(in_a2, k2, k2r)), _bf(jnp.where(in_a2, k2r, k2))]
        v_dup += [jnp.concatenate([_bf(jnp.where(in_a2, v2, v2r)), ones_kv], axis=1),
                  jnp.concatenate([_bf(jnp.where(in_a2, v2r, v2)), ones_kv], axis=1)]

    s = []
    for t in tiles:
        q = _rope(q_ref[:, t * LANES:(t + 1) * LANES], cos_c, sin_c, lo_half) * (HEAD_DIM ** -0.5)
        q_st = _bf(jnp.concatenate([jnp.where(in_a, q, 0.0), jnp.where(in_a, 0.0, q)], axis=0))
        s.append(jnp.where(valid, _dot(q_st, k_dup[t // 2], NT), -1e30))
    sink = [jnp.where(row_is_a, sink_ref[l * ATTN_HEADS + 2 * t], sink_ref[l * ATTN_HEADS + 2 * t + 1])
            for t in tiles]
    mx = [jnp.maximum(jnp.max(s[t], axis=-1, keepdims=True), sink[t]) for t in tiles]
    p = [jnp.exp(s[t] - mx[t]) for t in tiles]
    for t in tiles:
        ov = _dot(_bf(p[t]), v_dup[t // 2])
        o_st = ov[:, :LANES] * (1.0 / (ov[:, LANES:] + jnp.exp(sink[t] - mx[t])))
        o_ref[:, t * LANES:(t + 1) * LANES] = _bf(jnp.where(in_a, o_st[:w], o_st[w:]))


def _attention(proj, sinks, cos_t, sin_t, l):
    t = proj.shape[0]
    w = WINDOW
    prev = lambda i, s: (jnp.maximum(i - 1, 0), 0)
    cur = lambda i, s: (i, 0)
    return pl.pallas_call(
        functools.partial(_attn_kernel, l=l),
        grid_spec=pltpu.PrefetchScalarGridSpec(
            num_scalar_prefetch=1,
            grid=(t // w,),
            in_specs=[pl.BlockSpec((w, ATTN_WIDTH), lambda i, s: (i, COL_Q // ATTN_WIDTH)),
                      pl.BlockSpec((w, KV_WIDTH), lambda i, s: (i, COL_AK // KV_WIDTH)),
                      pl.BlockSpec((w, KV_WIDTH), lambda i, s: (jnp.maximum(i - 1, 0), COL_AK // KV_WIDTH)),
                      pl.BlockSpec((w, KV_WIDTH), lambda i, s: (i, COL_AV // KV_WIDTH)),
                      pl.BlockSpec((w, KV_WIDTH), lambda i, s: (jnp.maximum(i - 1, 0), COL_AV // KV_WIDTH)),
                      pl.BlockSpec((w, LANES), cur), pl.BlockSpec((w, LANES), cur),
                      pl.BlockSpec((w, LANES), prev), pl.BlockSpec((w, LANES), prev)],
            out_specs=pl.BlockSpec((w, ATTN_WIDTH), lambda i, s: (i, 0)),
        ),
        out_shape=jax.ShapeDtypeStruct((t, ATTN_WIDTH), BF16),
        compiler_params=_params(dimension_semantics=("parallel",)),
        name="attention",
    )(sinks, proj, proj, proj, proj, proj, cos_t, sin_t, cos_t, sin_t)


def _prep_kernel(*refs, has_vres):
    if has_vres:
        (r_ref, k_ref, v_ref, l_ref, rp_ref, kp_ref, vp_ref, lp_ref, mu_ref, mul_ref, par_ref,
         w2_ref, a2_ref, g2_ref, ones_ref, vf_ref, v1_ref, v2_ref,
         ro_ref, lw_ref, ko_ref, vo_ref, kk_ref, b_ref, g_ref, bon_ref) = refs
    else:
        (r_ref, k_ref, v_ref, l_ref, rp_ref, kp_ref, vp_ref, lp_ref, mu_ref, mul_ref, par_ref,
         w2_ref, a2_ref, g2_ref, ones_ref,
         ro_ref, lw_ref, ko_ref, vo_ref, kk_ref, b_ref, g_ref, bon_ref) = refs
    first = pl.program_id(0) == 0

    def shifted_lerp(x_ref, p_ref, mu):
        x = x_ref[...]
        prev_row = jnp.where(first, 0.0, p_ref[7:8, :])
        xr = pltpu.roll(x, 1, axis=0)
        row = _iota((8, x.shape[1]), 0)
        xp = jnp.concatenate([jnp.where(row == 0, prev_row, xr[0:8]), xr[8:]], axis=0)
        return x + (xp - x) * mu

    r = shifted_lerp(r_ref, rp_ref, mu_ref[0:1, :])
    k = shifted_lerp(k_ref, kp_ref, mu_ref[1:2, :])
    v = shifted_lerp(v_ref, vp_ref, mu_ref[2:3, :])
    lo = shifted_lerp(l_ref, lp_ref, mul_ref[...])

    w0, a0 = par_ref[0:1, :], par_ref[1:2, :]
    k_k, k_a, r_k = par_ref[2:3, :], par_ref[3:4, :], par_ref[4:5, :]

    wa = lo[:, 0:LANES]
    gd = lo[:, LANES:3 * LANES]
    log_decay = -float(np.exp(-0.5)) * jax.nn.sigmoid(w0 + _mm(jnp.tanh(wa), w2_ref[...]))
    a = jax.nn.sigmoid(a0 + _mm(wa, a2_ref[...]))
    g = _mm(jax.nn.sigmoid(gd), g2_ref[...])
    if has_vres:
        v0 = par_ref[5:6, :]
        mix = jax.nn.sigmoid(v0 + _mm(_mm(v, v1_ref[...]), v2_ref[...]))
        v = v + (vf_ref[...].astype(F32) - v) * mix

    kk = k * k_k
    k_mod = k * (1.0 + (a - 1.0) * k_a)
    rk = r * k_mod * r_k
    ones = ones_ref[...]
    for j in range(RWKV_WIDTH // SLAB):
        sl = slice(j * SLAB, (j + 1) * SLAB)
        kk_j = kk[:, sl]
        ss = _mm(kk_j * kk_j, ones)
        kk_n = kk_j * lax.rsqrt(jnp.maximum(ss, 1e-24))
        kk_ref[:, sl] = _bf(kk_n)
        b_ref[:, sl] = _bf(kk_n * a[:, sl])
        bon_ref[:, sl] = _bf(_mm(rk[:, sl], ones) * v[:, sl])
    ro_ref[...] = _bf(r)
    lw_ref[...] = log_decay
    ko_ref[...] = _bf(k_mod)
    vo_ref[...] = _bf(v)
    g_ref[...] = _bf(g)


def _rwkv_prep(proj, mu3, mu_lora, params, w2p, a2p, g2p, ones, vres, l):
    t = proj.shape[0]
    tm = 256
    wdt = RWKV_WIDTH
    has_vres = vres is not None

    def col(cb, width):
        return pl.BlockSpec((tm, width), lambda i: (i, cb))

    def prev(cb, width):
        return pl.BlockSpec((8, width), lambda i: (jnp.maximum(i * (tm // 8) - 1, 0), cb))

    def layer(a, lyr=l):
        return pl.BlockSpec((None,) + a.shape[1:], lambda i: (lyr, 0, 0))

    in_specs = [col(COL_R // wdt, wdt), col(COL_K // wdt, wdt), col(COL_V // wdt, wdt),
                col(COL_LORA // LORA_PAD, LORA_PAD),
                prev(COL_R // wdt, wdt), prev(COL_K // wdt, wdt), prev(COL_V // wdt, wdt),
                prev(COL_LORA // LORA_PAD, LORA_PAD),
                layer(mu3), layer(mu_lora), layer(params), layer(w2p), layer(a2p), layer(g2p),
                pl.BlockSpec(ones.shape, lambda i: (0, 0))]
    args = [proj] * 8 + [mu3, mu_lora, params, w2p, a2p, g2p, ones]
    if has_vres:
        v_first, v1p, v2p = vres
        in_specs += [pl.BlockSpec((tm, wdt), lambda i: (i, 0)), layer(v1p, l - 1), layer(v2p, l - 1)]
        args += [v_first, v1p, v2p]
    out_spec = pl.BlockSpec((tm, wdt), lambda i: (i, 0))
    out_dtypes = [BF16, F32, BF16, BF16, BF16, BF16, BF16, BF16]
    return pl.pallas_call(
        functools.partial(_prep_kernel, has_vres=has_vres),
        grid=(t // tm,),
        in_specs=in_specs,
        out_specs=[out_spec] * 8,
        out_shape=[jax.ShapeDtypeStruct((t, wdt), dt) for dt in out_dtypes],
        compiler_params=_params(dimension_semantics=("parallel",)),
        name="rwkv_prep",
    )(*args)


def _wkv_kernel(r_ref, lw_ref, k_ref, v_ref, kk_ref, b_ref, wgu_ref, wdn_ref, wout_ref,
                o_ref, wgu_o, wdn_o, wout_o, st_ref):
    c = CHUNK

    @pl.when(pl.program_id(0) == 0)
    def _():
        st_ref[...] = jnp.zeros_like(st_ref)

    wgu_o[...] = _bf(wgu_ref[...])
    wdn_o[...] = _bf(wdn_ref[...])
    wout_o[...] = _bf(wout_ref[...])

    row_t = _iota((c, c), 0)
    col_t = _iota((c, c), 1)
    l_incl = _bf(jnp.where(row_t >= col_t, 1.0, 0.0))

    lane = _iota((c, LANES), 1)
    m_a = lane < HEAD_DIM
    r2 = _iota((2 * c, 2 * c), 0)
    c2 = _iota((2 * c, 2 * c), 1)
    strict = r2 > c2
    incl = r2 >= c2
    eye = r2 == c2
    anti = (r2 < c) != (c2 < c)

    def stack(x):
        return jnp.concatenate([jnp.where(m_a, x, 0.0), jnp.where(m_a, 0.0, x)], axis=0)

    def stack_sw(x):
        return jnp.concatenate([jnp.where(m_a, 0.0, x), jnp.where(m_a, x, 0.0)], axis=0)

    tiles = range(N_TILES)
    sls = [slice(j * LANES, (j + 1) * LANES) for j in tiles]

    incl2 = jnp.concatenate([incl, incl], axis=1)
    blk = lambda s: (r2 // s) == (c2 // s)
    diag8 = strict & blk(8)

    def local_part(s, out):
        rows = slice(s * c, (s + 1) * c)
        cum_all = _cumsum_rows(l_incl, lw_ref[rows, :])
        a_st, r_st, rhs, v_sw, v_st, bk_tail, w_c = [], [], [], [], [], [], []
        for j in tiles:
            sl = sls[j]
            lw = lw_ref[rows, sl]
            cum = cum_all[:, sl]
            cum_last = cum[c - 1:c, :]
            e_neg = jnp.exp(-cum)
            e_tail = jnp.exp(cum_last - cum)
            k = k_ref[rows, sl].astype(F32)
            b = b_ref[rows, sl].astype(F32)
            a_st.append(stack(-kk_ref[rows, sl].astype(F32) * jnp.exp(cum - lw)))
            r_st.append(_bf(stack(r_ref[rows, sl].astype(F32) * jnp.exp(cum))))
            rhs.append(_bf(jnp.concatenate([stack(b * e_neg), stack(k * e_neg)], axis=0)))
            vs = pltpu.roll(v_ref[rows, sl].astype(F32), 64, axis=1)
            v_sw.append(_bf(vs))
            v_st.append(_bf(stack_sw(vs)))
            bk_tail.append(_bf(jnp.concatenate([b * e_tail, k * e_tail], axis=0)))
            w_c.append(jnp.exp(cum_last))
        yield
        d_ab, d_ak, d_r = [], [], []
        for j in tiles:
            gm = _dot(jnp.concatenate([_bf(a_st[j]), r_st[j]], axis=0), rhs[j], NT)
            d_ab.append(gm[:2 * c, :2 * c])
            d_ak.append(_bf(jnp.where(strict, gm[:2 * c, 2 * c:], 0.0)))
            d_r.append(_bf(jnp.where(incl2, gm[2 * c:, :], 0.0)))
        yield
        av = _pair_dots(d_ak, v_st)
        y0 = [a_st[j] + av[j] for j in tiles]
        d1 = [jnp.where(diag8, d_ab[j], 0.0) for j in tiles]
        p1 = [jnp.where(eye, 1.0, d1[j]) for j in tiles]
        d2 = _pair_dots(d1, d1)
        yield
        res = [_mm(d2[j], jnp.concatenate([p1[j], d2[j]], axis=1)) for j in tiles]
        p2 = [p1[j] + res[j][:, :2 * c] for j in tiles]
        yield
        d4p2 = _pair_dots([res[j][:, 2 * c:] for j in tiles], p2)
        tm = [p2[j] + d4p2[j] for j in tiles]
        for bs in (8, 16, 32):
            yield
            off = strict & blk(2 * bs) & jnp.logical_not(blk(bs))
            xm = _pair_dots(tm, [jnp.where(off, d_ab[j], 0.0) for j in tiles])
            yield
            xt = _pair_dots(xm, tm)
            tm = [tm[j] + xt[j] for j in tiles]
        yield
        out.update(y=_pair_dots(tm, y0), r_st=r_st, d_r=d_r, v_st=v_st, v_sw=v_sw, bk_tail=bk_tail, w_c=w_c)

    def state_part(s, loc, st):
        rows = slice(s * c, (s + 1) * c)
        ra, u_loc = [], []
        for j in tiles:
            yj = loc["y"][j]
            ah_st = jnp.concatenate([jnp.where(m_a, yj[:c], 0.0), jnp.where(m_a, 0.0, yj[c:])], axis=0)
            u_loc.append(jnp.concatenate([jnp.where(m_a, 0.0, yj[:c]), jnp.where(m_a, yj[c:], 0.0)], axis=0))
            ra.append(jnp.concatenate([loc["r_st"][j], _bf(ah_st)], axis=0))
        rs_as = _pair_dots(ra, st)
        u_st = [rs_as[j][2 * c:] + u_loc[j] for j in tiles]
        for j in tiles:
            o_st = rs_as[j][:2 * c] + _dot(loc["d_r"][j], jnp.concatenate([_bf(u_st[j]), loc["v_st"][j]], axis=0))
            o_ref[rows, sls[j]] = pltpu.roll(o_st[:c] + o_st[c:], 64, axis=1)
        new_st = []
        for j in tiles:
            u_sw = u_st[j][:c] + u_st[j][c:]
            delta = _dot(loc["bk_tail"][j], jnp.concatenate([_bf(u_sw), loc["v_sw"][j]], axis=0), TN)
            w_col = jnp.sum(jnp.where(eye, jnp.broadcast_to(loc["w_c"][j], (2 * c, 2 * c)), 0.0),
                            axis=1, keepdims=True)
            new_st.append(w_col * st[j] + jnp.where(anti, delta, 0.0))
        return new_st

    locs = [dict() for _ in range(WKV_SUB)]
    live = [local_part(s, locs[s]) for s in range(WKV_SUB)]
    while live:
        live = [g for g in live if next(g, "done") != "done"]
    st = [st_ref[j] for j in tiles]
    for s in range(WKV_SUB):
        st = state_part(s, locs[s], st)
    for j in tiles:
        st_ref[j] = st[j]


def _wkv(r, lw, k, v, kk, b, w_gate_up, w_down, w_out, l):
    t, wdt = r.shape
    rows = WKV_SUB * CHUNK
    steps = t // rows
    spec = pl.BlockSpec((rows, wdt), lambda i: (i, 0))

    def cast_specs(w, n_blocks):
        assert n_blocks <= steps and w.shape[1] % (16 * n_blocks) == 0
        blk = w.shape[1] // n_blocks
        return (pl.BlockSpec((None, blk, w.shape[2]), lambda i: (l, jnp.minimum(i, n_blocks - 1), 0)),
                pl.BlockSpec((blk, w.shape[2]), lambda i: (jnp.minimum(i, n_blocks - 1), 0)),
                jax.ShapeDtypeStruct(w.shape[1:], BF16))

    casts = [cast_specs(w_gate_up, 64), cast_specs(w_down, 32), cast_specs(w_out, 64)]
    return pl.pallas_call(
        _wkv_kernel,
        grid=(steps,),
        in_specs=[spec] * 6 + [c[0] for c in casts],
        out_specs=[spec] + [c[1] for c in casts],
        out_shape=[jax.ShapeDtypeStruct((t, wdt), F32)] + [c[2] for c in casts],
        scratch_shapes=[pltpu.VMEM((N_TILES, LANES, LANES), F32)],
        compiler_params=_params(dimension_semantics=("arbitrary",)),
        name="wkv",
    )(r, lw, k, v, kk, b, w_gate_up, w_down, w_out)


def _out_proj_kernel(ya_ref, o_ref, bon_ref, g_ref, gn_ref, ones_ref, w_ref, x_ref, mod_ref, ln_ref, out_ref,
                     *, parts):
    ones = ones_ref[...]
    pm = out_ref.shape[0] // parts
    slabs = [slice(j * SLAB, (j + 1) * SLAB) for j in range(RWKV_WIDTH // SLAB)]
    part_rows = [slice(p * pm, (p + 1) * pm) for p in range(parts)]
    oc = [[None] * len(slabs) for _ in range(parts)]
    var = [[None] * len(slabs) for _ in range(parts)]
    for p, rows in enumerate(part_rows):
        for j, sl in enumerate(slabs):
            o = o_ref[rows, sl]
            oc[p][j] = o - _head_sum(o, ones) * (1.0 / HEAD_DIM)
    for p in range(parts):
        for j in range(len(slabs)):
            var[p][j] = _mm(oc[p][j] * oc[p][j], ones) * (1.0 / HEAD_DIM)
    for p, rows in enumerate(part_rows):
        ys = []
        for j, sl in enumerate(slabs):
            y = ((oc[p][j] * lax.rsqrt(var[p][j] + RWKV_GN_EPS) * gn_ref[0:1, sl] + gn_ref[1:2, sl]
                  + bon_ref[rows, sl].astype(F32)) * g_ref[rows, sl].astype(F32))
            ys.append(_bf(y))
        mix = (_dot(ya_ref[rows, :], w_ref[:ATTN_WIDTH, :])
               + _dot(jnp.concatenate(ys, axis=1), w_ref[ATTN_WIDTH:, :]))
        z = DEEPNORM_ALPHA * x_ref[rows, :] + mod_ref[2:3, :] * mix
        out_ref[rows, :] = _layer_norm(z, ln_ref[0:1, :], ln_ref[1:2, :])


def _out_proj(y_attn, o, bonus, g, gn, ones, w_cat, x, mod, ln, l):
    t, d = x.shape
    tm = 512
    wdt = RWKV_WIDTH
    row = lambda width: pl.BlockSpec((tm, width), lambda i: (i, 0))
    layer = lambda a: pl.BlockSpec((None,) + a.shape[1:], lambda i: (l, 0, 0))
    return pl.pallas_call(
        functools.partial(_out_proj_kernel, parts=OUT_PARTS),
        grid=(t // tm,),
        in_specs=[row(ATTN_WIDTH), row(wdt), row(wdt), row(wdt), layer(gn),
                  pl.BlockSpec(ones.shape, lambda i: (0, 0)), pl.BlockSpec(w_cat.shape, lambda i: (0, 0)), row(d),
                  layer(mod), layer(ln)],
        out_specs=row(d),
        out_shape=jax.ShapeDtypeStruct((t, d), F32),
        compiler_params=_params(dimension_semantics=("parallel",)),
        name="out_proj",
    )(y_attn, o, bonus, g, gn, ones, w_cat, x, mod, ln)


def _ffn_kernel(x_ref, xp_ref, mod_ref, ln_ref, wg_ref, wu_ref, wd_ref, out_ref, u_ref, acc_ref, *, n_i):
    i, j = pl.program_id(0), pl.program_id(1)
    n_chunks = x_ref.shape[0] // FFN_LN_ROWS

    @pl.when((i == 0) & (j == 0))
    def _():
        acc_ref[...] = jnp.zeros_like(acc_ref)

    def finish_rows():
        r0 = pl.multiple_of(jnp.minimum(j, n_chunks - 1) * FFN_LN_ROWS, FFN_LN_ROWS)
        rows = pl.ds(r0, FFN_LN_ROWS)
        z = DEEPNORM_ALPHA * xp_ref[rows, :] + mod_ref[5:6, :] * acc_ref[(i + 1) % 2, rows, :]
        out_ref[rows, :] = _layer_norm(z, ln_ref[2:3, :], ln_ref[3:4, :])

    @pl.when(i < n_i)
    def _():
        cur = i % 2

        @pl.when(j == 0)
        def _():
            u_ref[...] = _bf(x_ref[...] * (1.0 + mod_ref[4:5, :]) + mod_ref[3:4, :])
            acc_ref[cur] = jnp.zeros(acc_ref.shape[1:], F32)

        finish_rows()
        u = u_ref[...]
        gate = _dot(u, wg_ref[...])
        up = _dot(u, wu_ref[...])
        h = _bf(gate * jax.nn.sigmoid(gate) * up)
        acc_ref[cur] += _dot(h, wd_ref[...])

    @pl.when(i == n_i)
    def _():
        finish_rows()


def _ffn(x, mod, ln, w_gu, w_dn, l):
    t, d = x.shape
    f = w_dn.shape[0]
    tm, tf = 512, 512
    nf, n_i = f // tf, t // tm
    assert nf >= tm // FFN_LN_ROWS
    cur = lambda i, j: (jnp.minimum(i, n_i - 1), 0)
    prev = lambda i, j: (jnp.maximum(i - 1, 0), 0)
    wj = lambda i, j: jnp.where(i < n_i, j, nf - 1)
    return pl.pallas_call(
        functools.partial(_ffn_kernel, n_i=n_i),
        grid=(n_i + 1, nf),
        in_specs=[pl.BlockSpec((tm, d), cur),
                  pl.BlockSpec((tm, d), prev),
                  pl.BlockSpec((None,) + mod.shape[1:], lambda i, j: (l, 0, 0)),
                  pl.BlockSpec((None,) + ln.shape[1:], lambda i, j: (l, 0, 0)),
                  pl.BlockSpec((d, tf), lambda i, j: (0, wj(i, j))),
                  pl.BlockSpec((d, tf), lambda i, j: (0, wj(i, j) + nf)),
                  pl.BlockSpec((tf, d), lambda i, j: (wj(i, j), 0))],
        out_specs=pl.BlockSpec((tm, d), prev),
        out_shape=jax.ShapeDtypeStruct((t, d), F32),
        scratch_shapes=[pltpu.VMEM((tm, d), BF16), pltpu.VMEM((2, tm, d), F32)],
        compiler_params=_params(dimension_semantics=("arbitrary", "arbitrary")),
        name="ffn",
    )(x, x, mod, ln, w_gu, w_gu, w_dn)


def _pad_axis(a, axis, before, total):
    pads = [(0, 0)] * a.ndim
    pads[axis] = (before, total - before - a.shape[axis])
    return jnp.pad(a, pads)


def kernel(x, c, positions, w_ada, b_ada, w_in, attn_sinks, rwkv_mu, rwkv_w0, rwkv_w2, rwkv_a0, rwkv_a2, rwkv_g2, rwkv_k_k, rwkv_k_a, rwkv_r_k, rwkv_gn_w, rwkv_gn_b, rwkv_v0, rwkv_v1, rwkv_v2, w_out, ln1_w, ln1_b, w_gate_up, w_down, ln2_w, ln2_b):
    bsz, t, d = x.shape
    assert bsz == 1 and d == D_MODEL
    depth = w_in.shape[0]
    rw = RWKV_WIDTH
    inv =ROPE_THETA ** (-jnp.arange(0, HEAD_DIM, 2, dtype=F32) / HEAD_DIM)
    ang = positions[0].astype(F32)[:, None] * inv
    cos, sin = jnp.cos(ang), jnp.sin(ang)
    cos_t = jnp.concatenate([cos, cos, cos, cos], axis=-1)
    sin_t = jnp.concatenate([-sin, sin, -sin, sin], axis=-1)

    w_pad = _pad_axis(w_in, 2, 0, N_PROJ).astype(BF16)
    sinks = attn_sinks.reshape(-1)
    mu3 = rwkv_mu[:, :3 * rw].reshape(depth, 3, rw)
    mu_lora = _pad_axis(rwkv_mu[:, 3 * rw:], 1, 0, LORA_PAD).reshape(depth, 1, LORA_PAD)
    zero = jnp.zeros((depth, rw), F32)
    v0_all = _pad_axis(rwkv_v0, 0, 1, depth)
    params = jnp.stack([rwkv_w0, rwkv_a0, rwkv_k_k, rwkv_k_a, rwkv_r_k.reshape(depth, rw), v0_all, zero, zero],
                       axis=1)
    w2p = _pad_axis(rwkv_w2, 1, 0, LANES)
    a2p = _pad_axis(rwkv_a2, 1, DECAY_LORA, LANES)
    g2p = _pad_axis(rwkv_g2, 1, 0, 2 * LANES)
    v1p = _pad_axis(rwkv_v1, 2, 0, LANES)
    v2p = _pad_axis(rwkv_v2, 1, 0, LANES)
    seg = np.arange(SLAB) // HEAD_DIM
    ones = jnp.asarray(seg[:, None] == seg[None, :], BF16)
    gn = jnp.stack([rwkv_gn_w, rwkv_gn_b], axis=1)
    ln = jnp.stack([ln1_w, ln1_b, ln2_w, ln2_b], axis=1)

    mod = _adaln(c, w_ada, b_ada)
    xs = x[0]
    v_first = None
    for l in range(depth):
        proj = _in_proj(xs, mod, w_pad, l)
        y_attn = _attention(proj, sinks, cos_t, sin_t, l)
        vres = None if l == 0 else (v_first, v1p, v2p)
        r, lw, k_mod, v, kk, b, g, bonus = _rwkv_prep(proj, mu3, mu_lora, params, w2p, a2p, g2p, ones, vres, l)
        if l == 0:
            v_first = v
        o, w_gu, w_dn, w_out_bf = _wkv(r, lw, k_mod, v, kk, b, w_gate_up, w_down, w_out, l)
        xs = _out_proj(y_attn, o, bonus, g, gn, ones, w_out_bf, xs, mod, ln, l)
        xs = _ffn(xs, mod, ln, w_gu, w_dn, l)
    return xs[None]
```

```python
import functools

import numpy as np
import jax
import jax.numpy as jnp
from jax import lax
from jax.experimental import pallas as pl
from jax.experimental.pallas import tpu as pltpu

F32 = jnp.float32
BF16 = jnp.bfloat16

D_MODEL = 2048
DEPTH = 2
HEAD_DIM = 64
HALF = HEAD_DIM // 2
ATTN_WIDTH = 1024
ATTN_HEADS = 16
KV_HEADS = 4
KV_WIDTH = KV_HEADS * HEAD_DIM
ATTN_COLS = ATTN_WIDTH + 2 * KV_WIDTH
WINDOW = 128
ROPE_THETA = 10000.0
RWKV_WIDTH = 1024
DECAY_LORA = 64
AAA_LORA = 64
GATE_LORA = 160
LORA_COLS = DECAY_LORA + AAA_LORA + GATE_LORA
MV_LORA = 32
RWKV_GN_EPS = 64e-5
FFN_HIDDEN = 5632
LN_EPS = 1e-5
N_MOD = 6
DEEPNORM_ALPHA = (2.0 * DEPTH) ** 0.25

LANES = 128
N_TILES = RWKV_WIDTH // LANES
CHUNK = 64
WKV_SUB = 2
OUT_PARTS = 2
FFN_LN_ROWS = 64
VMEM_LIMIT = 56 * 1024 * 1024

COL_Q, COL_AK, COL_AV, COL_LORA = 0, 1024, 1280, 1536
COL_R, COL_K, COL_V = 2048, 3072, 4096
N_PROJ = 5120
LORA_PAD = 512
SLAB = 256
PROJ_TN = 512

NN = ((1,), (0,))
NT = ((1,), (1,))
TN = ((0,), (0,))


def _dot(a, b, dims=NN):
    return lax.dot_general(a, b, (dims, ((), ())), preferred_element_type=F32)


def _bf(z):
    return z.astype(BF16)


def _mm(a, b):
    return _dot(_bf(a), _bf(b))


def _cumsum_rows(tri_bf16, b):
    b0 = _bf(b)
    r1 = b - b0.astype(F32)
    b1 = _bf(r1)
    b2 = _bf(r1 - b1.astype(F32))
    return _dot(tri_bf16, b0) + (_dot(tri_bf16, b1) + _dot(tri_bf16, b2))


def _head_sum(a, ones_bf16):
    a0 = _bf(a)
    a1 = _bf(a - a0.astype(F32))
    return _dot(a0, ones_bf16) + _dot(a1, ones_bf16)


def _pair_dots(lhs, rhs):
    out = []
    for p in range(0, len(lhs), 2):
        r0, r1 = _bf(rhs[p]), _bf(rhs[p + 1])
        z = jnp.zeros_like(r0)
        bd = jnp.concatenate([jnp.concatenate([r0, z], axis=1), jnp.concatenate([z, r1], axis=1)], axis=0)
        res = _dot(jnp.concatenate([_bf(lhs[p]), _bf(lhs[p + 1])], axis=1), bd)
        n = r0.shape[1]
        out += [res[:, :n], res[:, n:]]
    return out


def _iota(shape, dim):
    return lax.broadcasted_iota(jnp.int32, shape, dim)


def _layer_norm(z, w, b):
    mu = jnp.mean(z, axis=-1, keepdims=True)
    zc = z - mu
    var = jnp.mean(zc * zc, axis=-1, keepdims=True)
    return zc * lax.rsqrt(var + LN_EPS) * w + b


def _params(**kw):
    return pltpu.CompilerParams(vmem_limit_bytes=VMEM_LIMIT, **kw)


def _adaln_kernel(c_ref, w_ref, b_ref, o_ref, *, tn):
    cb = c_ref[...]
    cond = cb * jax.nn.sigmoid(cb)
    for j in range(tn // LANES):
        sl = slice(j * LANES, (j + 1) * LANES)
        o_ref[:, sl] = jnp.sum(w_ref[:, sl] * cond, axis=0, keepdims=True) + b_ref[:, sl]


def _adaln(c, w_ada, b_ada):
    depth, d, n = w_ada.shape
    tn = 1024
    c_b = jnp.broadcast_to(c.reshape(d, 1), (d, LANES))
    out = pl.pallas_call(
        functools.partial(_adaln_kernel, tn=tn),
        grid=(depth, n // tn),
        in_specs=[pl.BlockSpec((d, LANES), lambda l, j: (0, 0)),
                  pl.BlockSpec((None, d, tn), lambda l, j: (l, 0, j)),
                  pl.BlockSpec((None, 1, tn), lambda l, j: (l, 0, j))],
        out_specs=pl.BlockSpec((None, 1, tn), lambda l, j: (l, 0, j)),
        out_shape=jax.ShapeDtypeStruct((depth, 1, n), F32),
        compiler_params=_params(dimension_semantics=("parallel", "parallel")),
        name="adaln",
    )(c_b, w_ada, b_ada.reshape(depth, 1, n))
    return out.reshape(depth, N_MOD, d)


def _in_proj_kernel(x_ref, mod_ref, w_ref, o_ref, u_ref):
    @pl.when(pl.program_id(1) == 0)
    def _():
        u_ref[...] = _bf(x_ref[...] * (1.0 + mod_ref[1:2, :]) + mod_ref[0:1, :])

    o_ref[...] = _dot(u_ref[...], w_ref[...])


def _in_proj(x, mod, w_pad, l):
    t, d = x.shape
    tm, tn = 1024, PROJ_TN
    lora_out, lora_w = COL_LORA // tn, (ATTN_COLS + 3 * RWKV_WIDTH) // tn

    def w_block(j):
        return jnp.where(j < lora_out, j, jnp.where(j == lora_out, lora_w, j - 1))

    return pl.pallas_call(
        _in_proj_kernel,
        grid=(t // tm, N_PROJ // tn),
        in_specs=[pl.BlockSpec((tm, d), lambda i, j: (i, 0)),
                  pl.BlockSpec((None, N_MOD, d), lambda i, j: (l, 0, 0)),
                  pl.BlockSpec((None, d, tn), lambda i, j: (l, 0, w_block(j)))],
        out_specs=pl.BlockSpec((tm, tn), lambda i, j: (i, j)),
        out_shape=jax.ShapeDtypeStruct((t, N_PROJ), F32),
        scratch_shapes=[pltpu.VMEM((tm, d), BF16)],
        compiler_params=_params(dimension_semantics=("parallel", "arbitrary")),
        name="in_proj",
    )(x, mod, w_pad)


def _rope(t, cos, sin, lo_half):
    rot = jnp.where(lo_half, pltpu.roll(t, LANES - HALF, axis=1), pltpu.roll(t, HALF, axis=1))
    return t * cos + rot * sin


def _attn_kernel(sink_ref, q_ref, kc_ref, kp_ref, vc_ref, vp_ref, cosc_ref, sinc_ref, cosp_ref, sinp_ref,
                 o_ref, *, l):
    blk = pl.program_id(0)
    w = WINDOW
    cos_c, sin_c = cosc_ref[...], sinc_ref[...]
    cos_p, sin_p = cosp_ref[...], sinp_ref[...]

    lane = _iota((w, LANES), 1)
    lo_half = (lane % HEAD_DIM) < HALF
    in_a = lane < HEAD_DIM
    lane2 = _iota((2 * w, LANES), 1)
    in_a2 = lane2 < HEAD_DIM

    qa = _iota((2 * w, 2 * w), 0) % w
    ks = _iota((2 * w, 2 * w), 1)
    valid = (ks > qa) & (ks <= qa + w) & ((ks >= w) | (blk > 0))
    row_is_a = _iota((2 * w, 1), 0) < w
    n_tiles = ATTN_HEADS // 2
    tiles = range(n_tiles)

    ones_kv = jnp.ones((2 * w, LANES), BF16)
    k_dup, v_dup = [], []
    for m in range(KV_HEADS // 2):
        sl_kv = slice(m * LANES, (m + 1) * LANES)
        k2 = jnp.concatenate([_rope(kp_ref[:, sl_kv], cos_p, sin_p, lo_half),
                              _rope(kc_ref[:, sl_kv], cos_c, sin_c, lo_half)], axis=0)
        v2 = jnp.concatenate([vp_ref[:, sl_kv], vc_ref[:, sl_kv]], axis=0)
        k2r, v2r = pltpu.roll(k2, HEAD_DIM, axis=1), pltpu.roll(v2, HEAD_DIM, axis=1)
        k_dup += [_bf(jnp.where(in_a2, k2, k2r)), _bf(jnp.where(in_a2, k2r, k2))]
        v_dup += [jnp.concatenate([_bf(jnp.where(in_a2, v2, v2r)), ones_kv], axis=1),
                  jnp.concatenate([_bf(jnp.where(in_a2, v2r, v2)), ones_kv], axis=1)]

    s = []
    for t in tiles:
        q = _rope(q_ref[:, t * LANES:(t + 1) * LANES], cos_c, sin_c, lo_half) * (HEAD_DIM ** -0.5)
        q_st = _bf(jnp.concatenate([jnp.where(in_a, q, 0.0), jnp.where(in_a, 0.0, q)], axis=0))
        s.append(jnp.where(valid, _dot(q_st, k_dup[t // 2], NT), -1e30))
    sink = [jnp.where(row_is_a, sink_ref[l * ATTN_HEADS + 2 * t], sink_ref[l * ATTN_HEADS + 2 * t + 1])
            for t in tiles]
    mx = [jnp.maximum(jnp.max(s[t], axis=-1, keepdims=True), sink[t]) for t in tiles]
    p = [jnp.exp(s[t] - mx[t]) for t in tiles]
    for t in tiles:
        ov = _dot(_bf(p[t]), v_dup[t // 2])
        o_st = ov[:, :LANES] * (1.0 / (ov[:, LANES:] + jnp.exp(sink[t] - mx[t])))
        o_ref[:, t * LANES:(t + 1) * LANES] = _bf(jnp.where(in_a, o_st[:w], o_st[w:]))


def _attention(proj, sinks, cos_t, sin_t, l):
    t = proj.shape[0]
    w = WINDOW
    prev = lambda i, s: (jnp.maximum(i - 1, 0), 0)
    cur = lambda i, s: (i, 0)
    return pl.pallas_call(
        functools.partial(_attn_kernel, l=l),
        grid_spec=pltpu.PrefetchScalarGridSpec(
            num_scalar_prefetch=1,
            grid=(t // w,),
            in_specs=[pl.BlockSpec((w, ATTN_WIDTH), lambda i, s: (i, COL_Q // ATTN_WIDTH)),
                      pl.BlockSpec((w, KV_WIDTH), lambda i, s: (i, COL_AK // KV_WIDTH)),
                      pl.BlockSpec((w, KV_WIDTH), lambda i, s: (jnp.maximum(i - 1, 0), COL_AK // KV_WIDTH)),
                      pl.BlockSpec((w, KV_WIDTH), lambda i, s: (i, COL_AV // KV_WIDTH)),
                      pl.BlockSpec((w, KV_WIDTH), lambda i, s: (jnp.maximum(i - 1, 0), COL_AV // KV_WIDTH)),
                      pl.BlockSpec((w, LANES), cur), pl.BlockSpec((w, LANES), cur),
                      pl.BlockSpec((w, LANES), prev), pl.BlockSpec((w, LANES), prev)],
            out_specs=pl.BlockSpec((w, ATTN_WIDTH), lambda i, s: (i, 0)),
        ),
        out_shape=jax.ShapeDtypeStruct((t, ATTN_WIDTH), BF16),
        compiler_params=_params(dimension_semantics=("parallel",)),
        name="attention",
    )(sinks, proj, proj, proj, proj, proj, cos_t, sin_t, cos_t, sin_t)


def _prep_kernel(*refs, has_vres):
    if has_vres:
        (r_ref, k_ref, v_ref, l_ref, rp_ref, kp_ref, vp_ref, lp_ref, mu_ref, mul_ref, par_ref,
         w2_ref, a2_ref, g2_ref, ones_ref, vf_ref, v1_ref, v2_ref,
         ro_ref, lw_ref, ko_ref, vo_ref, kk_ref, b_ref, g_ref, bon_ref) = refs
    else:
        (r_ref, k_ref, v_ref, l_ref, rp_ref, kp_ref, vp_ref, lp_ref, mu_ref, mul_ref, par_ref,
         w2_ref, a2_ref, g2_ref, ones_ref,
         ro_ref, lw_ref, ko_ref, vo_ref, kk_ref, b_ref, g_ref, bon_ref) = refs
    first = pl.program_id(0) == 0

    def shifted_lerp(x_ref, p_ref, mu):
        x = x_ref[...]
        prev_row = jnp.where(first, 0.0, p_ref[7:8, :])
        xr = pltpu.roll(x, 1, axis=0)
        row = _iota((8, x.shape[1]), 0)
        xp = jnp.concatenate([jnp.where(row == 0, prev_row, xr[0:8]), xr[8:]], axis=0)
        return x + (xp - x) * mu

    r = shifted_lerp(r_ref, rp_ref, mu_ref[0:1, :])
    k = shifted_lerp(k_ref, kp_ref, mu_ref[1:2, :])
    v = shifted_lerp(v_ref, vp_ref, mu_ref[2:3, :])
    lo = shifted_lerp(l_ref, lp_ref, mul_ref[...])

    w0, a0 = par_ref[0:1, :], par_ref[1:2, :]
    k_k, k_a, r_k = par_ref[2:3, :], par_ref[3:4, :], par_ref[4:5, :]

    wa = lo[:, 0:LANES]
    gd = lo[:, LANES:3 * LANES]
    log_decay = -float(np.exp(-0.5)) * jax.nn.sigmoid(w0 + _mm(jnp.tanh(wa), w2_ref[...]))
    a = jax.nn.sigmoid(a0 + _mm(wa, a2_ref[...]))
    g = _mm(jax.nn.sigmoid(gd), g2_ref[...])
    if has_vres:
        v0 = par_ref[5:6, :]
        mix = jax.nn.sigmoid(v0 + _mm(_mm(v, v1_ref[...]), v2_ref[...]))
        v = v + (vf_ref[...].astype(F32) - v) * mix

    kk = k * k_k
    k_mod = k * (1.0 + (a - 1.0) * k_a)
    rk = r * k_mod * r_k
    ones = ones_ref[...]
    for j in range(RWKV_WIDTH // SLAB):
        sl = slice(j * SLAB, (j + 1) * SLAB)
        kk_j = kk[:, sl]
        ss = _mm(kk_j * kk_j, ones)
        kk_n = kk_j * lax.rsqrt(jnp.maximum(ss, 1e-24))
        kk_ref[:, sl] = _bf(kk_n)
        b_ref[:, sl] = _bf(kk_n * a[:, sl])
        bon_ref[:, sl] = _bf(_mm(rk[:, sl], ones) * v[:, sl])
    ro_ref[...] = _bf(r)
    lw_ref[...] = log_decay
    ko_ref[...] = _bf(k_mod)
    vo_ref[...] = _bf(v)
    g_ref[...] = _bf(g)


def _rwkv_prep(proj, mu3, mu_lora, params, w2p, a2p, g2p, ones, vres, l):
    t = proj.shape[0]
    tm = 256
    wdt = RWKV_WIDTH
    has_vres = vres is not None

    def col(cb, width):
        return pl.BlockSpec((tm, width), lambda i: (i, cb))

    def prev(cb, width):
        return pl.BlockSpec((8, width), lambda i: (jnp.maximum(i * (tm // 8) - 1, 0), cb))

    def layer(a, lyr=l):
        return pl.BlockSpec((None,) + a.shape[1:], lambda i: (lyr, 0, 0))

    in_specs = [col(COL_R // wdt, wdt), col(COL_K // wdt, wdt), col(COL_V // wdt, wdt),
                col(COL_LORA // LORA_PAD, LORA_PAD),
                prev(COL_R // wdt, wdt), prev(COL_K // wdt, wdt), prev(COL_V // wdt, wdt),
                prev(COL_LORA // LORA_PAD, LORA_PAD),
                layer(mu3), layer(mu_lora), layer(params), layer(w2p), layer(a2p), layer(g2p),
                pl.BlockSpec(ones.shape, lambda i: (0, 0))]
    args = [proj] * 8 + [mu3, mu_lora, params, w2p, a2p, g2p, ones]
    if has_vres:
        v_first, v1p, v2p = vres
        in_specs += [pl.BlockSpec((tm, wdt), lambda i: (i, 0)), layer(v1p, l - 1), layer(v2p, l - 1)]
        args += [v_first, v1p, v2p]
    out_spec = pl.BlockSpec((tm, wdt), lambda i: (i, 0))
    out_dtypes = [BF16, F32, BF16, BF16, BF16, BF16, BF16, BF16]
    return pl.pallas_call(
        functools.partial(_prep_kernel, has_vres=has_vres),
        grid=(t // tm,),
        in_specs=in_specs,
        out_specs=[out_spec] * 8,
        out_shape=[jax.ShapeDtypeStruct((t, wdt), dt) for dt in out_dtypes],
        compiler_params=_params(dimension_semantics=("parallel",)),
        name="rwkv_prep",
    )(*args)


def _wkv_kernel(r_ref, lw_ref, k_ref, v_ref, kk_ref, b_ref, wgu_ref, wdn_ref, wout_ref,
                o_ref, wgu_o, wdn_o, wout_o, st_ref):
    c = CHUNK

    @pl.when(pl.program_id(0) == 0)
    def _():
        st_ref[...] = jnp.zeros_like(st_ref)

    wgu_o[...] = _bf(wgu_ref[...])
    wdn_o[...] = _bf(wdn_ref[...])
    wout_o[...] = _bf(wout_ref[...])

    row_t = _iota((c, c), 0)
    col_t = _iota((c, c), 1)
    l_incl = _bf(jnp.where(row_t >= col_t, 1.0, 0.0))

    lane = _iota((c, LANES), 1)
    m_a = lane < HEAD_DIM
    r2 = _iota((2 * c, 2 * c), 0)
    c2 = _iota((2 * c, 2 * c), 1)
    strict = r2 > c2
    incl = r2 >= c2
    eye = r2 == c2
    anti = (r2 < c) != (c2 < c)

    def stack(x):
        return jnp.concatenate([jnp.where(m_a, x, 0.0), jnp.where(m_a, 0.0, x)], axis=0)

    def stack_sw(x):
        return jnp.concatenate([jnp.where(m_a, 0.0, x), jnp.where(m_a, x, 0.0)], axis=0)

    tiles = range(N_TILES)
    sls = [slice(j * LANES, (j + 1) * LANES) for j in tiles]

    incl2 = jnp.concatenate([incl, incl], axis=1)
    blk = lambda s: (r2 // s) == (c2 // s)
    diag8 = strict & blk(8)

    def local_part(s, out):
        rows = slice(s * c, (s + 1) * c)
        cum_all = _cumsum_rows(l_incl, lw_ref[rows, :])
        a_st, r_st, rhs, v_sw, v_st, bk_tail, w_c = [], [], [], [], [], [], []
        for j in tiles:
            sl = sls[j]
            lw = lw_ref[rows, sl]
            cum = cum_all[:, sl]
            cum_last = cum[c - 1:c, :]
            e_neg = jnp.exp(-cum)
            e_tail = jnp.exp(cum_last - cum)
            k = k_ref[rows, sl].astype(F32)
            b = b_ref[rows, sl].astype(F32)
            a_st.append(stack(-kk_ref[rows, sl].astype(F32) * jnp.exp(cum - lw)))
            r_st.append(_bf(stack(r_ref[rows, sl].astype(F32) * jnp.exp(cum))))
            rhs.append(_bf(jnp.concatenate([stack(b * e_neg), stack(k * e_neg)], axis=0)))
            vs = pltpu.roll(v_ref[rows, sl].astype(F32), 64, axis=1)
            v_sw.append(_bf(vs))
            v_st.append(_bf(stack_sw(vs)))
            bk_tail.append(_bf(jnp.concatenate([b * e_tail, k * e_tail], axis=0)))
            w_c.append(jnp.exp(cum_last))
        yield
        d_ab, d_ak, d_r = [], [], []
        for j in tiles:
            gm = _dot(jnp.concatenate([_bf(a_st[j]), r_st[j]], axis=0), rhs[j], NT)
            d_ab.append(gm[:2 * c, :2 * c])
            d_ak.append(_bf(jnp.where(strict, gm[:2 * c, 2 * c:], 0.0)))
            d_r.append(_bf(jnp.where(incl2, gm[2 * c:, :], 0.0)))
        yield
        av = _pair_dots(d_ak, v_st)
        y0 = [a_st[j] + av[j] for j in tiles]
        d1 = [jnp.where(diag8, d_ab[j], 0.0) for j in tiles]
        p1 = [jnp.where(eye, 1.0, d1[j]) for j in tiles]
        d2 = _pair_dots(d1, d1)
        yield
        res = [_mm(d2[j], jnp.concatenate([p1[j], d2[j]], axis=1)) for j in tiles]
        p2 = [p1[j] + res[j][:, :2 * c] for j in tiles]
        yield
        d4p2 = _pair_dots([res[j][:, 2 * c:] for j in tiles], p2)
        tm = [p2[j] + d4p2[j] for j in tiles]
        for bs in (8, 16, 32):
            yield
            off = strict & blk(2 * bs) & jnp.logical_not(blk(bs))
            xm = _pair_dots(tm, [jnp.where(off, d_ab[j], 0.0) for j in tiles])
            yield
            xt = _pair_dots(xm, tm)
            tm = [tm[j] + xt[j] for j in tiles]
        yield
        out.update(y=_pair_dots(tm, y0), r_st=r_st, d_r=d_r, v_st=v_st, v_sw=v_sw, bk_tail=bk_tail, w_c=w_c)

    def state_part(s, loc, st):
        rows = slice(s * c, (s + 1) * c)
        ra, u_loc = [], []
        for j in tiles:
            yj = loc["y"][j]
            ah_st = jnp.concatenate([jnp.where(m_a, yj[:c], 0.0), jnp.where(m_a, 0.0, yj[c:])], axis=0)
            u_loc.append(jnp.concatenate([jnp.where(m_a, 0.0, yj[:c]), jnp.where(m_a, yj[c:], 0.0)], axis=0))
            ra.append(jnp.concatenate([loc["r_st"][j], _bf(ah_st)], axis=0))
        rs_as = _pair_dots(ra, st)
        u_st = [rs_as[j][2 * c:] + u_loc[j] for j in tiles]
        for j in tiles:
            o_st = rs_as[j][:2 * c] + _dot(loc["d_r"][j], jnp.concatenate([_bf(u_st[j]), loc["v_st"][j]], axis=0))
            o_ref[rows, sls[j]] = pltpu.roll(o_st[:c] + o_st[c:], 64, axis=1)
        new_st = []
        for j in tiles:
            u_sw = u_st[j][:c] + u_st[j][c:]
            delta = _dot(loc["bk_tail"][j], jnp.concatenate([_bf(u_sw), loc["v_sw"][j]], axis=0), TN)
            w_col = jnp.sum(jnp.where(eye, jnp.broadcast_to(loc["w_c"][j], (2 * c, 2 * c)), 0.0),
                            axis=1, keepdims=True)
            new_st.append(w_col * st[j] + jnp.where(anti, delta, 0.0))
        return new_st

    locs = [dict() for _ in range(WKV_SUB)]
    live = [local_part(s, locs[s]) for s in range(WKV_SUB)]
    while live:
        live = [g for g in live if next(g, "done") != "done"]
    st = [st_ref[j] for j in tiles]
    for s in range(WKV_SUB):
        st = state_part(s, locs[s], st)
    for j in tiles:
        st_ref[j] = st[j]


def _wkv(r, lw, k, v, kk, b, w_gate_up, w_down, w_out, l):
    t, wdt = r.shape
    rows = WKV_SUB * CHUNK
    steps = t // rows
    spec = pl.BlockSpec((rows, wdt), lambda i: (i, 0))

    def cast_specs(w, n_blocks):
        assert n_blocks <= steps and w.shape[1] % (16 * n_blocks) == 0
        blk = w.shape[1] // n_blocks
        return (pl.BlockSpec((None, blk, w.shape[2]), lambda i: (l, jnp.minimum(i, n_blocks - 1), 0)),
                pl.BlockSpec((blk, w.shape[2]), lambda i: (jnp.minimum(i, n_blocks - 1), 0)),
                jax.ShapeDtypeStruct(w.shape[1:], BF16))

    casts = [cast_specs(w_gate_up, 64), cast_specs(w_down, 32), cast_specs(w_out, 64)]
    return pl.pallas_call(
        _wkv_kernel,
        grid=(steps,),
        in_specs=[spec] * 6 + [c[0] for c in casts],
        out_specs=[spec] + [c[1] for c in casts],
        out_shape=[jax.ShapeDtypeStruct((t, wdt), F32)] + [c[2] for c in casts],
        scratch_shapes=[pltpu.VMEM((N_TILES, LANES, LANES), F32)],
        compiler_params=_params(dimension_semantics=("arbitrary",)),
        name="wkv",
    )(r, lw, k, v, kk, b, w_gate_up, w_down, w_out)


def _out_proj_kernel(ya_ref, o_ref, bon_ref, g_ref, gn_ref, ones_ref, w_ref, x_ref, mod_ref, ln_ref, out_ref,
                     *, parts):
    ones = ones_ref[...]
    pm = out_ref.shape[0] // parts
    slabs = [slice(j * SLAB, (j + 1) * SLAB) for j in range(RWKV_WIDTH // SLAB)]
    part_rows = [slice(p * pm, (p + 1) * pm) for p in range(parts)]
    oc = [[None] * len(slabs) for _ in range(parts)]
    var = [[None] * len(slabs) for _ in range(parts)]
    for p, rows in enumerate(part_rows):
        for j, sl in enumerate(slabs):
            o = o_ref[rows, sl]
            oc[p][j] = o - _head_sum(o, ones) * (1.0 / HEAD_DIM)
    for p in range(parts):
        for j in range(len(slabs)):
            var[p][j] = _mm(oc[p][j] * oc[p][j], ones) * (1.0 / HEAD_DIM)
    for p, rows in enumerate(part_rows):
        ys = []
        for j, sl in enumerate(slabs):
            y = ((oc[p][j] * lax.rsqrt(var[p][j] + RWKV_GN_EPS) * gn_ref[0:1, sl] + gn_ref[1:2, sl]
                  + bon_ref[rows, sl].astype(F32)) * g_ref[rows, sl].astype(F32))
            ys.append(_bf(y))
        mix = (_dot(ya_ref[rows, :], w_ref[:ATTN_WIDTH, :])
               + _dot(jnp.concatenate(ys, axis=1), w_ref[ATTN_WIDTH:, :]))
        z = DEEPNORM_ALPHA * x_ref[rows, :] + mod_ref[2:3, :] * mix
        out_ref[rows, :] = _layer_norm(z, ln_ref[0:1, :], ln_ref[1:2, :])


def _out_proj(y_attn, o, bonus, g, gn, ones, w_cat, x, mod, ln, l):
    t, d = x.shape
    tm = 512
    wdt = RWKV_WIDTH
    row = lambda width: pl.BlockSpec((tm, width), lambda i: (i, 0))
    layer = lambda a: pl.BlockSpec((None,) + a.shape[1:], lambda i: (l, 0, 0))
    return pl.pallas_call(
        functools.partial(_out_proj_kernel, parts=OUT_PARTS),
        grid=(t // tm,),
        in_specs=[row(ATTN_WIDTH), row(wdt), row(wdt), row(wdt), layer(gn),
                  pl.BlockSpec(ones.shape, lambda i: (0, 0)), pl.BlockSpec(w_cat.shape, lambda i: (0, 0)), row(d),
                  layer(mod), layer(ln)],
        out_specs=row(d),
        out_shape=jax.ShapeDtypeStruct((t, d), F32),
        compiler_params=_params(dimension_semantics=("parallel",)),
        name="out_proj",
    )(y_attn, o, bonus, g, gn, ones, w_cat, x, mod, ln)


def _ffn_kernel(x_ref, xp_ref, mod_ref, ln_ref, wg_ref, wu_ref, wd_ref, out_ref, u_ref, acc_ref, done_ref,
                *, n_i):
    i, j = pl.program_id(0), pl.program_id(1)
    n_chunks = x_ref.shape[0] // FFN_LN_ROWS

    @pl.when((i == 0) & (j == 0))
    def _():
        done_ref[...] = jnp.zeros_like(done_ref)

    def finish_rows():
        r0 = pl.multiple_of(jnp.minimum(j, n_chunks - 1) * FFN_LN_ROWS, FFN_LN_ROWS)
        rows = pl.ds(r0, FFN_LN_ROWS)
        z = DEEPNORM_ALPHA * xp_ref[rows, :] + mod_ref[5:6, :] * done_ref[rows, :]
        out_ref[rows, :] = _layer_norm(z, ln_ref[2:3, :], ln_ref[3:4, :])

    @pl.when(i < n_i)
    def _():
        @pl.when(j == 0)
        def _():
            u_ref[...] = _bf(x_ref[...] * (1.0 + mod_ref[4:5, :]) + mod_ref[3:4, :])
            acc_ref[...] = jnp.zeros_like(acc_ref)

        finish_rows()
        u = u_ref[...]
        gate = _dot(u, wg_ref[...])
        up = _dot(u, wu_ref[...])
        h = _bf(gate * jax.nn.sigmoid(gate) * up)
        acc_ref[...] += _dot(h, wd_ref[...])

        @pl.when(j == pl.num_programs(1) - 1)
        def _():
            done_ref[...] = acc_ref[...]

    @pl.when(i == n_i)
    def _():
        finish_rows()


def _ffn(x, mod, ln, w_gu, w_dn, l):
    t, d = x.shape
    f = w_dn.shape[0]
    tm, tf = 512, 512
    nf, n_i = f // tf, t // tm
    assert nf >= tm // FFN_LN_ROWS
    cur = lambda i, j: (jnp.minimum(i, n_i - 1), 0)
    prev = lambda i, j: (jnp.maximum(i - 1, 0), 0)
    wj = lambda i, j: jnp.where(i < n_i, j, nf - 1)
    return pl.pallas_call(
        functools.partial(_ffn_kernel, n_i=n_i),
        grid=(n_i + 1, nf),
        in_specs=[pl.BlockSpec((tm, d), cur),
                  pl.BlockSpec((tm, d), prev),
                  pl.BlockSpec((None,) + mod.shape[1:], lambda i, j: (l, 0, 0)),
                  pl.BlockSpec((None,) + ln.shape[1:], lambda i, j: (l, 0, 0)),
                  pl.BlockSpec((d, tf), lambda i, j: (0, wj(i, j))),
                  pl.BlockSpec((d, tf), lambda i, j: (0, wj(i, j) + nf)),
                  pl.BlockSpec((tf, d), lambda i, j: (wj(i, j), 0))],
        out_specs=pl.BlockSpec((tm, d), prev),
        out_shape=jax.ShapeDtypeStruct((t, d), F32),
        scratch_shapes=[pltpu.VMEM((tm, d), BF16), pltpu.VMEM((tm, d), F32), pltpu.VMEM((tm, d), F32)],
        compiler_params=_params(dimension_semantics=("arbitrary", "arbitrary")),
        name="ffn",
    )(x, x, mod, ln, w_gu, w_gu, w_dn)


def _pad_axis(a, axis, before, total):
    pads = [(0, 0)] * a.ndim
    pads[axis] = (before, total - before - a.shape[axis])
    return jnp.pad(a, pads)


def kernel(x, c, positions, w_ada, b_ada, w_in, attn_sinks, rwkv_mu, rwkv_w0, rwkv_w2, rwkv_a0, rwkv_a2, rwkv_g2, rwkv_k_k, rwkv_k_a, rwkv_r_k, rwkv_gn_w, rwkv_gn_b, rwkv_v0, rwkv_v1, rwkv_v2, w_out, ln1_w, ln1_b, w_gate_up, w_down, ln2_w, ln2_b):
    bsz, t, d = x.shape
    assert bsz == 1 and d == D_MODEL
    depth = w_in.shape[0]
    rw = RWKV_WIDTH
    inv =ROPE_THETA ** (-jnp.arange(0, HEAD_DIM, 2, dtype=F32) / HEAD_DIM)
    ang = positions[0].astype(F32)[:, None] * inv
    cos, sin = jnp.cos(ang), jnp.sin(ang)
    cos_t = jnp.concatenate([cos, cos, cos, cos], axis=-1)
    sin_t = jnp.concatenate([-sin, sin, -sin, sin], axis=-1)

    w_pad = _pad_axis(w_in, 2, 0, N_PROJ).astype(BF16)
    sinks = attn_sinks.reshape(-1)
    mu3 = rwkv_mu[:, :3 * rw].reshape(depth, 3, rw)
    mu_lora = _pad_axis(rwkv_mu[:, 3 * rw:], 1, 0, LORA_PAD).reshape(depth, 1, LORA_PAD)
    zero = jnp.zeros((depth, rw), F32)
    v0_all = _pad_axis(rwkv_v0, 0, 1, depth)
    params = jnp.stack([rwkv_w0, rwkv_a0, rwkv_k_k, rwkv_k_a, rwkv_r_k.reshape(depth, rw), v0_all, zero, zero],
                       axis=1)
    w2p = _pad_axis(rwkv_w2, 1, 0, LANES)
    a2p = _pad_axis(rwkv_a2, 1, DECAY_LORA, LANES)
    g2p = _pad_axis(rwkv_g2, 1, 0, 2 * LANES)
    v1p = _pad_axis(rwkv_v1, 2, 0, LANES)
    v2p = _pad_axis(rwkv_v2, 1, 0, LANES)
    seg = np.arange(SLAB) // HEAD_DIM
    ones = jnp.asarray(seg[:, None] == seg[None, :], BF16)
    gn = jnp.stack([rwkv_gn_w, rwkv_gn_b], axis=1)
    ln = jnp.stack([ln1_w, ln1_b, ln2_w, ln2_b], axis=1)

    mod = _adaln(c, w_ada, b_ada)
    xs = x[0]
    v_first = None
    for l in range(depth):
        proj = _in_proj(xs, mod, w_pad, l)
        y_attn = _attention(proj, sinks, cos_t, sin_t, l)
        vres = None if l == 0 else (v_first, v1p, v2p)
        r, lw, k_mod, v, kk, b, g, bonus = _rwkv_prep(proj, mu3, mu_lora, params, w2p, a2p, g2p, ones, vres, l)
        if l == 0:
            v_first = v
        o, w_gu, w_dn, w_out_bf = _wkv(r, lw, k_mod, v, kk, b, w_gate_up, w_down, w_out, l)
        xs = _out_proj(y_attn, o, bonus, g, gn, ones, w_out_bf, xs, mod, ln, l)
        xs = _ffn(xs, mod, ln, w_gu, w_dn, l)
    return xs[None]
```

```python
import functools

import numpy as np
import jax
import jax.numpy as jnp
from jax import lax
from jax.experimental import pallas as pl
from jax.experimental.pallas import tpu as pltpu

F32 = jnp.float32
BF16 = jnp.bfloat16

D_MODEL = 2048
DEPTH = 2
HEAD_DIM = 64
HALF = HEAD_DIM // 2
ATTN_WIDTH = 1024
ATTN_HEADS = 16
KV_HEADS = 4
KV_WIDTH = KV_HEADS * HEAD_DIM
ATTN_COLS = ATTN_WIDTH + 2 * KV_WIDTH
WINDOW = 128
ROPE_THETA = 10000.0
RWKV_WIDTH = 1024
DECAY_LORA = 64
AAA_LORA = 64
GATE_LORA = 160
LORA_COLS = DECAY_LORA + AAA_LORA + GATE_LORA
MV_LORA = 32
RWKV_GN_EPS = 64e-5
FFN_HIDDEN = 5632
LN_EPS = 1e-5
N_MOD = 6
DEEPNORM_ALPHA = (2.0 * DEPTH) ** 0.25

LANES = 128
N_TILES = RWKV_WIDTH // LANES
CHUNK = 64
WKV_SUB = 2
OUT_PARTS = 2
VMEM_LIMIT = 56 * 1024 * 1024

COL_Q, COL_AK, COL_AV, COL_LORA = 0, 1024, 1280, 1536
COL_R = 2048
N_PROJ = 5120
LORA_PAD = 512
SLAB = 256
PROJ_TN = 512

NN = ((1,), (0,))
NT = ((1,), (1,))
TN = ((0,), (0,))


def _dot(a, b, dims=NN):
    return lax.dot_general(a, b, (dims, ((), ())), preferred_element_type=F32)


def _bf(z):
    return z.astype(BF16)


def _mm(a, b):
    return _dot(_bf(a), _bf(b))


def _cumsum_rows(tri_bf16, b):
    b0 = _bf(b)
    r1 = b - b0.astype(F32)
    b1 = _bf(r1)
    b2 = _bf(r1 - b1.astype(F32))
    return _dot(tri_bf16, b0) + (_dot(tri_bf16, b1) + _dot(tri_bf16, b2))


def _head_sum(a, ones_bf16):
    a0 = _bf(a)
    a1 = _bf(a - a0.astype(F32))
    return _dot(a0, ones_bf16) + _dot(a1, ones_bf16)


def _pair_dots(lhs, rhs):
    out = []
    for p in range(0, len(lhs), 2):
        r0, r1 = _bf(rhs[p]), _bf(rhs[p + 1])
        z = jnp.zeros_like(r0)
        bd = jnp.concatenate([jnp.concatenate([r0, z], axis=1), jnp.concatenate([z, r1], axis=1)], axis=0)
        res = _dot(jnp.concatenate([_bf(lhs[p]), _bf(lhs[p + 1])], axis=1), bd)
        n = r0.shape[1]
        out += [res[:, :n], res[:, n:]]
    return out


def _iota(shape, dim):
    return lax.broadcasted_iota(jnp.int32, shape, dim)


def _layer_norm(z, w, b):
    mu = jnp.mean(z, axis=-1, keepdims=True)
    zc = z - mu
    var = jnp.mean(zc * zc, axis=-1, keepdims=True)
    return zc * lax.rsqrt(var + LN_EPS) * w + b


def _params(**kw):
    return pltpu.CompilerParams(vmem_limit_bytes=VMEM_LIMIT, **kw)


def _adaln_kernel(c_ref, w_ref, b_ref, o_ref, *, tn):
    cb = c_ref[...]
    cond = cb * jax.nn.sigmoid(cb)
    for j in range(tn // LANES):
        sl = slice(j * LANES, (j + 1) * LANES)
        o_ref[:, sl] = jnp.sum(w_ref[:, sl] * cond, axis=0, keepdims=True) + b_ref[:, sl]


def _adaln(c, w_ada, b_ada):
    depth, d, n = w_ada.shape
    tn = 1024
    c_b = jnp.broadcast_to(c.reshape(d, 1), (d, LANES))
    out = pl.pallas_call(
        functools.partial(_adaln_kernel, tn=tn),
        grid=(depth, n // tn),
        in_specs=[pl.BlockSpec((d, LANES), lambda l, j: (0, 0)),
                  pl.BlockSpec((None, d, tn), lambda l, j: (l, 0, j)),
                  pl.BlockSpec((None, 1, tn), lambda l, j: (l, 0, j))],
        out_specs=pl.BlockSpec((None, 1, tn), lambda l, j: (l, 0, j)),
        out_shape=jax.ShapeDtypeStruct((depth, 1, n), F32),
        compiler_params=_params(dimension_semantics=("parallel", "parallel")),
        name="adaln",
    )(c_b, w_ada, b_ada.reshape(depth, 1, n))
    return out.reshape(depth, N_MOD, d)


def _in_proj_kernel(x_ref, mod_ref, w_ref, oa_ref, or_ref, u_ref, *, n_a):
    j = pl.program_id(1)

    @pl.when(j == 0)
    def _():
        u_ref[...] = _bf(x_ref[...] * (1.0 + mod_ref[1:2, :]) + mod_ref[0:1, :])

    @pl.when(j < n_a)
    def _():
        oa_ref[...] = _dot(u_ref[...], w_ref[...])

    @pl.when(j >= n_a)
    def _():
        or_ref[...] = _bf(_dot(u_ref[...], w_ref[...]))


def _in_proj(x, mod, w_pad, l):
    t, d = x.shape
    tm, tn = 1024, PROJ_TN
    n_a = COL_R // tn
    lora_out, lora_w = COL_LORA // tn, (ATTN_COLS + 3 * RWKV_WIDTH) // tn

    def w_block(j):
        return jnp.where(j < lora_out, j, jnp.where(j == lora_out, lora_w, j - 1))

    return pl.pallas_call(
        functools.partial(_in_proj_kernel, n_a=n_a),
        grid=(t // tm, N_PROJ // tn),
        in_specs=[pl.BlockSpec((tm, d), lambda i, j: (i, 0)),
                  pl.BlockSpec((None, N_MOD, d), lambda i, j: (l, 0, 0)),
                  pl.BlockSpec((None, d, tn), lambda i, j: (l, 0, w_block(j)))],
        out_specs=[pl.BlockSpec((tm, tn), lambda i, j: (i, jnp.minimum(j, n_a - 1))),
                   pl.BlockSpec((tm, tn), lambda i, j: (i, jnp.maximum(j - n_a, 0)))],
        out_shape=[jax.ShapeDtypeStruct((t, COL_R), F32), jax.ShapeDtypeStruct((t, N_PROJ - COL_R), BF16)],
        scratch_shapes=[pltpu.VMEM((tm, d), BF16)],
        compiler_params=_params(dimension_semantics=("parallel", "arbitrary")),
        name="in_proj",
    )(x, mod, w_pad)


def _rope(t, cos, sin, lo_half):
    rot = jnp.where(lo_half, pltpu.roll(t, LANES - HALF, axis=1), pltpu.roll(t, HALF, axis=1))
    return t * cos + rot * sin


def _attn_kernel(sink_ref, q_ref, kc_ref, kp_ref, vc_ref, vp_ref, cosc_ref, sinc_ref, cosp_ref, sinp_ref,
                 o_ref, *, l):
    blk = pl.program_id(0)
    w = WINDOW
    cos_c, sin_c = cosc_ref[...], sinc_ref[...]
    cos_p, sin_p = cosp_ref[...], sinp_ref[...]

    lane = _iota((w, LANES), 1)
    lo_half = (lane % HEAD_DIM) < HALF
    in_a = lane < HEAD_DIM
    lane2 = _iota((2 * w, LANES), 1)
    in_a2 = lane2 < HEAD_DIM

    qa = _iota((2 * w, 2 * w), 0) % w
    ks = _iota((2 * w, 2 * w), 1)
    valid = (ks > qa) & (ks <= qa + w) & ((ks >= w) | (blk > 0))
    row_is_a = _iota((2 * w, 1), 0) < w
    n_tiles = ATTN_HEADS // 2
    tiles = range(n_tiles)

    ones_kv = jnp.ones((2 * w, LANES), BF16)
    k_dup, v_dup = [], []
    for m in range(KV_HEADS // 2):
        sl_kv = slice(m * LANES, (m + 1) * LANES)
        k2 = jnp.concatenate([_rope(kp_ref[:, sl_kv], cos_p, sin_p, lo_half),
                              _rope(kc_ref[:, sl_kv], cos_c, sin_c, lo_half)], axis=0)
        v2 = jnp.concatenate([vp_ref[:, sl_kv], vc_ref[:, sl_kv]], axis=0)
        k2r, v2r = pltpu.roll(k2, HEAD_DIM, axis=1), pltpu.roll(v2, HEAD_DIM, axis=1)
        k_dup += [_bf(jnp.where(in_a2, k2, k2r)), _bf(jnp.where(in_a2, k2r, k2))]
        v_dup += [jnp.concatenate([_bf(jnp.where(in_a2, v2, v2r)), ones_kv], axis=1),
                  jnp.concatenate([_bf(jnp.where(in_a2, v2r, v2)), ones_kv], axis=1)]

    s = []
    for t in tiles:
        q = _rope(q_ref[:, t * LANES:(t + 1) * LANES], cos_c, sin_c, lo_half) * (HEAD_DIM ** -0.5)
        q_st = _bf(jnp.concatenate([jnp.where(in_a, q, 0.0), jnp.where(in_a, 0.0, q)], axis=0))
        s.append(jnp.where(valid, _dot(q_st, k_dup[t // 2], NT), -1e30))
    sink = [jnp.where(row_is_a, sink_ref[l * ATTN_HEADS + 2 * t], sink_ref[l * ATTN_HEADS + 2 * t + 1])
            for t in tiles]
    mx = [jnp.maximum(jnp.max(s[t], axis=-1, keepdims=True), sink[t]) for t in tiles]
    p = [jnp.exp(s[t] - mx[t]) for t in tiles]
    for t in tiles:
        ov = _dot(_bf(p[t]), v_dup[t // 2])
        o_st = ov[:, :LANES] * (1.0 / (ov[:, LANES:] + jnp.exp(sink[t] - mx[t])))
        o_ref[:, t * LANES:(t + 1) * LANES] = _bf(jnp.where(in_a, o_st[:w], o_st[w:]))


def _attention(proj, sinks, cos_t, sin_t, l):
    t = proj.shape[0]
    w = WINDOW
    prev = lambda i, s: (jnp.maximum(i - 1, 0), 0)
    cur = lambda i, s: (i, 0)
    return pl.pallas_call(
        functools.partial(_attn_kernel, l=l),
        grid_spec=pltpu.PrefetchScalarGridSpec(
            num_scalar_prefetch=1,
            grid=(t // w,),
            in_specs=[pl.BlockSpec((w, ATTN_WIDTH), lambda i, s: (i, COL_Q // ATTN_WIDTH)),
                      pl.BlockSpec((w, KV_WIDTH), lambda i, s: (i, COL_AK // KV_WIDTH)),
                      pl.BlockSpec((w, KV_WIDTH), lambda i, s: (jnp.maximum(i - 1, 0), COL_AK // KV_WIDTH)),
                      pl.BlockSpec((w, KV_WIDTH), lambda i, s: (i, COL_AV // KV_WIDTH)),
                      pl.BlockSpec((w, KV_WIDTH), lambda i, s: (jnp.maximum(i - 1, 0), COL_AV // KV_WIDTH)),
                      pl.BlockSpec((w, LANES), cur), pl.BlockSpec((w, LANES), cur),
                      pl.BlockSpec((w, LANES), prev), pl.BlockSpec((w, LANES), prev)],
            out_specs=pl.BlockSpec((w, ATTN_WIDTH), lambda i, s: (i, 0)),
        ),
        out_shape=jax.ShapeDtypeStruct((t, ATTN_WIDTH), BF16),
        compiler_params=_params(dimension_semantics=("parallel",)),
        name="attention",
    )(sinks, proj, proj, proj, proj, proj, cos_t, sin_t, cos_t, sin_t)


def _prep_kernel(*refs, has_vres):
    if has_vres:
        (r_ref, k_ref, v_ref, l_ref, rp_ref, kp_ref, vp_ref, lp_ref, mu_ref, mul_ref, par_ref,
         w2_ref, a2_ref, g2_ref, ones_ref, vf_ref, v1_ref, v2_ref,
         ro_ref, lw_ref, ko_ref, vo_ref, kk_ref, b_ref, g_ref, bon_ref) = refs
    else:
        (r_ref, k_ref, v_ref, l_ref, rp_ref, kp_ref, vp_ref, lp_ref, mu_ref, mul_ref, par_ref,
         w2_ref, a2_ref, g2_ref, ones_ref,
         ro_ref, lw_ref, ko_ref, vo_ref, kk_ref, b_ref, g_ref, bon_ref) = refs
    first = pl.program_id(0) == 0

    def shifted_lerp(x_ref, p_ref, mu):
        x = x_ref[...].astype(F32)
        last = p_ref.shape[0] - 1
        prev_row = jnp.where(first, 0.0, p_ref[last:last + 1, :].astype(F32))
        xr = pltpu.roll(x, 1, axis=0)
        row = _iota((8, x.shape[1]), 0)
        xp = jnp.concatenate([jnp.where(row == 0, prev_row, xr[0:8]), xr[8:]], axis=0)
        return x + (xp - x) * mu

    r = shifted_lerp(r_ref, rp_ref, mu_ref[0:1, :])
    k = shifted_lerp(k_ref, kp_ref, mu_ref[1:2, :])
    v = shifted_lerp(v_ref, vp_ref, mu_ref[2:3, :])
    lo = shifted_lerp(l_ref, lp_ref, mul_ref[...])

    w0, a0 = par_ref[0:1, :], par_ref[1:2, :]
    k_k, k_a, r_k = par_ref[2:3, :], par_ref[3:4, :], par_ref[4:5, :]

    wa = lo[:, 0:LANES]
    gd = lo[:, LANES:3 * LANES]
    log_decay = -float(np.exp(-0.5)) * jax.nn.sigmoid(w0 + _mm(jnp.tanh(wa), w2_ref[...]))
    a = jax.nn.sigmoid(a0 + _mm(wa, a2_ref[...]))
    g = _mm(jax.nn.sigmoid(gd), g2_ref[...])
    if has_vres:
        v0 = par_ref[5:6, :]
        mix = jax.nn.sigmoid(v0 + _mm(_mm(v, v1_ref[...]), v2_ref[...]))
        v = v + (vf_ref[...].astype(F32) - v) * mix

    kk = k * k_k
    k_mod = k * (1.0 + (a - 1.0) * k_a)
    rk = r * k_mod * r_k
    ones = ones_ref[...]
    for j in range(RWKV_WIDTH // SLAB):
        sl = slice(j * SLAB, (j + 1) * SLAB)
        kk_j = kk[:, sl]
        ss = _mm(kk_j * kk_j, ones)
        kk_n = kk_j * lax.rsqrt(jnp.maximum(ss, 1e-24))
        kk_ref[:, sl] = _bf(kk_n)
        b_ref[:, sl] = _bf(kk_n * a[:, sl])
        bon_ref[:, sl] = _bf(_mm(rk[:, sl], ones) * v[:, sl])
    ro_ref[...] = _bf(r)
    lw_ref[...] = log_decay
    ko_ref[...] = _bf(k_mod)
    vo_ref[...] = _bf(v)
    g_ref[...] = _bf(g)


def _rwkv_prep(proj_a, proj_r, mu3, mu_lora, params, w2p, a2p, g2p, ones, vres, l):
    t = proj_a.shape[0]
    tm = 256
    wdt = RWKV_WIDTH
    has_vres = vres is not None

    def col(cb, width):
        return pl.BlockSpec((tm, width), lambda i: (i, cb))

    def prev(cb, width, rows):
        return pl.BlockSpec((rows, width), lambda i: (jnp.maximum(i * (tm // rows) - 1, 0), cb))

    def layer(a, lyr=l):
        return pl.BlockSpec((None,) + a.shape[1:], lambda i: (lyr, 0, 0))

    in_specs = [col(0, wdt), col(1, wdt), col(2, wdt), col(COL_LORA // LORA_PAD, LORA_PAD),
                prev(0, wdt, 16), prev(1, wdt, 16), prev(2, wdt, 16), prev(COL_LORA // LORA_PAD, LORA_PAD, 8),
                layer(mu3), layer(mu_lora), layer(params), layer(w2p), layer(a2p), layer(g2p),
                pl.BlockSpec(ones.shape, lambda i: (0, 0))]
    args = [proj_r, proj_r, proj_r, proj_a, proj_r, proj_r, proj_r, proj_a, mu3, mu_lora, params, w2p, a2p, g2p, ones]
    if has_vres:
        v_first, v1p, v2p = vres
        in_specs += [pl.BlockSpec((tm, wdt), lambda i: (i, 0)), layer(v1p, l - 1), layer(v2p, l - 1)]
        args += [v_first, v1p, v2p]
    out_spec = pl.BlockSpec((tm, wdt), lambda i: (i, 0))
    out_dtypes = [BF16, F32, BF16, BF16, BF16, BF16, BF16, BF16]
    return pl.pallas_call(
        functools.partial(_prep_kernel, has_vres=has_vres),
        grid=(t // tm,),
        in_specs=in_specs,
        out_specs=[out_spec] * 8,
        out_shape=[jax.ShapeDtypeStruct((t, wdt), dt) for dt in out_dtypes],
        compiler_params=_params(dimension_semantics=("parallel",)),
        name="rwkv_prep",
    )(*args)


def _wkv_kernel(r_ref, lw_ref, k_ref, v_ref, kk_ref, b_ref, wgu_ref, wdn_ref, wout_ref,
                o_ref, wgu_o, wdn_o, wout_o, st_ref):
    c = CHUNK

    @pl.when(pl.program_id(0) == 0)
    def _():
        st_ref[...] = jnp.zeros_like(st_ref)

    wgu_o[...] = _bf(wgu_ref[...])
    wdn_o[...] = _bf(wdn_ref[...])
    wout_o[...] = _bf(wout_ref[...])

    row_t = _iota((c, c), 0)
    col_t = _iota((c, c), 1)
    l_incl = _bf(jnp.where(row_t >= col_t, 1.0, 0.0))

    lane = _iota((c, LANES), 1)
    m_a = lane < HEAD_DIM
    r2 = _iota((2 * c, 2 * c), 0)
    c2 = _iota((2 * c, 2 * c), 1)
    strict = r2 > c2
    incl = r2 >= c2
    eye = r2 == c2
    anti = (r2 < c) != (c2 < c)

    def stack(x):
        return jnp.concatenate([jnp.where(m_a, x, 0.0), jnp.where(m_a, 0.0, x)], axis=0)

    def stack_sw(x):
        return jnp.concatenate([jnp.where(m_a, 0.0, x), jnp.where(m_a, x, 0.0)], axis=0)

    tiles = range(N_TILES)
    sls = [slice(j * LANES, (j + 1) * LANES) for j in tiles]

    incl2 = jnp.concatenate([incl, incl], axis=1)
    blk = lambda s: (r2 // s) == (c2 // s)
    diag8 = strict & blk(8)

    def local_part(s, out):
        rows = slice(s * c, (s + 1) * c)
        cum_all = _cumsum_rows(l_incl, lw_ref[rows, :])
        a_st, r_st, rhs, v_sw, v_st, bk_tail, w_c = [], [], [], [], [], [], []
        for j in tiles:
            sl = sls[j]
            lw = lw_ref[rows, sl]
            cum = cum_all[:, sl]
            cum_last = cum[c - 1:c, :]
            e_neg = jnp.exp(-cum)
            e_tail = jnp.exp(cum_last - cum)
            k = k_ref[rows, sl].astype(F32)
            b = b_ref[rows, sl].astype(F32)
            a_st.append(stack(-kk_ref[rows, sl].astype(F32) * jnp.exp(cum - lw)))
            r_st.append(_bf(stack(r_ref[rows, sl].astype(F32) * jnp.exp(cum))))
            rhs.append(_bf(jnp.concatenate([stack(b * e_neg), stack(k * e_neg)], axis=0)))
            vs = pltpu.roll(v_ref[rows, sl].astype(F32), 64, axis=1)
            v_sw.append(_bf(vs))
            v_st.append(_bf(stack_sw(vs)))
            bk_tail.append(_bf(jnp.concatenate([b * e_tail, k * e_tail], axis=0)))
            w_c.append(jnp.exp(cum_last))
        yield
        d_ab, d_ak, d_r = [], [], []
        for j in tiles:
            gm = _dot(jnp.concatenate([_bf(a_st[j]), r_st[j]], axis=0), rhs[j], NT)
            d_ab.append(gm[:2 * c, :2 * c])
            d_ak.append(_bf(jnp.where(strict, gm[:2 * c, 2 * c:], 0.0)))
            d_r.append(_bf(jnp.where(incl2, gm[2 * c:, :], 0.0)))
        yield
        av = _pair_dots(d_ak, v_st)
        y0 = [a_st[j] + av[j] for j in tiles]
        d1 = [jnp.where(diag8, d_ab[j], 0.0) for j in tiles]
        p1 = [jnp.where(eye, 1.0, d1[j]) for j in tiles]
        d2 = _pair_dots(d1, d1)
        yield
        res = [_mm(d2[j], jnp.concatenate([p1[j], d2[j]], axis=1)) for j in tiles]
        p2 = [p1[j] + res[j][:, :2 * c] for j in tiles]
        yield
        d4p2 = _pair_dots([res[j][:, 2 * c:] for j in tiles], p2)
        tm = [p2[j] + d4p2[j] for j in tiles]
        for bs in (8, 16, 32):
            yield
            off = strict & blk(2 * bs) & jnp.logical_not(blk(bs))
            xm = _pair_dots(tm, [jnp.where(off, d_ab[j], 0.0) for j in tiles])
            yield
            xt = _pair_dots(xm, tm)
            tm = [tm[j] + xt[j] for j in tiles]
        yield
        out.update(y=_pair_dots(tm, y0), r_st=r_st, d_r=d_r, v_st=v_st, v_sw=v_sw, bk_tail=bk_tail, w_c=w_c)

    def state_part(s, loc, st):
        rows = slice(s * c, (s + 1) * c)
        ra, u_loc = [], []
        for j in tiles:
            yj = loc["y"][j]
            ah_st = jnp.concatenate([jnp.where(m_a, yj[:c], 0.0), jnp.where(m_a, 0.0, yj[c:])], axis=0)
            u_loc.append(jnp.concatenate([jnp.where(m_a, 0.0, yj[:c]), jnp.where(m_a, yj[c:], 0.0)], axis=0))
            ra.append(jnp.concatenate([loc["r_st"][j], _bf(ah_st)], axis=0))
        rs_as = _pair_dots(ra, st)
        u_st = [rs_as[j][2 * c:] + u_loc[j] for j in tiles]
        for j in tiles:
            o_st = rs_as[j][:2 * c] + _dot(loc["d_r"][j], jnp.concatenate([_bf(u_st[j]), loc["v_st"][j]], axis=0))
            o_ref[rows, sls[j]] = pltpu.roll(o_st[:c] + o_st[c:], 64, axis=1)
        new_st = []
        for j in tiles:
            u_sw = u_st[j][:c] + u_st[j][c:]
            delta = _dot(loc["bk_tail"][j], jnp.concatenate([_bf(u_sw), loc["v_sw"][j]], axis=0), TN)
            w_col = jnp.sum(jnp.where(eye, jnp.broadcast_to(loc["w_c"][j], (2 * c, 2 * c)), 0.0),
                            axis=1, keepdims=True)
            new_st.append(w_col * st[j] + jnp.where(anti, delta, 0.0))
        return new_st

    locs = [dict() for _ in range(WKV_SUB)]
    live = [local_part(s, locs[s]) for s in range(WKV_SUB)]
    while live:
        live = [g for g in live if next(g, "done") != "done"]
    st = [st_ref[j] for j in tiles]
    for s in range(WKV_SUB):
        st = state_part(s, locs[s], st)
    for j in tiles:
        st_ref[j] = st[j]


def _wkv(r, lw, k, v, kk, b, w_gate_up, w_down, w_out, l):
    t, wdt = r.shape
    rows = WKV_SUB * CHUNK
    steps = t // rows
    spec = pl.BlockSpec((rows, wdt), lambda i: (i, 0))

    def cast_specs(w, n_blocks):
        assert n_blocks <= steps and w.shape[1] % (16 * n_blocks) == 0
        blk = w.shape[1] // n_blocks
        return (pl.BlockSpec((None, blk, w.shape[2]), lambda i: (l, jnp.minimum(i, n_blocks - 1), 0)),
                pl.BlockSpec((blk, w.shape[2]), lambda i: (jnp.minimum(i, n_blocks - 1), 0)),
                jax.ShapeDtypeStruct(w.shape[1:], BF16))

    casts = [cast_specs(w_gate_up, 64), cast_specs(w_down, 32), cast_specs(w_out, 64)]
    return pl.pallas_call(
        _wkv_kernel,
        grid=(steps,),
        in_specs=[spec] * 6 + [c[0] for c in casts],
        out_specs=[spec] + [c[1] for c in casts],
        out_shape=[jax.ShapeDtypeStruct((t, wdt), F32)] + [c[2] for c in casts],
        scratch_shapes=[pltpu.VMEM((N_TILES, LANES, LANES), F32)],
        compiler_params=_params(dimension_semantics=("arbitrary",)),
        name="wkv",
    )(r, lw, k, v, kk, b, w_gate_up, w_down, w_out)


def _out_proj_kernel(ya_ref, o_ref, bon_ref, g_ref, gn_ref, ones_ref, w_ref, x_ref, mod_ref, ln_ref, out_ref,
                     *, parts):
    ones = ones_ref[...]
    pm = out_ref.shape[0] // parts
    slabs = [slice(j * SLAB, (j + 1) * SLAB) for j in range(RWKV_WIDTH // SLAB)]
    part_rows = [slice(p * pm, (p + 1) * pm) for p in range(parts)]
    oc = [[None] * len(slabs) for _ in range(parts)]
    var = [[None] * len(slabs) for _ in range(parts)]
    for p, rows in enumerate(part_rows):
        for j, sl in enumerate(slabs):
            o = o_ref[rows, sl]
            oc[p][j] = o - _head_sum(o, ones) * (1.0 / HEAD_DIM)
    for p in range(parts):
        for j in range(len(slabs)):
            var[p][j] = _mm(oc[p][j] * oc[p][j], ones) * (1.0 / HEAD_DIM)
    for p, rows in enumerate(part_rows):
        ys = []
        for j, sl in enumerate(slabs):
            y = ((oc[p][j] * lax.rsqrt(var[p][j] + RWKV_GN_EPS) * gn_ref[0:1, sl] + gn_ref[1:2, sl]
                  + bon_ref[rows, sl].astype(F32)) * g_ref[rows, sl].astype(F32))
            ys.append(_bf(y))
        mix = (_dot(ya_ref[rows, :], w_ref[:ATTN_WIDTH, :])
               + _dot(jnp.concatenate(ys, axis=1), w_ref[ATTN_WIDTH:, :]))
        z = DEEPNORM_ALPHA * x_ref[rows, :] + mod_ref[2:3, :] * mix
        out_ref[rows, :] = _layer_norm(z, ln_ref[0:1, :], ln_ref[1:2, :])


def _out_proj(y_attn, o, bonus, g, gn, ones, w_cat, x, mod, ln, l):
    t, d = x.shape
    tm = 512
    wdt = RWKV_WIDTH
    row = lambda width: pl.BlockSpec((tm, width), lambda i: (i, 0))
    layer = lambda a: pl.BlockSpec((None,) + a.shape[1:], lambda i: (l, 0, 0))
    return pl.pallas_call(
        functools.partial(_out_proj_kernel, parts=OUT_PARTS),
        grid=(t // tm,),
        in_specs=[row(ATTN_WIDTH), row(wdt), row(wdt), row(wdt), layer(gn),
                  pl.BlockSpec(ones.shape, lambda i: (0, 0)), pl.BlockSpec(w_cat.shape, lambda i: (0, 0)), row(d),
                  layer(mod), layer(ln)],
        out_specs=row(d),
        out_shape=jax.ShapeDtypeStruct((t, d), F32),
        compiler_params=_params(dimension_semantics=("parallel",)),
        name="out_proj",
    )(y_attn, o, bonus, g, gn, ones, w_cat, x, mod, ln)


def _ffn_kernel(x_ref, mod_ref, ln_ref, wg_ref, wu_ref, wd_ref, out_ref, u_ref, acc_ref):
    j = pl.program_id(1)

    @pl.when(j == 0)
    def _():
        u_ref[...] = _bf(x_ref[...] * (1.0 + mod_ref[4:5, :]) + mod_ref[3:4, :])
        acc_ref[...] = jnp.zeros_like(acc_ref)

    u = u_ref[...]
    gate = _dot(u, wg_ref[...])
    up = _dot(u, wu_ref[...])
    h = _bf(gate * jax.nn.sigmoid(gate) * up)
    acc_ref[...] += _dot(h, wd_ref[...])

    @pl.when(j == pl.num_programs(1) - 1)
    def _():
        z = DEEPNORM_ALPHA * x_ref[...] + mod_ref[5:6, :] * acc_ref[...]
        out_ref[...] = _layer_norm(z, ln_ref[2:3, :], ln_ref[3:4, :])


def _ffn(x, mod, ln, w_gu, w_dn, l):
    t, d = x.shape
    f = w_dn.shape[0]
    tm, tf = 512, 512
    nf = f // tf
    return pl.pallas_call(
        _ffn_kernel,
        grid=(t // tm, nf),
        in_specs=[pl.BlockSpec((tm, d), lambda i, j: (i, 0)),
                  pl.BlockSpec((None,) + mod.shape[1:], lambda i, j: (l, 0, 0)),
                  pl.BlockSpec((None,) + ln.shape[1:], lambda i, j: (l, 0, 0)),
                  pl.BlockSpec((d, tf), lambda i, j: (0, j)),
                  pl.BlockSpec((d, tf), lambda i, j: (0, j + nf)),
                  pl.BlockSpec((tf, d), lambda i, j: (j, 0))],
        out_specs=pl.BlockSpec((tm, d), lambda i, j: (i, 0)),
        out_shape=jax.ShapeDtypeStruct((t, d), F32),
        scratch_shapes=[pltpu.VMEM((tm, d), BF16), pltpu.VMEM((tm, d), F32)],
        compiler_params=_params(dimension_semantics=("parallel", "arbitrary")),
        name="ffn",
    )(x, mod, ln, w_gu, w_gu, w_dn)


def _pad_axis(a, axis, before, total):
    pads = [(0, 0)] * a.ndim
    pads[axis] = (before, total - before - a.shape[axis])
    return jnp.pad(a, pads)


def kernel(x, c, positions, w_ada, b_ada, w_in, attn_sinks, rwkv_mu, rwkv_w0, rwkv_w2, rwkv_a0, rwkv_a2, rwkv_g2, rwkv_k_k, rwkv_k_a, rwkv_r_k, rwkv_gn_w, rwkv_gn_b, rwkv_v0, rwkv_v1, rwkv_v2, w_out, ln1_w, ln1_b, w_gate_up, w_down, ln2_w, ln2_b):
    bsz, t, d = x.shape
    assert bsz == 1 and d == D_MODEL
    depth = w_in.shape[0]
    rw = RWKV_WIDTH
    inv =ROPE_THETA ** (-jnp.arange(0, HEAD_DIM, 2, dtype=F32) / HEAD_DIM)
    ang = positions[0].astype(F32)[:, None] * inv
    cos, sin = jnp.cos(ang), jnp.sin(ang)
    cos_t = jnp.concatenate([cos, cos, cos, cos], axis=-1)
    sin_t = jnp.concatenate([-sin, sin, -sin, sin], axis=-1)

    w_pad = _pad_axis(w_in, 2, 0, N_PROJ).astype(BF16)
    sinks = attn_sinks.reshape(-1)
    mu3 = rwkv_mu[:, :3 * rw].reshape(depth, 3, rw)
    mu_lora = _pad_axis(rwkv_mu[:, 3 * rw:], 1, 0, LORA_PAD).reshape(depth, 1, LORA_PAD)
    zero = jnp.zeros((depth, rw), F32)
    v0_all = _pad_axis(rwkv_v0, 0, 1, depth)
    params = jnp.stack([rwkv_w0, rwkv_a0, rwkv_k_k, rwkv_k_a, rwkv_r_k.reshape(depth, rw), v0_all, zero, zero],
                       axis=1)
    w2p = _pad_axis(rwkv_w2, 1, 0, LANES)
    a2p = _pad_axis(rwkv_a2, 1, DECAY_LORA, LANES)
    g2p = _pad_axis(rwkv_g2, 1, 0, 2 * LANES)
    v1p = _pad_axis(rwkv_v1, 2, 0, LANES)
    v2p = _pad_axis(rwkv_v2, 1, 0, LANES)
    seg = np.arange(SLAB) // HEAD_DIM
    ones = jnp.asarray(seg[:, None] == seg[None, :], BF16)
    gn = jnp.stack([rwkv_gn_w, rwkv_gn_b], axis=1)
    ln = jnp.stack([ln1_w, ln1_b, ln2_w, ln2_b], axis=1)

    mod = _adaln(c, w_ada, b_ada)
    xs = x[0]
    v_first = None
    for l in range(depth):
        proj_a, proj_r = _in_proj(xs, mod, w_pad, l)
        y_attn = _attention(proj_a, sinks, cos_t, sin_t, l)
        vres = None if l == 0 else (v_first, v1p, v2p)
        r, lw, k_mod, v, kk, b, g, bonus = _rwkv_prep(proj_a, proj_r, mu3, mu_lora, params, w2p, a2p, g2p, ones,
                                                      vres, l)
        if l == 0:
            v_first = v
        o, w_gu, w_dn, w_out_bf = _wkv(r, lw, k_mod, v, kk, b, w_gate_up, w_down, w_out, l)
        xs = _out_proj(y_attn, o, bonus, g, gn, ones, w_out_bf, xs, mod, ln, l)
        xs = _ffn(xs, mod, ln, w_gu, w_dn, l)
    return xs[None]
```

```python
import functools

import numpy as np
import jax
import jax.numpy as jnp
from jax import lax
from jax.experimental import pallas as pl
from jax.experimental.pallas import tpu as pltpu

F32 = jnp.float32
BF16 = jnp.bfloat16

D_MODEL = 2048
DEPTH = 2
HEAD_DIM = 64
HALF = HEAD_DIM // 2
ATTN_WIDTH = 1024
ATTN_HEADS = 16
KV_HEADS = 4
KV_WIDTH = KV_HEADS * HEAD_DIM
ATTN_COLS = ATTN_WIDTH + 2 * KV_WIDTH
WINDOW = 128
ROPE_THETA = 10000.0
RWKV_WIDTH = 1024
DECAY_LORA = 64
AAA_LORA = 64
GATE_LORA = 160
LORA_COLS = DECAY_LORA + AAA_LORA + GATE_LORA
MV_LORA = 32
RWKV_GN_EPS = 64e-5
FFN_HIDDEN = 5632
LN_EPS = 1e-5
N_MOD = 6
DEEPNORM_ALPHA = (2.0 * DEPTH) ** 0.25

LANES = 128
N_TILES = RWKV_WIDTH // LANES
CHUNK = 64
WKV_SUB = 2
OUT_PARTS = 2
VMEM_LIMIT = 56 * 1024 * 1024

COL_Q, COL_AK, COL_AV, COL_LORA = 0, 1024, 1280, 1536
COL_R = 2048
N_PROJ = 5120
LORA_PAD = 512
SLAB = 256
PROJ_TN = 512

NN = ((1,), (0,))
NT = ((1,), (1,))
TN = ((0,), (0,))


def _dot(a, b, dims=NN):
    return lax.dot_general(a, b, (dims, ((), ())), preferred_element_type=F32)


def _bf(z):
    return z.astype(BF16)


def _mm(a, b):
    return _dot(_bf(a), _bf(b))


def _cumsum_rows(tri_bf16, b):
    b0 = _bf(b)
    r1 = b - b0.astype(F32)
    b1 = _bf(r1)
    b2 = _bf(r1 - b1.astype(F32))
    return _dot(tri_bf16, b0) + (_dot(tri_bf16, b1) + _dot(tri_bf16, b2))


def _head_sum(a, ones_bf16):
    a0 = _bf(a)
    a1 = _bf(a - a0.astype(F32))
    return _dot(a0, ones_bf16) + _dot(a1, ones_bf16)


def _pair_dots(lhs, rhs):
    out = []
    for p in range(0, len(lhs), 2):
        r0, r1 = _bf(rhs[p]), _bf(rhs[p + 1])
        z = jnp.zeros_like(r0)
        bd = jnp.concatenate([jnp.concatenate([r0, z], axis=1), jnp.concatenate([z, r1], axis=1)], axis=0)
        res = _dot(jnp.concatenate([_bf(lhs[p]), _bf(lhs[p + 1])], axis=1), bd)
        n = r0.shape[1]
        out += [res[:, :n], res[:, n:]]
    return out


def _iota(shape, dim):
    return lax.broadcasted_iota(jnp.int32, shape, dim)


def _layer_norm(z, w, b):
    mu = jnp.mean(z, axis=-1, keepdims=True)
    zc = z - mu
    var = jnp.mean(zc * zc, axis=-1, keepdims=True)
    return zc * lax.rsqrt(var + LN_EPS) * w + b


def _params(**kw):
    return pltpu.CompilerParams(vmem_limit_bytes=VMEM_LIMIT, **kw)


def _adaln_kernel(c_ref, w_ref, b_ref, o_ref, *, tn):
    cb = c_ref[...]
    cond = cb * jax.nn.sigmoid(cb)
    for j in range(tn // LANES):
        sl = slice(j * LANES, (j + 1) * LANES)
        o_ref[:, sl] = jnp.sum(w_ref[:, sl] * cond, axis=0, keepdims=True) + b_ref[:, sl]


def _adaln(c, w_ada, b_ada):
    depth, d, n = w_ada.shape
    tn = 2048
    c_b = jnp.broadcast_to(c.reshape(d, 1), (d, LANES))
    out = pl.pallas_call(
        functools.partial(_adaln_kernel, tn=tn),
        grid=(depth, n // tn),
        in_specs=[pl.BlockSpec((d, LANES), lambda l, j: (0, 0)),
                  pl.BlockSpec((None, d, tn), lambda l, j: (l, 0, j)),
                  pl.BlockSpec((None, 1, tn), lambda l, j: (l, 0, j))],
        out_specs=pl.BlockSpec((None, 1, tn), lambda l, j: (l, 0, j)),
        out_shape=jax.ShapeDtypeStruct((depth, 1, n), F32),
        compiler_params=_params(dimension_semantics=("parallel", "parallel")),
        name="adaln",
    )(c_b, w_ada, b_ada.reshape(depth, 1, n))
    return out.reshape(depth, N_MOD, d)


def _in_proj_kernel(x_ref, mod_ref, w_ref, oa_ref, or_ref, u_ref, *, n_a):
    j = pl.program_id(1)

    @pl.when(j == 0)
    def _():
        u_ref[...] = _bf(x_ref[...] * (1.0 + mod_ref[1:2, :]) + mod_ref[0:1, :])

    @pl.when(j < n_a)
    def _():
        oa_ref[...] = _dot(u_ref[...], w_ref[...])

    @pl.when(j >= n_a)
    def _():
        or_ref[...] = _bf(_dot(u_ref[...], w_ref[...]))


def _in_proj(x, mod, w_pad, l):
    t, d = x.shape
    tm, tn = 1024, PROJ_TN
    n_a = COL_R // tn
    lora_out, lora_w = COL_LORA // tn, (ATTN_COLS + 3 * RWKV_WIDTH) // tn

    def w_block(j):
        return jnp.where(j < lora_out, j, jnp.where(j == lora_out, lora_w, j - 1))

    return pl.pallas_call(
        functools.partial(_in_proj_kernel, n_a=n_a),
        grid=(t // tm, N_PROJ // tn),
        in_specs=[pl.BlockSpec((tm, d), lambda i, j: (i, 0)),
                  pl.BlockSpec((None, N_MOD, d), lambda i, j: (l, 0, 0)),
                  pl.BlockSpec((None, d, tn), lambda i, j: (l, 0, w_block(j)))],
        out_specs=[pl.BlockSpec((tm, tn), lambda i, j: (i, jnp.minimum(j, n_a - 1))),
                   pl.BlockSpec((tm, tn), lambda i, j: (i, jnp.maximum(j - n_a, 0)))],
        out_shape=[jax.ShapeDtypeStruct((t, COL_R), F32), jax.ShapeDtypeStruct((t, N_PROJ - COL_R), BF16)],
        scratch_shapes=[pltpu.VMEM((tm, d), BF16)],
        compiler_params=_params(dimension_semantics=("parallel", "arbitrary")),
        name="in_proj",
    )(x, mod, w_pad)


def _rope(t, cos, sin, lo_half):
    rot = jnp.where(lo_half, pltpu.roll(t, LANES - HALF, axis=1), pltpu.roll(t, HALF, axis=1))
    return t * cos + rot * sin


def _attn_kernel(sink_ref, q_ref, kc_ref, kp_ref, vc_ref, vp_ref, cosc_ref, sinc_ref, cosp_ref, sinp_ref,
                 o_ref, *, l):
    blk = pl.program_id(0)
    w = WINDOW
    cos_c, sin_c = cosc_ref[...], sinc_ref[...]
    cos_p, sin_p = cosp_ref[...], sinp_ref[...]

    lane = _iota((w, LANES), 1)
    lo_half = (lane % HEAD_DIM) < HALF
    in_a = lane < HEAD_DIM
    lane2 = _iota((2 * w, LANES), 1)
    in_a2 = lane2 < HEAD_DIM

    qa = _iota((2 * w, 2 * w), 0) % w
    ks = _iota((2 * w, 2 * w), 1)
    valid = (ks > qa) & (ks <= qa + w) & ((ks >= w) | (blk > 0))
    row_is_a = _iota((2 * w, 1), 0) < w
    n_tiles = ATTN_HEADS // 2
    tiles = range(n_tiles)

    ones_kv = jnp.ones((2 * w, LANES), BF16)
    k_dup, v_dup = [], []
    for m in range(KV_HEADS // 2):
        sl_kv = slice(m * LANES, (m + 1) * LANES)
        k2 = jnp.concatenate([_rope(kp_ref[:, sl_kv], cos_p, sin_p, lo_half),
                              _rope(kc_ref[:, sl_kv], cos_c, sin_c, lo_half)], axis=0)
        v2 = jnp.concatenate([vp_ref[:, sl_kv], vc_ref[:, sl_kv]], axis=0)
        k2r, v2r = pltpu.roll(k2, HEAD_DIM, axis=1), pltpu.roll(v2, HEAD_DIM, axis=1)
        k_dup += [_bf(jnp.where(in_a2, k2, k2r)), _bf(jnp.where(in_a2, k2r, k2))]
        v_dup += [jnp.concatenate([_bf(jnp.where(in_a2, v2, v2r)), ones_kv], axis=1),
                  jnp.concatenate([_bf(jnp.where(in_a2, v2r, v2)), ones_kv], axis=1)]

    s = []
    for t in tiles:
        q = _rope(q_ref[:, t * LANES:(t + 1) * LANES], cos_c, sin_c, lo_half) * (HEAD_DIM ** -0.5)
        q_st = _bf(jnp.concatenate([jnp.where(in_a, q, 0.0), jnp.where(in_a, 0.0, q)], axis=0))
        s.append(jnp.where(valid, _dot(q_st, k_dup[t // 2], NT), -1e30))
    sink = [jnp.where(row_is_a, sink_ref[l * ATTN_HEADS + 2 * t], sink_ref[l * ATTN_HEADS + 2 * t + 1])
            for t in tiles]
    mx = [jnp.maximum(jnp.max(s[t], axis=-1, keepdims=True), sink[t]) for t in tiles]
    p = [jnp.exp(s[t] - mx[t]) for t in tiles]
    for t in tiles:
        ov = _dot(_bf(p[t]), v_dup[t // 2])
        o_st = ov[:, :LANES] * (1.0 / (ov[:, LANES:] + jnp.exp(sink[t] - mx[t])))
        o_ref[:, t * LANES:(t + 1) * LANES] = _bf(jnp.where(in_a, o_st[:w], o_st[w:]))


def _attention(proj, sinks, cos_t, sin_t, l):
    t = proj.shape[0]
    w = WINDOW
    prev = lambda i, s: (jnp.maximum(i - 1, 0), 0)
    cur = lambda i, s: (i, 0)
    return pl.pallas_call(
        functools.partial(_attn_kernel, l=l),
        grid_spec=pltpu.PrefetchScalarGridSpec(
            num_scalar_prefetch=1,
            grid=(t // w,),
            in_specs=[pl.BlockSpec((w, ATTN_WIDTH), lambda i, s: (i, COL_Q // ATTN_WIDTH)),
                      pl.BlockSpec((w, KV_WIDTH), lambda i, s: (i, COL_AK // KV_WIDTH)),
                      pl.BlockSpec((w, KV_WIDTH), lambda i, s: (jnp.maximum(i - 1, 0), COL_AK // KV_WIDTH)),
                      pl.BlockSpec((w, KV_WIDTH), lambda i, s: (i, COL_AV // KV_WIDTH)),
                      pl.BlockSpec((w, KV_WIDTH), lambda i, s: (jnp.maximum(i - 1, 0), COL_AV // KV_WIDTH)),
                      pl.BlockSpec((w, LANES), cur), pl.BlockSpec((w, LANES), cur),
                      pl.BlockSpec((w, LANES), prev), pl.BlockSpec((w, LANES), prev)],
            out_specs=pl.BlockSpec((w, ATTN_WIDTH), lambda i, s: (i, 0)),
        ),
        out_shape=jax.ShapeDtypeStruct((t, ATTN_WIDTH), BF16),
        compiler_params=_params(dimension_semantics=("parallel",)),
        name="attention",
    )(sinks, proj, proj, proj, proj, proj, cos_t, sin_t, cos_t, sin_t)


def _prep_kernel(*refs, has_vres):
    if has_vres:
        (r_ref, k_ref, v_ref, l_ref, rp_ref, kp_ref, vp_ref, lp_ref, mu_ref, mul_ref, par_ref,
         w2_ref, a2_ref, g2_ref, ones_ref, vf_ref, v1_ref, v2_ref,
         ro_ref, lw_ref, ko_ref, vo_ref, kk_ref, b_ref, g_ref, bon_ref) = refs
    else:
        (r_ref, k_ref, v_ref, l_ref, rp_ref, kp_ref, vp_ref, lp_ref, mu_ref, mul_ref, par_ref,
         w2_ref, a2_ref, g2_ref, ones_ref,
         ro_ref, lw_ref, ko_ref, vo_ref, kk_ref, b_ref, g_ref, bon_ref) = refs
    first = pl.program_id(0) == 0

    def shifted_lerp(x_ref, p_ref, mu):
        x = x_ref[...].astype(F32)
        last = p_ref.shape[0] - 1
        prev_row = jnp.where(first, 0.0, p_ref[last:last + 1, :].astype(F32))
        xr = pltpu.roll(x, 1, axis=0)
        row = _iota((8, x.shape[1]), 0)
        xp = jnp.concatenate([jnp.where(row == 0, prev_row, xr[0:8]), xr[8:]], axis=0)
        return x + (xp - x) * mu

    r = shifted_lerp(r_ref, rp_ref, mu_ref[0:1, :])
    k = shifted_lerp(k_ref, kp_ref, mu_ref[1:2, :])
    v = shifted_lerp(v_ref, vp_ref, mu_ref[2:3, :])
    lo = shifted_lerp(l_ref, lp_ref, mul_ref[...])

    w0, a0 = par_ref[0:1, :], par_ref[1:2, :]
    k_k, k_a, r_k = par_ref[2:3, :], par_ref[3:4, :], par_ref[4:5, :]

    wa = lo[:, 0:LANES]
    gd = lo[:, LANES:3 * LANES]
    log_decay = -float(np.exp(-0.5)) * jax.nn.sigmoid(w0 + _mm(jnp.tanh(wa), w2_ref[...]))
    a = jax.nn.sigmoid(a0 + _mm(wa, a2_ref[...]))
    g = _mm(jax.nn.sigmoid(gd), g2_ref[...])
    if has_vres:
        v0 = par_ref[5:6, :]
        mix = jax.nn.sigmoid(v0 + _mm(_mm(v, v1_ref[...]), v2_ref[...]))
        v = v + (vf_ref[...].astype(F32) - v) * mix

    kk = k * k_k
    k_mod = k * (1.0 + (a - 1.0) * k_a)
    rk = r * k_mod * r_k
    ones = ones_ref[...]
    for j in range(RWKV_WIDTH // SLAB):
        sl = slice(j * SLAB, (j + 1) * SLAB)
        kk_j = kk[:, sl]
        ss = _mm(kk_j * kk_j, ones)
        kk_n = kk_j * lax.rsqrt(jnp.maximum(ss, 1e-24))
        kk_ref[:, sl] = _bf(kk_n)
        b_ref[:, sl] = _bf(kk_n * a[:, sl])
        bon_ref[:, sl] = _bf(_mm(rk[:, sl], ones) * v[:, sl])
    ro_ref[...] = _bf(r)
    lw_ref[...] = log_decay
    ko_ref[...] = _bf(k_mod)
    vo_ref[...] = _bf(v)
    g_ref[...] = _bf(g)


def _rwkv_prep(proj_a, proj_r, mu3, mu_lora, params, w2p, a2p, g2p, ones, vres, l):
    t = proj_a.shape[0]
    tm = 256
    wdt = RWKV_WIDTH
    has_vres = vres is not None

    def col(cb, width):
        return pl.BlockSpec((tm, width), lambda i: (i, cb))

    def prev(cb, width, rows):
        return pl.BlockSpec((rows, width), lambda i: (jnp.maximum(i * (tm // rows) - 1, 0), cb))

    def layer(a, lyr=l):
        return pl.BlockSpec((None,) + a.shape[1:], lambda i: (lyr, 0, 0))

    in_specs = [col(0, wdt), col(1, wdt), col(2, wdt), col(COL_LORA // LORA_PAD, LORA_PAD),
                prev(0, wdt, 16), prev(1, wdt, 16), prev(2, wdt, 16), prev(COL_LORA // LORA_PAD, LORA_PAD, 8),
                layer(mu3), layer(mu_lora), layer(params), layer(w2p), layer(a2p), layer(g2p),
                pl.BlockSpec(ones.shape, lambda i: (0, 0))]
    args = [proj_r, proj_r, proj_r, proj_a, proj_r, proj_r, proj_r, proj_a, mu3, mu_lora, params, w2p, a2p, g2p, ones]
    if has_vres:
        v_first, v1p, v2p = vres
        in_specs += [pl.BlockSpec((tm, wdt), lambda i: (i, 0)), layer(v1p, l - 1), layer(v2p, l - 1)]
        args += [v_first, v1p, v2p]
    out_spec = pl.BlockSpec((tm, wdt), lambda i: (i, 0))
    out_dtypes = [BF16, F32, BF16, BF16, BF16, BF16, BF16, BF16]
    return pl.pallas_call(
        functools.partial(_prep_kernel, has_vres=has_vres),
        grid=(t // tm,),
        in_specs=in_specs,
        out_specs=[out_spec] * 8,
        out_shape=[jax.ShapeDtypeStruct((t, wdt), dt) for dt in out_dtypes],
        compiler_params=_params(dimension_semantics=("parallel",)),
        name="rwkv_prep",
    )(*args)


def _wkv_kernel(r_ref, lw_ref, k_ref, v_ref, kk_ref, b_ref, wgu_ref, wdn_ref, wout_ref,
                o_ref, wgu_o, wdn_o, wout_o, st_ref):
    c = CHUNK

    @pl.when(pl.program_id(0) == 0)
    def _():
        st_ref[...] = jnp.zeros_like(st_ref)

    wgu_o[...] = _bf(wgu_ref[...])
    wdn_o[...] = _bf(wdn_ref[...])
    wout_o[...] = _bf(wout_ref[...])

    row_t = _iota((c, c), 0)
    col_t = _iota((c, c), 1)
    l_incl = _bf(jnp.where(row_t >= col_t, 1.0, 0.0))

    lane = _iota((c, LANES), 1)
    m_a = lane < HEAD_DIM
    r2 = _iota((2 * c, 2 * c), 0)
    c2 = _iota((2 * c, 2 * c), 1)
    strict = r2 > c2
    incl = r2 >= c2
    eye = r2 == c2
    anti = (r2 < c) != (c2 < c)

    def stack(x):
        return jnp.concatenate([jnp.where(m_a, x, 0.0), jnp.where(m_a, 0.0, x)], axis=0)

    def stack_sw(x):
        return jnp.concatenate([jnp.where(m_a, 0.0, x), jnp.where(m_a, x, 0.0)], axis=0)

    tiles = range(N_TILES)
    sls = [slice(j * LANES, (j + 1) * LANES) for j in tiles]

    incl2 = jnp.concatenate([incl, incl], axis=1)
    blk = lambda s: (r2 // s) == (c2 // s)
    diag8 = strict & blk(8)

    def local_part(s, out):
        rows = slice(s * c, (s + 1) * c)
        cum_all = _cumsum_rows(l_incl, lw_ref[rows, :])
        a_st, r_st, rhs, v_sw, v_st, bk_tail, w_c = [], [], [], [], [], [], []
        for j in tiles:
            sl = sls[j]
            lw = lw_ref[rows, sl]
            cum = cum_all[:, sl]
            cum_last = cum[c - 1:c, :]
            e_neg = jnp.exp(-cum)
            e_tail = jnp.exp(cum_last - cum)
            k = k_ref[rows, sl].astype(F32)
            b = b_ref[rows, sl].astype(F32)
            a_st.append(stack(-kk_ref[rows, sl].astype(F32) * jnp.exp(cum - lw)))
            r_st.append(_bf(stack(r_ref[rows, sl].astype(F32) * jnp.exp(cum))))
            rhs.append(_bf(jnp.concatenate([stack(b * e_neg), stack(k * e_neg)], axis=0)))
            vs = pltpu.roll(v_ref[rows, sl].astype(F32), 64, axis=1)
            v_sw.append(_bf(vs))
            v_st.append(_bf(stack_sw(vs)))
            bk_tail.append(_bf(jnp.concatenate([b * e_tail, k * e_tail], axis=0)))
            w_c.append(jnp.exp(cum_last))
        yield
        d_ab, d_ak, d_r = [], [], []
        for j in tiles:
            gm = _dot(jnp.concatenate([_bf(a_st[j]), r_st[j]], axis=0), rhs[j], NT)
            d_ab.append(gm[:2 * c, :2 * c])
            d_ak.append(_bf(jnp.where(strict, gm[:2 * c, 2 * c:], 0.0)))
            d_r.append(_bf(jnp.where(incl2, gm[2 * c:, :], 0.0)))
        yield
        av = _pair_dots(d_ak, v_st)
        y0 = [a_st[j] + av[j] for j in tiles]
        d1 = [jnp.where(diag8, d_ab[j], 0.0) for j in tiles]
        p1 = [jnp.where(eye, 1.0, d1[j]) for j in tiles]
        d2 = _pair_dots(d1, d1)
        yield
        res = [_mm(d2[j], jnp.concatenate([p1[j], d2[j]], axis=1)) for j in tiles]
        p2 = [p1[j] + res[j][:, :2 * c] for j in tiles]
        yield
        d4p2 = _pair_dots([res[j][:, 2 * c:] for j in tiles], p2)
        tm = [p2[j] + d4p2[j] for j in tiles]
        for bs in (8, 16, 32):
            yield
            off = strict & blk(2 * bs) & jnp.logical_not(blk(bs))
            xm = _pair_dots(tm, [jnp.where(off, d_ab[j], 0.0) for j in tiles])
            yield
            xt = _pair_dots(xm, tm)
            tm = [tm[j] + xt[j] for j in tiles]
        yield
        out.update(y=_pair_dots(tm, y0), r_st=r_st, d_r=d_r, v_st=v_st, v_sw=v_sw, bk_tail=bk_tail, w_c=w_c)

    def state_part(s, loc, st):
        rows = slice(s * c, (s + 1) * c)
        ra, u_loc = [], []
        for j in tiles:
            yj = loc["y"][j]
            ah_st = jnp.concatenate([jnp.where(m_a, yj[:c], 0.0), jnp.where(m_a, 0.0, yj[c:])], axis=0)
            u_loc.append(jnp.concatenate([jnp.where(m_a, 0.0, yj[:c]), jnp.where(m_a, yj[c:], 0.0)], axis=0))
            ra.append(jnp.concatenate([loc["r_st"][j], _bf(ah_st)], axis=0))
        rs_as = _pair_dots(ra, st)
        u_st = [rs_as[j][2 * c:] + u_loc[j] for j in tiles]
        for j in tiles:
            o_st = rs_as[j][:2 * c] + _dot(loc["d_r"][j], jnp.concatenate([_bf(u_st[j]), loc["v_st"][j]], axis=0))
            o_ref[rows, sls[j]] = pltpu.roll(o_st[:c] + o_st[c:], 64, axis=1)
        new_st = []
        for j in tiles:
            u_sw = u_st[j][:c] + u_st[j][c:]
            delta = _dot(loc["bk_tail"][j], jnp.concatenate([_bf(u_sw), loc["v_sw"][j]], axis=0), TN)
            w_col = jnp.sum(jnp.where(eye, jnp.broadcast_to(loc["w_c"][j], (2 * c, 2 * c)), 0.0),
                            axis=1, keepdims=True)
            new_st.append(w_col * st[j] + jnp.where(anti, delta, 0.0))
        return new_st

    locs = [dict() for _ in range(WKV_SUB)]
    live = [local_part(s, locs[s]) for s in range(WKV_SUB)]
    while live:
        live = [g for g in live if next(g, "done") != "done"]
    st = [st_ref[j] for j in tiles]
    for s in range(WKV_SUB):
        st = state_part(s, locs[s], st)
    for j in tiles:
        st_ref[j] = st[j]


def _wkv(r, lw, k, v, kk, b, w_gate_up, w_down, w_out, l):
    t, wdt = r.shape
    rows = WKV_SUB * CHUNK
    steps = t // rows
    spec = pl.BlockSpec((rows, wdt), lambda i: (i, 0))

    def cast_specs(w, n_blocks):
        assert n_blocks <= steps and w.shape[1] % (16 * n_blocks) == 0
        blk = w.shape[1] // n_blocks
        return (pl.BlockSpec((None, blk, w.shape[2]), lambda i: (l, jnp.minimum(i, n_blocks - 1), 0)),
                pl.BlockSpec((blk, w.shape[2]), lambda i: (jnp.minimum(i, n_blocks - 1), 0)),
                jax.ShapeDtypeStruct(w.shape[1:], BF16))

    casts = [cast_specs(w_gate_up, 64), cast_specs(w_down, 32), cast_specs(w_out, 64)]
    return pl.pallas_call(
        _wkv_kernel,
        grid=(steps,),
        in_specs=[spec] * 6 + [c[0] for c in casts],
        out_specs=[spec] + [c[1] for c in casts],
        out_shape=[jax.ShapeDtypeStruct((t, wdt), F32)] + [c[2] for c in casts],
        scratch_shapes=[pltpu.VMEM((N_TILES, LANES, LANES), F32)],
        compiler_params=_params(dimension_semantics=("arbitrary",)),
        name="wkv",
    )(r, lw, k, v, kk, b, w_gate_up, w_down, w_out)


def _out_proj_kernel(ya_ref, o_ref, bon_ref, g_ref, gn_ref, ones_ref, w_ref, x_ref, mod_ref, ln_ref, out_ref,
                     *, parts):
    ones = ones_ref[...]
    pm = out_ref.shape[0] // parts
    slabs = [slice(j * SLAB, (j + 1) * SLAB) for j in range(RWKV_WIDTH // SLAB)]
    part_rows = [slice(p * pm, (p + 1) * pm) for p in range(parts)]
    oc = [[None] * len(slabs) for _ in range(parts)]
    var = [[None] * len(slabs) for _ in range(parts)]
    for p, rows in enumerate(part_rows):
        for j, sl in enumerate(slabs):
            o = o_ref[rows, sl]
            oc[p][j] = o - _head_sum(o, ones) * (1.0 / HEAD_DIM)
    for p in range(parts):
        for j in range(len(slabs)):
            var[p][j] = _mm(oc[p][j] * oc[p][j], ones) * (1.0 / HEAD_DIM)
    for p, rows in enumerate(part_rows):
        ys = []
        for j, sl in enumerate(slabs):
            y = ((oc[p][j] * lax.rsqrt(var[p][j] + RWKV_GN_EPS) * gn_ref[0:1, sl] + gn_ref[1:2, sl]
                  + bon_ref[rows, sl].astype(F32)) * g_ref[rows, sl].astype(F32))
            ys.append(_bf(y))
        mix = (_dot(ya_ref[rows, :], w_ref[:ATTN_WIDTH, :])
               + _dot(jnp.concatenate(ys, axis=1), w_ref[ATTN_WIDTH:, :]))
        z = DEEPNORM_ALPHA * x_ref[rows, :] + mod_ref[2:3, :] * mix
        out_ref[rows, :] = _layer_norm(z, ln_ref[0:1, :], ln_ref[1:2, :])


def _out_proj(y_attn, o, bonus, g, gn, ones, w_cat, x, mod, ln, l):
    t, d = x.shape
    tm = 512
    wdt = RWKV_WIDTH
    row = lambda width: pl.BlockSpec((tm, width), lambda i: (i, 0))
    layer = lambda a: pl.BlockSpec((None,) + a.shape[1:], lambda i: (l, 0, 0))
    return pl.pallas_call(
        functools.partial(_out_proj_kernel, parts=OUT_PARTS),
        grid=(t // tm,),
        in_specs=[row(ATTN_WIDTH), row(wdt), row(wdt), row(wdt), layer(gn),
                  pl.BlockSpec(ones.shape, lambda i: (0, 0)), pl.BlockSpec(w_cat.shape, lambda i: (0, 0)), row(d),
                  layer(mod), layer(ln)],
        out_specs=row(d),
        out_shape=jax.ShapeDtypeStruct((t, d), F32),
        compiler_params=_params(dimension_semantics=("parallel",)),
        name="out_proj",
    )(y_attn, o, bonus, g, gn, ones, w_cat, x, mod, ln)


def _ffn_kernel(x_ref, mod_ref, ln_ref, wg_ref, wu_ref, wd_ref, out_ref, u_ref, acc_ref):
    j = pl.program_id(1)

    @pl.when(j == 0)
    def _():
        u_ref[...] = _bf(x_ref[...] * (1.0 + mod_ref[4:5, :]) + mod_ref[3:4, :])
        acc_ref[...] = jnp.zeros_like(acc_ref)

    u = u_ref[...]
    gate = _dot(u, wg_ref[...])
    up = _dot(u, wu_ref[...])
    h = _bf(gate * jax.nn.sigmoid(gate) * up)
    acc_ref[...] += _dot(h, wd_ref[...])

    @pl.when(j == pl.num_programs(1) - 1)
    def _():
        z = DEEPNORM_ALPHA * x_ref[...] + mod_ref[5:6, :] * acc_ref[...]
        out_ref[...] = _layer_norm(z, ln_ref[2:3, :], ln_ref[3:4, :])


def _ffn(x, mod, ln, w_gu, w_dn, l):
    t, d = x.shape
    f = w_dn.shape[0]
    tm, tf = 512, 512
    nf = f // tf
    return pl.pallas_call(
        _ffn_kernel,
        grid=(t // tm, nf),
        in_specs=[pl.BlockSpec((tm, d), lambda i, j: (i, 0)),
                  pl.BlockSpec((None,) + mod.shape[1:], lambda i, j: (l, 0, 0)),
                  pl.BlockSpec((None,) + ln.shape[1:], lambda i, j: (l, 0, 0)),
                  pl.BlockSpec((d, tf), lambda i, j: (0, j)),
                  pl.BlockSpec((d, tf), lambda i, j: (0, j + nf)),
                  pl.BlockSpec((tf, d), lambda i, j: (j, 0))],
        out_specs=pl.BlockSpec((tm, d), lambda i, j: (i, 0)),
        out_shape=jax.ShapeDtypeStruct((t, d), F32),
        scratch_shapes=[pltpu.VMEM((tm, d), BF16), pltpu.VMEM((tm, d), F32)],
        compiler_params=_params(dimension_semantics=("parallel", "arbitrary")),
        name="ffn",
    )(x, mod, ln, w_gu, w_gu, w_dn)


def _pad_axis(a, axis, before, total):
    pads = [(0, 0)] * a.ndim
    pads[axis] = (before, total - before - a.shape[axis])
    return jnp.pad(a, pads)


def kernel(x, c, positions, w_ada, b_ada, w_in, attn_sinks, rwkv_mu, rwkv_w0, rwkv_w2, rwkv_a0, rwkv_a2, rwkv_g2, rwkv_k_k, rwkv_k_a, rwkv_r_k, rwkv_gn_w, rwkv_gn_b, rwkv_v0, rwkv_v1, rwkv_v2, w_out, ln1_w, ln1_b, w_gate_up, w_down, ln2_w, ln2_b):
    bsz, t, d = x.shape
    assert bsz == 1 and d == D_MODEL
    depth = w_in.shape[0]
    rw = RWKV_WIDTH
    inv =ROPE_THETA ** (-jnp.arange(0, HEAD_DIM, 2, dtype=F32) / HEAD_DIM)
    ang = positions[0].astype(F32)[:, None] * inv
    cos, sin = jnp.cos(ang), jnp.sin(ang)
    cos_t = jnp.concatenate([cos, cos, cos, cos], axis=-1)
    sin_t = jnp.concatenate([-sin, sin, -sin, sin], axis=-1)

    w_pad = jnp.concatenate([w_in.astype(BF16), jnp.zeros((depth, d, N_PROJ - w_in.shape[2]), BF16)], axis=2)
    sinks = attn_sinks.reshape(-1)
    mu3 = rwkv_mu[:, :3 * rw].reshape(depth, 3, rw)
    mu_lora = _pad_axis(rwkv_mu[:, 3 * rw:], 1, 0, LORA_PAD).reshape(depth, 1, LORA_PAD)
    zero = jnp.zeros((depth, rw), F32)
    v0_all = _pad_axis(rwkv_v0, 0, 1, depth)
    params = jnp.stack([rwkv_w0, rwkv_a0, rwkv_k_k, rwkv_k_a, rwkv_r_k.reshape(depth, rw), v0_all, zero, zero],
                       axis=1)
    w2p = _pad_axis(rwkv_w2, 1, 0, LANES)
    a2p = _pad_axis(rwkv_a2, 1, DECAY_LORA, LANES)
    g2p = _pad_axis(rwkv_g2, 1, 0, 2 * LANES)
    v1p = _pad_axis(rwkv_v1, 2, 0, LANES)
    v2p = _pad_axis(rwkv_v2, 1, 0, LANES)
    seg = np.arange(SLAB) // HEAD_DIM
    ones = jnp.asarray(seg[:, None] == seg[None, :], BF16)
    gn = jnp.stack([rwkv_gn_w, rwkv_gn_b], axis=1)
    ln = jnp.stack([ln1_w, ln1_b, ln2_w, ln2_b], axis=1)

    mod = _adaln(c, w_ada, b_ada)
    xs = x[0]
    v_first = None
    for l in range(depth):
        proj_a, proj_r = _in_proj(xs, mod, w_pad, l)
        y_attn = _attention(proj_a, sinks, cos_t, sin_t, l)
        vres = None if l == 0 else (v_first, v1p, v2p)
        r, lw, k_mod, v, kk, b, g, bonus = _rwkv_prep(proj_a, proj_r, mu3, mu_lora, params, w2p, a2p, g2p, ones,
                                                      vres, l)
        if l == 0:
            v_first = v
        o, w_gu, w_dn, w_out_bf = _wkv(r, lw, k_mod, v, kk, b, w_gate_up, w_down, w_out, l)
        xs = _out_proj(y_attn, o, bonus, g, gn, ones, w_out_bf, xs, mod, ln, l)
        xs = _ffn(xs, mod, ln, w_gu, w_dn, l)
    return xs[None]
```

```python
import functools

import numpy as np
import jax
import jax.numpy as jnp
from jax import lax
from jax.experimental import pallas as pl
from jax.experimental.pallas import tpu as pltpu

F32 = jnp.float32
BF16 = jnp.bfloat16

D_MODEL = 2048
DEPTH = 2
HEAD_DIM = 64
HALF = HEAD_DIM // 2
ATTN_WIDTH = 1024
ATTN_HEADS = 16
KV_HEADS = 4
KV_WIDTH = KV_HEADS * HEAD_DIM
ATTN_COLS = ATTN_WIDTH + 2 * KV_WIDTH
WINDOW = 128
ROPE_THETA = 10000.0
RWKV_WIDTH = 1024
DECAY_LORA = 64
AAA_LORA = 64
GATE_LORA = 160
LORA_COLS = DECAY_LORA + AAA_LORA + GATE_LORA
MV_LORA = 32
RWKV_GN_EPS = 64e-5
FFN_HIDDEN = 5632
LN_EPS = 1e-5
N_MOD = 6
DEEPNORM_ALPHA = (2.0 * DEPTH) ** 0.25

LANES = 128
N_TILES = RWKV_WIDTH // LANES
CHUNK = 64
WKV_SUB = 2
OUT_PARTS = 2
VMEM_LIMIT = 56 * 1024 * 1024

COL_Q, COL_AK, COL_AV, COL_LORA = 0, 1024, 1280, 1536
COL_R = 2048
N_PROJ = 5120
LORA_PAD = 512
SLAB = 256
PROJ_TN = 512

NN = ((1,), (0,))
NT = ((1,), (1,))
TN = ((0,), (0,))


def _dot(a, b, dims=NN):
    return lax.dot_general(a, b, (dims, ((), ())), preferred_element_type=F32)


def _bf(z):
    return z.astype(BF16)


def _mm(a, b):
    return _dot(_bf(a), _bf(b))


def _cumsum_rows(tri_bf16, b):
    b0 = _bf(b)
    r1 = b - b0.astype(F32)
    b1 = _bf(r1)
    b2 = _bf(r1 - b1.astype(F32))
    return _dot(tri_bf16, b0) + (_dot(tri_bf16, b1) + _dot(tri_bf16, b2))


def _head_sum(a, ones_bf16):
    a0 = _bf(a)
    a1 = _bf(a - a0.astype(F32))
    return _dot(a0, ones_bf16) + _dot(a1, ones_bf16)


def _pair_dots(lhs, rhs):
    out = []
    for p in range(0, len(lhs), 2):
        r0, r1 = _bf(rhs[p]), _bf(rhs[p + 1])
        z = jnp.zeros_like(r0)
        bd = jnp.concatenate([jnp.concatenate([r0, z], axis=1), jnp.concatenate([z, r1], axis=1)], axis=0)
        res = _dot(jnp.concatenate([_bf(lhs[p]), _bf(lhs[p + 1])], axis=1), bd)
        n = r0.shape[1]
        out += [res[:, :n], res[:, n:]]
    return out


def _iota(shape, dim):
    return lax.broadcasted_iota(jnp.int32, shape, dim)


def _layer_norm(z, w, b):
    mu = jnp.mean(z, axis=-1, keepdims=True)
    zc = z - mu
    var = jnp.mean(zc * zc, axis=-1, keepdims=True)
    return zc * lax.rsqrt(var + LN_EPS) * w + b


def _params(**kw):
    return pltpu.CompilerParams(vmem_limit_bytes=VMEM_LIMIT, **kw)


def _adaln_kernel(c_ref, w_ref, b_ref, o_ref, *, tn):
    cb = c_ref[...]
    cond = cb * jax.nn.sigmoid(cb)
    for j in range(tn // LANES):
        sl = slice(j * LANES, (j + 1) * LANES)
        o_ref[:, sl] = jnp.sum(w_ref[:, sl] * cond, axis=0, keepdims=True) + b_ref[:, sl]


def _adaln(c, w_ada, b_ada):
    depth, d, n = w_ada.shape
    tn = 1024
    c_b = jnp.broadcast_to(c.reshape(d, 1), (d, LANES))
    out = pl.pallas_call(
        functools.partial(_adaln_kernel, tn=tn),
        grid=(depth, n // tn),
        in_specs=[pl.BlockSpec((d, LANES), lambda l, j: (0, 0)),
                  pl.BlockSpec((None, d, tn), lambda l, j: (l, 0, j)),
                  pl.BlockSpec((None, 1, tn), lambda l, j: (l, 0, j))],
        out_specs=pl.BlockSpec((None, 1, tn), lambda l, j: (l, 0, j)),
        out_shape=jax.ShapeDtypeStruct((depth, 1, n), F32),
        compiler_params=_params(dimension_semantics=("parallel", "parallel")),
        name="adaln",
    )(c_b, w_ada, b_ada.reshape(depth, 1, n))
    return out.reshape(depth, N_MOD, d)


def _in_proj_kernel(x_ref, mod_ref, w_ref, oa_ref, or_ref, u_ref, *, n_a):
    j = pl.program_id(1)

    @pl.when(j == 0)
    def _():
        u_ref[...] = _bf(x_ref[...] * (1.0 + mod_ref[1:2, :]) + mod_ref[0:1, :])

    @pl.when(j < n_a)
    def _():
        oa_ref[...] = _dot(u_ref[...], w_ref[...])

    @pl.when(j >= n_a)
    def _():
        or_ref[...] = _bf(_dot(u_ref[...], w_ref[...]))


def _in_proj(x, mod, w_pad, l):
    t, d = x.shape
    tm, tn = 1024, PROJ_TN
    n_a = COL_R // tn
    lora_out, lora_w = COL_LORA // tn, (ATTN_COLS + 3 * RWKV_WIDTH) // tn

    def w_block(j):
        return jnp.where(j < lora_out, j, jnp.where(j == lora_out, lora_w, j - 1))

    return pl.pallas_call(
        functools.partial(_in_proj_kernel, n_a=n_a),
        grid=(t // tm, N_PROJ // tn),
        in_specs=[pl.BlockSpec((tm, d), lambda i, j: (i, 0)),
                  pl.BlockSpec((None, N_MOD, d), lambda i, j: (l, 0, 0)),
                  pl.BlockSpec((None, d, tn), lambda i, j: (l, 0, w_block(j)))],
        out_specs=[pl.BlockSpec((tm, tn), lambda i, j: (i, jnp.minimum(j, n_a - 1))),
                   pl.BlockSpec((tm, tn), lambda i, j: (i, jnp.maximum(j - n_a, 0)))],
        out_shape=[jax.ShapeDtypeStruct((t, COL_R), F32), jax.ShapeDtypeStruct((t, N_PROJ - COL_R), BF16)],
        scratch_shapes=[pltpu.VMEM((tm, d), BF16)],
        compiler_params=_params(dimension_semantics=("parallel", "arbitrary")),
        name="in_proj",
    )(x, mod, w_pad)


def _rope(t, cos, sin, lo_half):
    rot = jnp.where(lo_half, pltpu.roll(t, LANES - HALF, axis=1), pltpu.roll(t, HALF, axis=1))
    return t * cos + rot * sin


def _attn_kernel(sink_ref, q_ref, kc_ref, vc_ref, cosc_ref, sinc_ref, o_ref, kprev_ref, vprev_ref, *, l):
    blk = pl.program_id(0)
    w = WINDOW
    cos_c, sin_c = cosc_ref[...], sinc_ref[...]

    @pl.when(blk == 0)
    def _():
        kprev_ref[...] = jnp.zeros_like(kprev_ref)
        vprev_ref[...] = jnp.zeros_like(vprev_ref)

    lane = _iota((w, LANES), 1)
    lo_half = (lane % HEAD_DIM) < HALF
    in_a = lane < HEAD_DIM

    qa = _iota((2 * w, 2 * w), 0) % w
    ks = _iota((2 * w, 2 * w), 1)
    valid = (ks > qa) & (ks <= qa + w) & ((ks >= w) | (blk > 0))
    row_is_a = _iota((2 * w, 1), 0) < w
    n_tiles = ATTN_HEADS // 2
    tiles = range(n_tiles)

    ones_kv = jnp.ones((2 * w, LANES), BF16)
    k_dup, v_dup = [], []
    for m in range(KV_HEADS // 2):
        sl_kv = slice(m * LANES, (m + 1) * LANES)
        k1 = _rope(kc_ref[:, sl_kv], cos_c, sin_c, lo_half)
        v1 = vc_ref[:, sl_kv]
        k1r, v1r = pltpu.roll(k1, HEAD_DIM, axis=1), pltpu.roll(v1, HEAD_DIM, axis=1)
        k_cur = [_bf(jnp.where(in_a, k1, k1r)), _bf(jnp.where(in_a, k1r, k1))]
        v_cur = [_bf(jnp.where(in_a, v1, v1r)), _bf(jnp.where(in_a, v1r, v1))]
        for h in range(2):
            g = 2 * m + h
            k_dup.append(jnp.concatenate([kprev_ref[g], k_cur[h]], axis=0))
            v_dup.append(jnp.concatenate([jnp.concatenate([vprev_ref[g], v_cur[h]], axis=0), ones_kv], axis=1))
            kprev_ref[g] = k_cur[h]
            vprev_ref[g] = v_cur[h]

    s = []
    for t in tiles:
        q = _rope(q_ref[:, t * LANES:(t + 1) * LANES], cos_c, sin_c, lo_half) * (HEAD_DIM ** -0.5)
        q_st = _bf(jnp.concatenate([jnp.where(in_a, q, 0.0), jnp.where(in_a, 0.0, q)], axis=0))
        s.append(jnp.where(valid, _dot(q_st, k_dup[t // 2], NT), -1e30))
    sink = [jnp.where(row_is_a, sink_ref[l * ATTN_HEADS + 2 * t], sink_ref[l * ATTN_HEADS + 2 * t + 1])
            for t in tiles]
    mx = [jnp.maximum(jnp.max(s[t], axis=-1, keepdims=True), sink[t]) for t in tiles]
    p = [jnp.exp(s[t] - mx[t]) for t in tiles]
    for t in tiles:
        ov = _dot(_bf(p[t]), v_dup[t // 2])
        o_st = ov[:, :LANES] * (1.0 / (ov[:, LANES:] + jnp.exp(sink[t] - mx[t])))
        o_ref[:, t * LANES:(t + 1) * LANES] = _bf(jnp.where(in_a, o_st[:w], o_st[w:]))


def _attention(proj, sinks, cos_t, sin_t, l):
    t = proj.shape[0]
    w = WINDOW
    cur = lambda i, s: (i, 0)
    return pl.pallas_call(
        functools.partial(_attn_kernel, l=l),
        grid_spec=pltpu.PrefetchScalarGridSpec(
            num_scalar_prefetch=1,
            grid=(t // w,),
            in_specs=[pl.BlockSpec((w, ATTN_WIDTH), lambda i, s: (i, COL_Q // ATTN_WIDTH)),
                      pl.BlockSpec((w, KV_WIDTH), lambda i, s: (i, COL_AK // KV_WIDTH)),
                      pl.BlockSpec((w, KV_WIDTH), lambda i, s: (i, COL_AV // KV_WIDTH)),
                      pl.BlockSpec((w, LANES), cur), pl.BlockSpec((w, LANES), cur)],
            out_specs=pl.BlockSpec((w, ATTN_WIDTH), lambda i, s: (i, 0)),
            scratch_shapes=[pltpu.VMEM((KV_HEADS, w, LANES), BF16), pltpu.VMEM((KV_HEADS, w, LANES), BF16)],
        ),
        out_shape=jax.ShapeDtypeStruct((t, ATTN_WIDTH), BF16),
        compiler_params=_params(dimension_semantics=("arbitrary",)),
        name="attention",
    )(sinks, proj, proj, proj, cos_t, sin_t)


def _prep_kernel(*refs, has_vres):
    if has_vres:
        (r_ref, k_ref, v_ref, l_ref, rp_ref, kp_ref, vp_ref, lp_ref, mu_ref, mul_ref, par_ref,
         w2_ref, a2_ref, g2_ref, ones_ref, vf_ref, v1_ref, v2_ref,
         ro_ref, lw_ref, ko_ref, vo_ref, kk_ref, b_ref, g_ref, bon_ref) = refs
    else:
        (r_ref, k_ref, v_ref, l_ref, rp_ref, kp_ref, vp_ref, lp_ref, mu_ref, mul_ref, par_ref,
         w2_ref, a2_ref, g2_ref, ones_ref,
         ro_ref, lw_ref, ko_ref, vo_ref, kk_ref, b_ref, g_ref, bon_ref) = refs
    first = pl.program_id(0) == 0

    def shifted_lerp(x_ref, p_ref, mu):
        x = x_ref[...].astype(F32)
        last = p_ref.shape[0] - 1
        prev_row = jnp.where(first, 0.0, p_ref[last:last + 1, :].astype(F32))
        xr = pltpu.roll(x, 1, axis=0)
        row = _iota((8, x.shape[1]), 0)
        xp = jnp.concatenate([jnp.where(row == 0, prev_row, xr[0:8]), xr[8:]], axis=0)
        return x + (xp - x) * mu

    r = shifted_lerp(r_ref, rp_ref, mu_ref[0:1, :])
    k = shifted_lerp(k_ref, kp_ref, mu_ref[1:2, :])
    v = shifted_lerp(v_ref, vp_ref, mu_ref[2:3, :])
    lo = shifted_lerp(l_ref, lp_ref, mul_ref[...])

    w0, a0 = par_ref[0:1, :], par_ref[1:2, :]
    k_k, k_a, r_k = par_ref[2:3, :], par_ref[3:4, :], par_ref[4:5, :]

    wa = lo[:, 0:LANES]
    gd = lo[:, LANES:3 * LANES]
    log_decay = -float(np.exp(-0.5)) * jax.nn.sigmoid(w0 + _mm(jnp.tanh(wa), w2_ref[...]))
    a = jax.nn.sigmoid(a0 + _mm(wa, a2_ref[...]))
    g = _mm(jax.nn.sigmoid(gd), g2_ref[...])
    if has_vres:
        v0 = par_ref[5:6, :]
        mix = jax.nn.sigmoid(v0 + _mm(_mm(v, v1_ref[...]), v2_ref[...]))
        v = v + (vf_ref[...].astype(F32) - v) * mix

    kk = k * k_k
    k_mod = k * (1.0 + (a - 1.0) * k_a)
    rk = r * k_mod * r_k
    ones = ones_ref[...]
    for j in range(RWKV_WIDTH // SLAB):
        sl = slice(j * SLAB, (j + 1) * SLAB)
        kk_j = kk[:, sl]
        ss = _mm(kk_j * kk_j, ones)
        kk_n = kk_j * lax.rsqrt(jnp.maximum(ss, 1e-24))
        kk_ref[:, sl] = _bf(kk_n)
        b_ref[:, sl] = _bf(kk_n * a[:, sl])
        bon_ref[:, sl] = _bf(_mm(rk[:, sl], ones) * v[:, sl])
    ro_ref[...] = _bf(r)
    lw_ref[...] = log_decay
    ko_ref[...] = _bf(k_mod)
    vo_ref[...] = _bf(v)
    g_ref[...] = _bf(g)


def _rwkv_prep(proj_a, proj_r, mu3, mu_lora, params, w2p, a2p, g2p, ones, vres, l):
    t = proj_a.shape[0]
    tm = 256
    wdt = RWKV_WIDTH
    has_vres = vres is not None

    def col(cb, width):
        return pl.BlockSpec((tm, width), lambda i: (i, cb))

    def prev(cb, width, rows):
        return pl.BlockSpec((rows, width), lambda i: (jnp.maximum(i * (tm // rows) - 1, 0), cb))

    def layer(a, lyr=l):
        return pl.BlockSpec((None,) + a.shape[1:], lambda i: (lyr, 0, 0))

    in_specs = [col(0, wdt), col(1, wdt), col(2, wdt), col(COL_LORA // LORA_PAD, LORA_PAD),
                prev(0, wdt, 16), prev(1, wdt, 16), prev(2, wdt, 16), prev(COL_LORA // LORA_PAD, LORA_PAD, 8),
                layer(mu3), layer(mu_lora), layer(params), layer(w2p), layer(a2p), layer(g2p),
                pl.BlockSpec(ones.shape, lambda i: (0, 0))]
    args = [proj_r, proj_r, proj_r, proj_a, proj_r, proj_r, proj_r, proj_a, mu3, mu_lora, params, w2p, a2p, g2p, ones]
    if has_vres:
        v_first, v1p, v2p = vres
        in_specs += [pl.BlockSpec((tm, wdt), lambda i: (i, 0)), layer(v1p, l - 1), layer(v2p, l - 1)]
        args += [v_first, v1p, v2p]
    out_spec = pl.BlockSpec((tm, wdt), lambda i: (i, 0))
    out_dtypes = [BF16, F32, BF16, BF16, BF16, BF16, BF16, BF16]
    return pl.pallas_call(
        functools.partial(_prep_kernel, has_vres=has_vres),
        grid=(t // tm,),
        in_specs=in_specs,
        out_specs=[out_spec] * 8,
        out_shape=[jax.ShapeDtypeStruct((t, wdt), dt) for dt in out_dtypes],
        compiler_params=_params(dimension_semantics=("parallel",)),
        name="rwkv_prep",
    )(*args)


def _wkv_kernel(r_ref, lw_ref, k_ref, v_ref, kk_ref, b_ref, wgu_ref, wdn_ref, wout_ref,
                o_ref, wgu_o, wdn_o, wout_o, st_ref):
    c = CHUNK

    @pl.when(pl.program_id(0) == 0)
    def _():
        st_ref[...] = jnp.zeros_like(st_ref)

    wgu_o[...] = _bf(wgu_ref[...])
    wdn_o[...] = _bf(wdn_ref[...])
    wout_o[...] = _bf(wout_ref[...])

    row_t = _iota((c, c), 0)
    col_t = _iota((c, c), 1)
    l_incl = _bf(jnp.where(row_t >= col_t, 1.0, 0.0))

    lane = _iota((c, LANES), 1)
    m_a = lane < HEAD_DIM
    r2 = _iota((2 * c, 2 * c), 0)
    c2 = _iota((2 * c, 2 * c), 1)
    strict = r2 > c2
    incl = r2 >= c2
    eye = r2 == c2
    anti = (r2 < c) != (c2 < c)

    def stack(x):
        return jnp.concatenate([jnp.where(m_a, x, 0.0), jnp.where(m_a, 0.0, x)], axis=0)

    def stack_sw(x):
        return jnp.concatenate([jnp.where(m_a, 0.0, x), jnp.where(m_a, x, 0.0)], axis=0)

    tiles = range(N_TILES)
    sls = [slice(j * LANES, (j + 1) * LANES) for j in tiles]

    incl2 = jnp.concatenate([incl, incl], axis=1)
    blk = lambda s: (r2 // s) == (c2 // s)
    diag8 = strict & blk(8)

    def local_part(s, out):
        rows = slice(s * c, (s + 1) * c)
        cum_all = _cumsum_rows(l_incl, lw_ref[rows, :])
        a_st, r_st, rhs, v_sw, v_st, bk_tail, w_c = [], [], [], [], [], [], []
        for j in tiles:
            sl = sls[j]
            lw = lw_ref[rows, sl]
            cum = cum_all[:, sl]
            cum_last = cum[c - 1:c, :]
            e_neg = jnp.exp(-cum)
            e_tail = jnp.exp(cum_last - cum)
            k = k_ref[rows, sl].astype(F32)
            b = b_ref[rows, sl].astype(F32)
            a_st.append(stack(-kk_ref[rows, sl].astype(F32) * jnp.exp(cum - lw)))
            r_st.append(_bf(stack(r_ref[rows, sl].astype(F32) * jnp.exp(cum))))
            rhs.append(_bf(jnp.concatenate([stack(b * e_neg), stack(k * e_neg)], axis=0)))
            vs = pltpu.roll(v_ref[rows, sl].astype(F32), 64, axis=1)
            v_sw.append(_bf(vs))
            v_st.append(_bf(stack_sw(vs)))
            bk_tail.append(_bf(jnp.concatenate([b * e_tail, k * e_tail], axis=0)))
            w_c.append(jnp.exp(cum_last))
        yield
        d_ab, d_ak, d_r = [], [], []
        for j in tiles:
            gm = _dot(jnp.concatenate([_bf(a_st[j]), r_st[j]], axis=0), rhs[j], NT)
            d_ab.append(gm[:2 * c, :2 * c])
            d_ak.append(_bf(jnp.where(strict, gm[:2 * c, 2 * c:], 0.0)))
            d_r.append(_bf(jnp.where(incl2, gm[2 * c:, :], 0.0)))
        yield
        av = _pair_dots(d_ak, v_st)
        y0 = [a_st[j] + av[j] for j in tiles]
        d1 = [jnp.where(diag8, d_ab[j], 0.0) for j in tiles]
        p1 = [jnp.where(eye, 1.0, d1[j]) for j in tiles]
        d2 = _pair_dots(d1, d1)
        yield
        res = [_mm(d2[j], jnp.concatenate([p1[j], d2[j]], axis=1)) for j in tiles]
        p2 = [p1[j] + res[j][:, :2 * c] for j in tiles]
        yield
        d4p2 = _pair_dots([res[j][:, 2 * c:] for j in tiles], p2)
        tm = [p2[j] + d4p2[j] for j in tiles]
        for bs in (8, 16, 32):
            yield
            off = strict & blk(2 * bs) & jnp.logical_not(blk(bs))
            xm = _pair_dots(tm, [jnp.where(off, d_ab[j], 0.0) for j in tiles])
            yield
            xt = _pair_dots(xm, tm)
            tm = [tm[j] + xt[j] for j in tiles]
        yield
        out.update(y=_pair_dots(tm, y0), r_st=r_st, d_r=d_r, v_st=v_st, v_sw=v_sw, bk_tail=bk_tail, w_c=w_c)

    def state_part(s, loc, st):
        rows = slice(s * c, (s + 1) * c)
        ra, u_loc = [], []
        for j in tiles:
            yj = loc["y"][j]
            ah_st = jnp.concatenate([jnp.where(m_a, yj[:c], 0.0), jnp.where(m_a, 0.0, yj[c:])], axis=0)
            u_loc.append(jnp.concatenate([jnp.where(m_a, 0.0, yj[:c]), jnp.where(m_a, yj[c:], 0.0)], axis=0))
            ra.append(jnp.concatenate([loc["r_st"][j], _bf(ah_st)], axis=0))
        rs_as = _pair_dots(ra, st)
        u_st = [rs_as[j][2 * c:] + u_loc[j] for j in tiles]
        for j in tiles:
            o_st = rs_as[j][:2 * c] + _dot(loc["d_r"][j], jnp.concatenate([_bf(u_st[j]), loc["v_st"][j]], axis=0))
            o_ref[rows, sls[j]] = pltpu.roll(o_st[:c] + o_st[c:], 64, axis=1)
        new_st = []
        for j in tiles:
            u_sw = u_st[j][:c] + u_st[j][c:]
            delta = _dot(loc["bk_tail"][j], jnp.concatenate([_bf(u_sw), loc["v_sw"][j]], axis=0), TN)
            w_col = jnp.sum(jnp.where(eye, jnp.broadcast_to(loc["w_c"][j], (2 * c, 2 * c)), 0.0),
                            axis=1, keepdims=True)
            new_st.append(w_col * st[j] + jnp.where(anti, delta, 0.0))
        return new_st

    locs = [dict() for _ in range(WKV_SUB)]
    live = [local_part(s, locs[s]) for s in range(WKV_SUB)]
    while live:
        live = [g for g in live if next(g, "done") != "done"]
    st = [st_ref[j] for j in tiles]
    for s in range(WKV_SUB):
        st = state_part(s, locs[s], st)
    for j in tiles:
        st_ref[j] = st[j]


def _wkv(r, lw, k, v, kk, b, w_gate_up, w_down, w_out, l):
    t, wdt = r.shape
    rows = WKV_SUB * CHUNK
    steps = t // rows
    spec = pl.BlockSpec((rows, wdt), lambda i: (i, 0))

    def cast_specs(w, n_blocks):
        assert n_blocks <= steps and w.shape[1] % (16 * n_blocks) == 0
        blk = w.shape[1] // n_blocks
        return (pl.BlockSpec((None, blk, w.shape[2]), lambda i: (l, jnp.minimum(i, n_blocks - 1), 0)),
                pl.BlockSpec((blk, w.shape[2]), lambda i: (jnp.minimum(i, n_blocks - 1), 0)),
                jax.ShapeDtypeStruct(w.shape[1:], BF16))

    casts = [cast_specs(w_gate_up, 64), cast_specs(w_down, 32), cast_specs(w_out, 64)]
    return pl.pallas_call(
        _wkv_kernel,
        grid=(steps,),
        in_specs=[spec] * 6 + [c[0] for c in casts],
        out_specs=[spec] + [c[1] for c in casts],
        out_shape=[jax.ShapeDtypeStruct((t, wdt), F32)] + [c[2] for c in casts],
        scratch_shapes=[pltpu.VMEM((N_TILES, LANES, LANES), F32)],
        compiler_params=_params(dimension_semantics=("arbitrary",)),
        name="wkv",
    )(r, lw, k, v, kk, b, w_gate_up, w_down, w_out)


def _out_proj_kernel(ya_ref, o_ref, bon_ref, g_ref, gn_ref, ones_ref, w_ref, x_ref, mod_ref, ln_ref, out_ref,
                     *, parts):
    ones = ones_ref[...]
    pm = out_ref.shape[0] // parts
    slabs = [slice(j * SLAB, (j + 1) * SLAB) for j in range(RWKV_WIDTH // SLAB)]
    part_rows = [slice(p * pm, (p + 1) * pm) for p in range(parts)]
    oc = [[None] * len(slabs) for _ in range(parts)]
    var = [[None] * len(slabs) for _ in range(parts)]
    for p, rows in enumerate(part_rows):
        for j, sl in enumerate(slabs):
            o = o_ref[rows, sl]
            oc[p][j] = o - _head_sum(o, ones) * (1.0 / HEAD_DIM)
    for p in range(parts):
        for j in range(len(slabs)):
            var[p][j] = _mm(oc[p][j] * oc[p][j], ones) * (1.0 / HEAD_DIM)
    for p, rows in enumerate(part_rows):
        ys = []
        for j, sl in enumerate(slabs):
            y = ((oc[p][j] * lax.rsqrt(var[p][j] + RWKV_GN_EPS) * gn_ref[0:1, sl] + gn_ref[1:2, sl]
                  + bon_ref[rows, sl].astype(F32)) * g_ref[rows, sl].astype(F32))
            ys.append(_bf(y))
        mix = (_dot(ya_ref[rows, :], w_ref[:ATTN_WIDTH, :])
               + _dot(jnp.concatenate(ys, axis=1), w_ref[ATTN_WIDTH:, :]))
        z = DEEPNORM_ALPHA * x_ref[rows, :] + mod_ref[2:3, :] * mix
        out_ref[rows, :] = _layer_norm(z, ln_ref[0:1, :], ln_ref[1:2, :])


def _out_proj(y_attn, o, bonus, g, gn, ones, w_cat, x, mod, ln, l):
    t, d = x.shape
    tm = 512
    wdt = RWKV_WIDTH
    row = lambda width: pl.BlockSpec((tm, width), lambda i: (i, 0))
    layer = lambda a: pl.BlockSpec((None,) + a.shape[1:], lambda i: (l, 0, 0))
    return pl.pallas_call(
        functools.partial(_out_proj_kernel, parts=OUT_PARTS),
        grid=(t // tm,),
        in_specs=[row(ATTN_WIDTH), row(wdt), row(wdt), row(wdt), layer(gn),
                  pl.BlockSpec(ones.shape, lambda i: (0, 0)), pl.BlockSpec(w_cat.shape, lambda i: (0, 0)), row(d),
                  layer(mod), layer(ln)],
        out_specs=row(d),
        out_shape=jax.ShapeDtypeStruct((t, d), F32),
        compiler_params=_params(dimension_semantics=("parallel",)),
        name="out_proj",
    )(y_attn, o, bonus, g, gn, ones, w_cat, x, mod, ln)


def _ffn_kernel(x_ref, mod_ref, ln_ref, wg_ref, wu_ref, wd_ref, out_ref, u_ref, acc_ref):
    j = pl.program_id(1)

    @pl.when(j == 0)
    def _():
        u_ref[...] = _bf(x_ref[...] * (1.0 + mod_ref[4:5, :]) + mod_ref[3:4, :])
        acc_ref[...] = jnp.zeros_like(acc_ref)

    u = u_ref[...]
    gate = _dot(u, wg_ref[...])
    up = _dot(u, wu_ref[...])
    h = _bf(gate * jax.nn.sigmoid(gate) * up)
    acc_ref[...] += _dot(h, wd_ref[...])

    @pl.when(j == pl.num_programs(1) - 1)
    def _():
        z = DEEPNORM_ALPHA * x_ref[...] + mod_ref[5:6, :] * acc_ref[...]
        out_ref[...] = _layer_norm(z, ln_ref[2:3, :], ln_ref[3:4, :])


def _ffn(x, mod, ln, w_gu, w_dn, l):
    t, d = x.shape
    f = w_dn.shape[0]
    tm, tf = 512, 512
    nf = f // tf
    return pl.pallas_call(
        _ffn_kernel,
        grid=(t // tm, nf),
        in_specs=[pl.BlockSpec((tm, d), lambda i, j: (i, 0)),
                  pl.BlockSpec((None,) + mod.shape[1:], lambda i, j: (l, 0, 0)),
                  pl.BlockSpec((None,) + ln.shape[1:], lambda i, j: (l, 0, 0)),
                  pl.BlockSpec((d, tf), lambda i, j: (0, j)),
                  pl.BlockSpec((d, tf), lambda i, j: (0, j + nf)),
                  pl.BlockSpec((tf, d), lambda i, j: (j, 0))],
        out_specs=pl.BlockSpec((tm, d), lambda i, j: (i, 0)),
        out_shape=jax.ShapeDtypeStruct((t, d), F32),
        scratch_shapes=[pltpu.VMEM((tm, d), BF16), pltpu.VMEM((tm, d), F32)],
        compiler_params=_params(dimension_semantics=("parallel", "arbitrary")),
        name="ffn",
    )(x, mod, ln, w_gu, w_gu, w_dn)


def _pad_axis(a, axis, before, total):
    pads = [(0, 0)] * a.ndim
    pads[axis] = (before, total - before - a.shape[axis])
    return jnp.pad(a, pads)


def kernel(x, c, positions, w_ada, b_ada, w_in, attn_sinks, rwkv_mu, rwkv_w0, rwkv_w2, rwkv_a0, rwkv_a2, rwkv_g2, rwkv_k_k, rwkv_k_a, rwkv_r_k, rwkv_gn_w, rwkv_gn_b, rwkv_v0, rwkv_v1, rwkv_v2, w_out, ln1_w, ln1_b, w_gate_up, w_down, ln2_w, ln2_b):
    bsz, t, d = x.shape
    assert bsz == 1 and d == D_MODEL
    depth = w_in.shape[0]
    rw = RWKV_WIDTH
    inv =ROPE_THETA ** (-jnp.arange(0, HEAD_DIM, 2, dtype=F32) / HEAD_DIM)
    ang = positions[0].astype(F32)[:, None] * inv
    cos, sin = jnp.cos(ang), jnp.sin(ang)
    cos_t = jnp.concatenate([cos, cos, cos, cos], axis=-1)
    sin_t = jnp.concatenate([-sin, sin, -sin, sin], axis=-1)

    w_pad = _pad_axis(w_in, 2, 0, N_PROJ).astype(BF16)
    sinks = attn_sinks.reshape(-1)
    mu3 = rwkv_mu[:, :3 * rw].reshape(depth, 3, rw)
    mu_lora = _pad_axis(rwkv_mu[:, 3 * rw:], 1, 0, LORA_PAD).reshape(depth, 1, LORA_PAD)
    zero = jnp.zeros((depth, rw), F32)
    v0_all = _pad_axis(rwkv_v0, 0, 1, depth)
    params = jnp.stack([rwkv_w0, rwkv_a0, rwkv_k_k, rwkv_k_a, rwkv_r_k.reshape(depth, rw), v0_all, zero, zero],
                       axis=1)
    w2p = _pad_axis(rwkv_w2, 1, 0, LANES)
    a2p = _pad_axis(rwkv_a2, 1, DECAY_LORA, LANES)
    g2p = _pad_axis(rwkv_g2, 1, 0, 2 * LANES)
    v1p = _pad_axis(rwkv_v1, 2, 0, LANES)
    v2p = _pad_axis(rwkv_v2, 1, 0, LANES)
    seg = np.arange(SLAB) // HEAD_DIM
    ones = jnp.asarray(seg[:, None] == seg[None, :], BF16)
    gn = jnp.stack([rwkv_gn_w, rwkv_gn_b], axis=1)
    ln = jnp.stack([ln1_w, ln1_b, ln2_w, ln2_b], axis=1)

    mod = _adaln(c, w_ada, b_ada)
    xs = x[0]
    v_first = None
    for l in range(depth):
        proj_a, proj_r = _in_proj(xs, mod, w_pad, l)
        y_attn = _attention(proj_a, sinks, cos_t, sin_t, l)
        vres = None if l == 0 else (v_first, v1p, v2p)
        r, lw, k_mod, v, kk, b, g, bonus = _rwkv_prep(proj_a, proj_r, mu3, mu_lora, params, w2p, a2p, g2p, ones,
                                                      vres, l)
        if l == 0:
            v_first = v
        o, w_gu, w_dn, w_out_bf = _wkv(r, lw, k_mod, v, kk, b, w_gate_up, w_down, w_out, l)
        xs = _out_proj(y_attn, o, bonus, g, gn, ones, w_out_bf, xs, mod, ln, l)
        xs = _ffn(xs, mod, ln, w_gu, w_dn, l)
    return xs[None]
```
